```python
import math
import jax
import jax.numpy as jnp
from jax import lax
import numpy as np

D_MODEL = 1024
BATCH = 2
SEQ = 8192
DEPTH = 4
DEC_BATCH = 128
DEC_SEQ = 8
PAST_LEN = 2048
PAGE_SIZE = 128

HEAD_DIM = D_MODEL // 16
N_A_LAYERS = DEPTH // 2
N_B_LAYERS = DEPTH - N_A_LAYERS
A_HEADS = 12
A_KV = 4
A_REP = A_HEADS // A_KV
CMP_LEN = 32
SLC_LEN = 64
N_SEL = 16
WIN_A = 512
QB_A = 64
B_GROUPS = ((128, 1), (512, 4), (2048, 16))
N_B_GROUPS = 3
B_KV = 4
B_REP = 2
B_HEADS_PER_GROUP = B_KV * B_REP
WIN_B = 2048
QB_B = 128
MEM_LEN = 256
MEM_HEADS = 4
EPS = 1e-6
A_Q = A_HEADS * HEAD_DIM
A_KVW = 2 * A_KV * HEAD_DIM
MEM_W = MEM_HEADS * HEAD_DIM
A_SPLITS = (A_Q, A_KVW, A_KVW, A_KVW, 3 * A_HEADS, 3 * A_Q, MEM_W, MEM_W)
A_IN = A_Q + 3 * A_KVW + 3 * A_HEADS + 3 * A_Q + 2 * MEM_W
B_Q = N_B_GROUPS * B_HEADS_PER_GROUP * HEAD_DIM
B_O = B_HEADS_PER_GROUP * HEAD_DIM
B_SPLITS = (B_Q, B_O, MEM_W, MEM_W)
B_IN = B_Q + B_O + 2 * MEM_W
B_KVW = 2 * B_KV * HEAD_DIM

kernel_name = 'nsa_dilated_yoco_decoder'


def split_cols(p, sizes):
    return jnp.split(p, [int(i) for i in np.cumsum(sizes)[:-1]], axis=-1)


def rmsnorm(x, g):
    xf = x.astype(jnp.float32)
    y = xf * lax.rsqrt(jnp.mean(xf * xf, axis=-1, keepdims=True) + EPS)
    return (y * g.astype(jnp.float32)).astype(x.dtype)


def masked_softmax(s, mask):
    s = jnp.where(mask, s, -jnp.inf)
    m = jnp.max(s, axis=-1, keepdims=True)
    m = jnp.where(jnp.isfinite(m), m, 0.0)
    e = jnp.exp(s - m)
    den = jnp.sum(e, axis=-1, keepdims=True)
    p = e / jnp.where(den > 0, den, 1.0)
    return p, (m + jnp.log(den))[..., 0]


def alibi_slopes(n):
    return 2.0 ** (-8.0 * np.arange(1, n + 1) / n)


def dilated_slopes():
    base = alibi_slopes(B_HEADS_PER_GROUP)
    return np.stack([base / r for (_, r) in B_GROUPS]).reshape(N_B_GROUPS, B_KV, B_REP)


def compress(rows, pos_emb, w1, w2):
    lead = rows.shape[:-3]
    t, g, d = rows.shape[-3:]
    blocks = rows.reshape(lead + (t // CMP_LEN, CMP_LEN, g, d)) + pos_emb[:, None, :]
    h = jax.nn.silu(jnp.einsum('...nlgd,lde->...nge', blocks, w1))
    return jnp.einsum('...nge,ef->...ngf', h, w2)


def nsa_seq(q, pos, ck, cv, sk, sv, wk, wv, wpos, slopes):
    t, g, r, d = q.shape
    f32 = jnp.float32
    scale = d ** -0.5
    posf = pos.astype(f32)
    sl4 = slopes[None, :, :, None]
    nc = ck.shape[0]
    c_idx = jnp.arange(nc)
    c_end = c_idx * CMP_LEN + (CMP_LEN - 1)
    c_mid = c_idx.astype(f32) * CMP_LEN + 0.5 * (CMP_LEN - 1)
    s = jnp.einsum('tgrd,ngd->tgrn', q, ck, preferred_element_type=f32) * scale
    s = s - sl4 * (posf[:, None] - c_mid[None, :])[:, None, None, :]
    p_c, _ = masked_softmax(s, (c_end[None, :] <= pos[:, None])[:, None, None, :])
    o_cmp = jnp.einsum('tgrn,ngd->tgrd', p_c.astype(cv.dtype), cv)
    ns = sk.shape[0] // SLC_LEN
    per = SLC_LEN // CMP_LEN
    imp = jnp.sum(p_c, axis=2)
    imp = jnp.pad(imp, ((0, 0), (0, 0), (0, ns * per - nc))).reshape(t, g, ns, per).sum(-1)
    blk = jnp.arange(ns)[None, :]
    cur = (pos // SLC_LEN)[:, None]
    forced = (blk == 0) | (blk == cur) | (blk == cur - 1)
    imp = jnp.where(forced[:, None, :], A_REP + 1.0, imp)
    imp = jnp.where((blk <= cur)[:, None, :], imp, -1.0)
    _, idx = lax.top_k(imp, min(N_SEL, ns))
    k_sel = idx.shape[-1]
    g_ix = jnp.arange(g)[None, :, None]
    gk = sk.reshape(ns, SLC_LEN, g, d).transpose(2, 0, 1, 3)[g_ix, idx]
    gv = sv.reshape(ns, SLC_LEN, g, d).transpose(2, 0, 1, 3)[g_ix, idx]
    kpos = idx[..., None] * SLC_LEN + jnp.arange(SLC_LEN)
    s = jnp.einsum('tgrd,tgkld->tgrkl', q, gk, preferred_element_type=f32) * scale
    s = s - slopes[None, :, :, None, None] * (posf[:, None, None, None] - kpos.astype(f32))[:, :, None]
    mask = (kpos <= pos[:, None, None, None]).reshape(t, g, 1, k_sel * SLC_LEN)
    p_s, _ = masked_softmax(s.reshape(t, g, r, k_sel * SLC_LEN), mask)
    o_slc = jnp.einsum('tgrkl,tgkld->tgrd', p_s.reshape(t, g, r, k_sel, SLC_LEN).astype(sv.dtype), gv)
    dist = pos[:, None] - wpos[None, :]
    wmask = (dist >= 0) & (dist <= WIN_A) & (wpos >= 0)[None, :]
    s = jnp.einsum('tgrd,wgd->tgrw', q, wk, preferred_element_type=f32) * scale
    s = s - sl4 * dist.astype(f32)[:, None, None, :]
    p_w, _ = masked_softmax(s, wmask[:, None, None, :])
    o_win = jnp.einsum('tgrw,wgd->tgrd', p_w.astype(wv.dtype), wv)
    return o_cmp, o_slc, o_win


def nsa_prompt(q, kc, ks, kw, cmp_pos_l, cmp_w1_l, cmp_w2_l, slopes):
    n_b, s_len = q.shape[:2]
    ck = compress(kc[:, :, 0], cmp_pos_l[0], cmp_w1_l[0], cmp_w2_l[0])
    cv = compress(kc[:, :, 1], cmp_pos_l[1], cmp_w1_l[1], cmp_w2_l[1])
    wpad = jnp.pad(kw, ((0, 0), (WIN_A, 0), (0, 0), (0, 0), (0, 0)))
    core = jax.vmap(nsa_seq, in_axes=(0, None, 0, 0, 0, 0, 0, 0, None, None))

    def block(i):
        s0 = i * QB_A
        qb = lax.dynamic_slice_in_dim(q, s0, QB_A, axis=1)
        wb = lax.dynamic_slice_in_dim(wpad, s0, QB_A + WIN_A, axis=1)
        pos = s0 + jnp.arange(QB_A)
        wpos = s0 - WIN_A + jnp.arange(QB_A + WIN_A)
        return core(qb, pos, ck, cv, ks[:, :, 0], ks[:, :, 1], wb[:, :, 0], wb[:, :, 1], wpos, slopes)

    outs = lax.map(block, jnp.arange(s_len // QB_A))
    return [o.transpose(1, 0, 2, 3, 4, 5).reshape(n_b, s_len, A_Q) for o in outs]


def nsa_sample(q, kc, ks, kw, cache_cmp_kv, cache_slc_kv, cache_win_kv, page_table, layer, cmp_pos_l, cmp_w1_l, cmp_w2_l, slopes):
    n, t = q.shape[:2]
    past = page_table.shape[1] * cache_cmp_kv.shape[1]
    tk = past + t
    nc = tk // CMP_LEN
    ns = -(-tk // SLC_LEN)
    wbuf = cache_win_kv.shape[1]
    pos = past + jnp.arange(t)
    wpos = past - wbuf + jnp.arange(wbuf + t)

    def one(args):
        pt, qb, cn, sn, wb, wn = args
        cpast = cache_cmp_kv[pt][:, :, layer].reshape((past,) + cn.shape[1:])
        spast = cache_slc_kv[pt][:, :, layer].reshape((past,) + sn.shape[1:])
        kvc = jnp.concatenate([cpast, cn], 0)[: nc * CMP_LEN]
        ck = compress(kvc[:, 0], cmp_pos_l[0], cmp_w1_l[0], cmp_w2_l[0])
        cv = compress(kvc[:, 1], cmp_pos_l[1], cmp_w1_l[1], cmp_w2_l[1])
        kvs = jnp.pad(jnp.concatenate([spast, sn], 0), ((0, ns * SLC_LEN - tk), (0, 0), (0, 0), (0, 0)))
        kvw = jnp.concatenate([wb[:, layer], wn], 0)
        return nsa_seq(qb, pos, ck, cv, kvs[:, 0], kvs[:, 1], kvw[:, 0], kvw[:, 1], wpos, slopes)

    outs = lax.map(one, (page_table, q, kc, ks, cache_win_kv, kw))
    return [o.reshape(n, t, A_Q) for o in outs]


def combine_groups(outs, lses):
    w = jax.nn.softmax(jnp.stack(lses, 0), axis=0)
    return jnp.einsum('gnthj,gnthjd->nthjd', w.astype(outs[0].dtype), jnp.stack(outs, 0))


def dilated_prompt(q, kv, slopes):
    n_b, s_len = q.shape[:2]
    d = HEAD_DIM
    f32 = jnp.float32
    outs, lses = [], []
    for gi, (win, dil) in enumerate(B_GROUPS):
        wr = win // dil
        n = s_len // dil
        nb = -(-n // QB_B)
        n_pad = nb * QB_B
        qg = q[:, :, gi].reshape(n_b, n, dil, B_KV, B_REP, d).transpose(0, 2, 1, 3, 4, 5).reshape(n_b * dil, n, B_KV, B_REP, d)
        kg = kv.reshape(n_b, n, dil, 2, B_KV, d).transpose(0, 2, 1, 3, 4, 5).reshape(n_b * dil, n, 2, B_KV, d)
        qg = jnp.pad(qg, ((0, 0), (0, n_pad - n), (0, 0), (0, 0), (0, 0)))
        kg = jnp.pad(kg, ((0, 0), (wr, n_pad - n), (0, 0), (0, 0), (0, 0)))
        sl = (slopes[gi] * dil)[None, None, :, :, None]

        def block(i):
            a0 = i * QB_B
            qb = lax.dynamic_slice_in_dim(qg, a0, QB_B, axis=1)
            kb = lax.dynamic_slice_in_dim(kg, a0, QB_B + wr, axis=1)
            a = a0 + jnp.arange(QB_B)
            ak = a0 - wr + jnp.arange(QB_B + wr)
            dist = a[:, None] - ak[None, :]
            mask = (dist >= 0) & (dist <= wr) & (ak >= 0)[None, :]
            s = jnp.einsum('bthjd,bkhd->bthjk', qb, kb[:, :, 0], preferred_element_type=f32) * d ** -0.5
            s = s - sl * dist.astype(f32)[None, :, None, None, :]
            p, lse = masked_softmax(s, mask[None, :, None, None, :])
            o = jnp.einsum('bthjk,bkhd->bthjd', p.astype(kb.dtype), kb[:, :, 1])
            return o, lse

        o, lse = lax.map(block, jnp.arange(nb))
        o = o.transpose(1, 0, 2, 3, 4, 5).reshape(n_b * dil, n_pad, B_KV, B_REP, d)[:, :n]
        lse = lse.transpose(1, 0, 2, 3, 4).reshape(n_b * dil, n_pad, B_KV, B_REP)[:, :n]
        outs.append(o.reshape(n_b, dil, n, B_KV, B_REP, d).transpose(0, 2, 1, 3, 4, 5).reshape(n_b, s_len, B_KV, B_REP, d))
        lses.append(lse.reshape(n_b, dil, n, B_KV, B_REP).transpose(0, 2, 1, 3, 4).reshape(n_b, s_len, B_KV, B_REP))
    return combine_groups(outs, lses)


def dilated_sample(q, kv_new, buf, slopes):
    f32 = jnp.float32
    t = q.shape[1]
    wb = buf.shape[1]
    kcat = jnp.concatenate([buf, kv_new], 1)
    outs, lses = [], []
    for gi, (win, dil) in enumerate(B_GROUPS):
        steps = jnp.arange(win // dil + 1)
        idx = wb + jnp.arange(t)[:, None] - steps[None, :] * dil
        kg = kcat[:, jnp.maximum(idx, 0)]
        s = jnp.einsum('bthjd,btkhd->bthjk', q[:, :, gi], kg[:, :, :, 0], preferred_element_type=f32) * HEAD_DIM ** -0.5
        s = s - slopes[gi][None, None, :, :, None] * (steps * dil).astype(f32)
        p, lse = masked_softmax(s, (idx >= 0)[None, :, None, None, :])
        outs.append(jnp.einsum('bthjk,btkhd->bthjd', p.astype(kg.dtype), kg[:, :, :, 1]))
        lses.append(lse)
    return combine_groups(outs, lses)


def mem_kv(mem, g, w):
    n, m, _ = mem.shape
    return (rmsnorm(mem, g) @ w).reshape(n, m, 2, MEM_HEADS, HEAD_DIM)


def mem_attend(qm, mkv):
    s = jnp.einsum('nthd,nmhd->nthm', qm, mkv[:, :, 0], preferred_element_type=jnp.float32) * HEAD_DIM ** -0.5
    p = jax.nn.softmax(s, axis=-1)
    return jnp.einsum('nthm,nmhd->nthd', p.astype(mkv.dtype), mkv[:, :, 1])


def shared_kv(x, g, w):
    n, t, _ = x.shape
    return (rmsnorm(x, g) @ w).reshape(n, t, 2, B_KV, HEAD_DIM)


def a_project(x, g_pre, w_in):
    n, t, _ = x.shape
    q, kc, ks, kw, gate, z, qm, zm = split_cols(rmsnorm(x, g_pre) @ w_in, A_SPLITS)
    kvr = lambda a: a.reshape(n, t, 2, A_KV, HEAD_DIM)
    return (q.reshape(n, t, A_KV, A_REP, HEAD_DIM), kvr(kc), kvr(ks), kvr(kw),
            jax.nn.sigmoid(gate).reshape(n, t, 3, A_HEADS), z.reshape(n, t, 3, A_Q),
            qm.reshape(n, t, MEM_HEADS, HEAD_DIM), zm)


def a_merge(x, o_br, gate, z, om, zm, w_out, g_post):
    n, t, _ = x.shape
    o = (jnp.stack(o_br, 2) * jax.nn.silu(z)).reshape(n, t, 3, A_HEADS, HEAD_DIM)
    o = jnp.einsum('ntbhd,ntbh->nthd', o, gate).reshape(n, t, A_Q)
    om = om.reshape(n, t, MEM_W) * jax.nn.silu(zm)
    y = jnp.concatenate([o, om], -1) @ w_out
    return x + rmsnorm(y, g_post)


def b_project(x, g_pre, w_in):
    n, t, _ = x.shape
    q, z, qm, zm = split_cols(rmsnorm(x, g_pre) @ w_in, B_SPLITS)
    return (q.reshape(n, t, N_B_GROUPS, B_KV, B_REP, HEAD_DIM), z,
            qm.reshape(n, t, MEM_HEADS, HEAD_DIM), zm)


def b_merge(x, o, z, om, zm, w_out, g_post):
    n, t, _ = x.shape
    o = o.reshape(n, t, B_O) * jax.nn.silu(z)
    om = om.reshape(n, t, MEM_W) * jax.nn.silu(zm)
    y = jnp.concatenate([o, om], -1) @ w_out
    return x + rmsnorm(y, g_post)


def setup_inputs(seed: int = 0) -> dict:
    key = jax.random.key(seed)
    ks = jax.random.split(key, 24)
    f32 = jnp.float32
    n_pages = PAST_LEN // PAGE_SIZE
    n_used = DEC_BATCH * n_pages
    n_pool = n_used + max(1, n_used // 4)
    win_a_buf = min(WIN_A, PAST_LEN)
    win_b_buf = min(WIN_B, PAST_LEN)

    def nrm(k, shape, scale=1.0):
        return scale * jax.random.normal(k, shape, f32)

    def gain(k, shape):
        return 1.0 + 0.05 * jax.random.normal(k, shape, f32)

    page_table = jax.random.permutation(ks[8], n_pool)[:n_used].reshape(DEC_BATCH, n_pages).astype(jnp.int32)
    return {
        'x_prompt': nrm(ks[0], (BATCH, SEQ, D_MODEL)),
        'x_sample': nrm(ks[1], (DEC_BATCH, DEC_SEQ, D_MODEL)),
        'mem_prompt': nrm(ks[2], (BATCH, MEM_LEN, D_MODEL)),
        'cache_cmp_kv': nrm(ks[3], (n_pool, PAGE_SIZE, N_A_LAYERS, 2, A_KV, HEAD_DIM)),
        'cache_slc_kv': nrm(ks[4], (n_pool, PAGE_SIZE, N_A_LAYERS, 2, A_KV, HEAD_DIM)),
        'cache_win_kv': nrm(ks[5], (DEC_BATCH, win_a_buf, N_A_LAYERS, 2, A_KV, HEAD_DIM)),
        'cache_dil_kv': nrm(ks[6], (DEC_BATCH, win_b_buf, 2, B_KV, HEAD_DIM)),
        'cache_mem_kv': nrm(ks[7], (DEC_BATCH, MEM_LEN, DEPTH, 2, MEM_HEADS, HEAD_DIM)),
        'page_table': page_table,
        'g_pre': gain(ks[9], (DEPTH, D_MODEL)),
        'g_post': gain(ks[10], (DEPTH, D_MODEL)),
        'g_mem': gain(ks[11], (DEPTH, D_MODEL)),
        'w_mem_kv': nrm(ks[12], (DEPTH, D_MODEL, 2 * MEM_W), D_MODEL ** -0.5),
        'w_in_a': nrm(ks[13], (N_A_LAYERS, D_MODEL, A_IN), D_MODEL ** -0.5),
        'w_out_a': nrm(ks[14], (N_A_LAYERS, A_Q + MEM_W, D_MODEL), (A_Q + MEM_W) ** -0.5),
        'cmp_pos': nrm(ks[15], (N_A_LAYERS, 2, CMP_LEN, HEAD_DIM), 0.1),
        'cmp_w1': nrm(ks[16], (N_A_LAYERS, 2, CMP_LEN, HEAD_DIM, HEAD_DIM), (CMP_LEN * HEAD_DIM) ** -0.5),
        'cmp_w2': nrm(ks[17], (N_A_LAYERS, 2, HEAD_DIM, HEAD_DIM), HEAD_DIM ** -0.5),
        'g_kv_b': gain(ks[18], (D_MODEL,)),
        'w_kv_b': nrm(ks[19], (D_MODEL, B_KVW), D_MODEL ** -0.5),
        'w_in_b': nrm(ks[20], (N_B_LAYERS, D_MODEL, B_IN), D_MODEL ** -0.5),
        'w_out_b': nrm(ks[21], (N_B_LAYERS, B_O + MEM_W, D_MODEL), (B_O + MEM_W) ** -0.5),
    }


def reference(x_prompt, x_sample, mem_prompt, cache_cmp_kv, cache_slc_kv, cache_win_kv, cache_dil_kv, cache_mem_kv,
              page_table, g_pre, g_post, g_mem, w_mem_kv, w_in_a, w_out_a, cmp_pos, cmp_w1, cmp_w2,
              g_kv_b, w_kv_b, w_in_b, w_out_b):
    slopes_a = jnp.asarray(alibi_slopes(A_HEADS), jnp.float32).reshape(A_KV, A_REP)
    slopes_b = jnp.asarray(dilated_slopes(), jnp.float32)
    s_len = x_prompt.shape[1]
    xp, xs = x_prompt, x_sample
    cmp_p, slc_p, win_p, cmp_s, slc_s, win_s, memkv_p = [], [], [], [], [], [], []
    kvb_p = None
    kvb_s = None
    for l in range(DEPTH):
        mkv_p = mem_kv(mem_prompt, g_mem[l], w_mem_kv[l])
        mkv_s = cache_mem_kv[:, :, l]
        memkv_p.append(mkv_p)
        if l < N_A_LAYERS:
            q, kc, ks, kw, gate, z, qm, zm = a_project(xp, g_pre[l], w_in_a[l])
            o_br = nsa_prompt(q, kc, ks, kw, cmp_pos[l], cmp_w1[l], cmp_w2[l], slopes_a)
            xp = a_merge(xp, o_br, gate, z, mem_attend(qm, mkv_p), zm, w_out_a[l], g_post[l])
            cmp_p.append(kc)
            slc_p.append(ks)
            win_p.append(kw[:, -min(WIN_A, s_len):])
            q, kc, ks, kw, gate, z, qm, zm = a_project(xs, g_pre[l], w_in_a[l])
            o_br = nsa_sample(q, kc, ks, kw, cache_cmp_kv, cache_slc_kv, cache_win_kv, page_table, l,
                              cmp_pos[l], cmp_w1[l], cmp_w2[l], slopes_a)
            xs = a_merge(xs, o_br, gate, z, mem_attend(qm, mkv_s), zm, w_out_a[l], g_post[l])
            cmp_s.append(kc)
            slc_s.append(ks)
            win_s.append(jnp.concatenate([cache_win_kv[:, :, l], kw], 1)[:, -cache_win_kv.shape[1]:])
            if l == N_A_LAYERS - 1:
                kvb_p = shared_kv(xp, g_kv_b, w_kv_b)
                kvb_s = shared_kv(xs, g_kv_b, w_kv_b)
        else:
            lb = l - N_A_LAYERS
            q, z, qm, zm = b_project(xp, g_pre[l], w_in_b[lb])
            o = dilated_prompt(q, kvb_p, slopes_b)
            xp = b_merge(xp, o, z, mem_attend(qm, mkv_p), zm, w_out_b[lb], g_post[l])
            q, z, qm, zm = b_project(xs, g_pre[l], w_in_b[lb])
            o = dilated_sample(q, kvb_s, cache_dil_kv, slopes_b)
            xs = b_merge(xs, o, z, mem_attend(qm, mkv_s), zm, w_out_b[lb], g_post[l])
    new_cmp_p = jnp.stack(cmp_p, 2)
    new_cmp_s = jnp.stack(cmp_s, 2)
    new_slc_p = jnp.stack(slc_p, 2)
    new_slc_s = jnp.stack(slc_s, 2)
    new_win_p = jnp.stack(win_p, 2)
    new_win_s = jnp.stack(win_s, 2)
    new_dil_p = kvb_p[:, -min(WIN_B, s_len):]
    new_dil_s = jnp.concatenate([cache_dil_kv, kvb_s], 1)[:, -cache_dil_kv.shape[1]:]
    new_mem_p = jnp.stack(memkv_p, 2)
    return (xp, xs, new_cmp_p, new_cmp_s, new_slc_p, new_slc_s, new_win_p, new_win_s, new_dil_p, new_dil_s, new_mem_p)
```

```python
import functools
import numpy as np
import jax
import jax.numpy as jnp
from jax import lax
from jax.experimental import pallas as pl
from jax.experimental.pallas import tpu as pltpu

F32 = jnp.float32
BF16 = jnp.bfloat16

HEAD_DIM = 64
LANES = 128
A_HEADS = 12
A_KV = 4
A_REP = A_HEADS // A_KV
CMP_LEN = 32
SLC_LEN = 64
N_SEL = 16
WIN_A = 512
B_GROUPS = ((128, 1), (512, 4), (2048, 16))
N_B_GROUPS = 3
B_KV = 4
B_REP = 2
B_HPG = B_KV * B_REP
MEM_HEADS = 4
EPS = 1e-6
A_Q = A_HEADS * HEAD_DIM
A_KVW = 2 * A_KV * HEAD_DIM
MEM_W = MEM_HEADS * HEAD_DIM
B_Q = N_B_GROUPS * B_HPG * HEAD_DIM
B_O = B_HPG * HEAD_DIM
QSCALE = HEAD_DIM ** -0.5
NEG = -1e30
VMEM_LIMIT = 56 * 1024 * 1024


def _alibi(n):
    return [float(2.0 ** (-8.0 * i / n)) for i in range(1, n + 1)]


SLOPES_A = _alibi(A_HEADS)
SLOPES_B = _alibi(B_HPG)


def _cparams(sem):
    return pltpu.CompilerParams(dimension_semantics=sem, vmem_limit_bytes=VMEM_LIMIT)


def _silu(x):
    return x * (1.0 / (1.0 + jnp.exp(-x)))


def _dot_nt(a, b):
    return lax.dot_general(a, b, (((1,), (1,)), ((), ())), preferred_element_type=F32)


def _dot(a, b):
    return jnp.dot(a, b, preferred_element_type=F32)


def _pick_half(lane, a, b):
    return jnp.where(lane < HEAD_DIM, a, b)


def _assemble(pieces, rows):
    lane = lax.broadcasted_iota(jnp.int32, (rows, LANES), 1)
    cols = []
    for c in range(len(pieces) // 2):
        halves = []
        for k in (0, 1):
            arr, useful = pieces[2 * c + k]
            if useful != k:
                arr = pltpu.roll(arr, HEAD_DIM, 1)
            halves.append(arr)
        cols.append(_pick_half(lane, halves[0], halves[1]))
    return jnp.concatenate(cols, axis=1)


def _proj_kernel(x_ref, g_ref, w_ref, *out_refs, plan):
    x = x_ref[...]
    ms = jnp.mean(x * x, axis=-1, keepdims=True)
    xn = ((x * lax.rsqrt(ms + EPS)) * g_ref[...]).astype(BF16)
    for (c0, c1, dests) in plan:
        y = _dot(xn, w_ref[:, c0:c1])
        for (oi, o0, scale) in dests:
            v = y if scale == 1.0 else y * scale
            out_refs[oi][:, o0:o0 + (c1 - c0)] = v.astype(out_refs[oi].dtype)


def _proj(x2d, g, w_bf, outs, tm=256, chunk=512):
    m, d = x2d.shape
    n = w_bf.shape[1]
    tm = min(tm, m)
    edges = sorted({0, n} | {o[0] for o in outs} | {o[0] + o[1] for o in outs})
    plan = []
    for a, b in zip(edges[:-1], edges[1:]):
        for c0 in range(a, b, chunk):
            c1 = min(c0 + chunk, b)
            dests = tuple((oi, c0 - o[0], float(o[3])) for oi, o in enumerate(outs)
                          if o[0] <= c0 and c1 <= o[0] + o[1])
            if dests:
                plan.append((c0, c1, dests))
    return pl.pallas_call(
        functools.partial(_proj_kernel, plan=tuple(plan)),
        grid=(m // tm,),
        in_specs=[pl.BlockSpec((tm, d), lambda i: (i, 0)),
                  pl.BlockSpec((1, d), lambda i: (0, 0)),
                  pl.BlockSpec((d, n), lambda i: (0, 0))],
        out_specs=[pl.BlockSpec((tm, o[1]), lambda i: (i, 0)) for o in outs],
        out_shape=[jax.ShapeDtypeStruct((m, o[1]), o[2]) for o in outs],
        compiler_params=_cparams(("arbitrary",)),
        name="proj",
    )(x2d, g.reshape(1, d), w_bf)


def _compress_kernel(x_ref, pos_ref, w1_ref, w2_ref, o_ref, acc_ref):
    l = pl.program_id(1)

    @pl.when(l == 0)
    def _():
        acc_ref[...] = jnp.zeros_like(acc_ref)

    x = x_ref[...]
    for kv in range(2):
        for p in range(2):
            c0 = kv * 256 + p * LANES
            xb = (x[:, c0:c0 + LANES] + pos_ref[kv:kv + 1, :]).astype(BF16)
            acc_ref[2 * kv + p] += _dot(xb, w1_ref[kv])

    @pl.when(l == CMP_LEN - 1)
    def _():
        for kv in range(2):
            for p in range(2):
                c0 = kv * 256 + p * LANES
                h = _silu(acc_ref[2 * kv + p]).astype(BF16)
                o_ref[:, c0:c0 + LANES] = _dot(h, w2_ref[kv])


def _compress(src2d, col_blocks, col_off, pos2, w1bd, w2bd, r):
    nb = src2d.shape[0]
    r = min(r, nb)
    return pl.pallas_call(
        _compress_kernel,
        grid=(nb // r, CMP_LEN),
        in_specs=[pl.BlockSpec((r, 512), lambda i, l: (i, l * col_blocks + col_off)),
                  pl.BlockSpec((None, 2, LANES), lambda i, l: (l, 0, 0)),
                  pl.BlockSpec((None, 2, LANES, LANES), lambda i, l: (l, 0, 0, 0)),
                  pl.BlockSpec((2, LANES, LANES), lambda i, l: (0, 0, 0))],
        out_specs=pl.BlockSpec((r, 512), lambda i, l: (i, 0)),
        out_shape=jax.ShapeDtypeStruct((nb, 512), F32),
        scratch_shapes=[pltpu.VMEM((4, r, LANES), F32)],
        compiler_params=_cparams(("arbitrary", "arbitrary")),
        name="compress",
    )(src2d, pos2, w1bd, w2bd)


def _topk_mask_t(imp_t, k):
    nb, cols = imp_t.shape
    row = lax.broadcasted_iota(jnp.int32, (nb, cols), 0)
    work = imp_t
    sel = jnp.zeros((nb, cols), F32)
    for _ in range(k):
        m = jnp.max(work, axis=0, keepdims=True)
        idx = jnp.min(jnp.where(work == m, row, nb), axis=0, keepdims=True)
        hit = row == idx
        sel = jnp.where(hit, 1.0, sel)
        work = jnp.where(hit, -3.0, work)
    return sel


def _group_q(q_ref, g, rows):
    return jnp.concatenate(
        [q_ref[:, (g * A_REP + r) * LANES:(g * A_REP + r + 1) * LANES] for r in range(A_REP)], axis=0)


def _nsa_cmp_kernel(q_ref, ckv_ref, o_ref, sel_ref, *, tq, nc, ns):
    i = pl.program_id(1)
    q0 = i * tq
    half = nc // 2
    pos = q0 + lax.broadcasted_iota(jnp.int32, (tq, 1), 0)
    posf = pos.astype(F32)
    tok = lax.broadcasted_iota(jnp.int32, (1, nc), 1)
    cidx = jnp.where(tok < half, 2 * tok, 2 * (tok - half) + 1)
    c_end = cidx * CMP_LEN + (CMP_LEN - 1)
    c_mid = cidx.astype(F32) * CMP_LEN + 0.5 * (CMP_LEN - 1)
    cmask = c_end <= pos
    dist = posf - c_mid
    blk = lax.broadcasted_iota(jnp.int32, (1, ns), 1)
    cur = jnp.right_shift(pos, 6)
    forced = (blk == 0) | (blk == cur) | (blk == cur - 1)
    allowed = blk <= cur
    ck = ckv_ref[:, 0:256].astype(BF16)
    cv = ckv_ref[:, 256:512].astype(BF16)
    pieces = []
    for g in range(A_KV):
        p = g // 2
        s = _dot_nt(_group_q(q_ref, g, tq), ck[:, p * LANES:(p + 1) * LANES])
        parts = []
        for r in range(A_REP):
            sr = s[r * tq:(r + 1) * tq, :] - SLOPES_A[g * A_REP + r] * dist
            parts.append(jnp.where(cmask, sr, -jnp.inf))
        sm = jnp.concatenate(parts, axis=0)
        m = jnp.max(sm, axis=-1, keepdims=True)
        m = jnp.where(m == -jnp.inf, 0.0, m)
        e = jnp.exp(sm - m)
        den = jnp.sum(e, axis=-1, keepdims=True)
        pc = e / jnp.where(den > 0, den, 1.0)
        og = _dot(pc.astype(BF16), cv[:, p * LANES:(p + 1) * LANES])
        for r in range(A_REP):
            pieces.append((og[r * tq:(r + 1) * tq, :], g % 2))
        ps = pc[0:tq] + pc[tq:2 * tq] + pc[2 * tq:3 * tq]
        imp = ps[:, 0:half] + ps[:, half:nc]
        imp = jnp.where(forced, A_REP + 1.0, imp)
        imp = jnp.where(allowed, imp, -1.0)
        sel_t = _topk_mask_t(imp.T, min(N_SEL, ns))
        sel_ref[:, g * ns:(g + 1) * ns] = sel_t.T.astype(BF16)
    o_ref[...] = _assemble(pieces, tq)


def _nsa_cmp(q_exp, ckv, n_b, s_len, tq):
    nc = ckv.shape[1]
    ns = s_len // SLC_LEN
    assert nc == 2 * ns
    return pl.pallas_call(
        functools.partial(_nsa_cmp_kernel, tq=tq, nc=nc, ns=ns),
        grid=(n_b, s_len // tq),
        in_specs=[pl.BlockSpec((None, tq, A_HEADS * LANES), lambda b, i: (b, i, 0)),
                  pl.BlockSpec((None, nc, 512), lambda b, i: (b, 0, 0))],
        out_specs=[pl.BlockSpec((None, tq, A_Q), lambda b, i: (b, i, 0)),
                   pl.BlockSpec((None, tq, A_KV * ns), lambda b, i: (b, i, 0))],
        out_shape=[jax.ShapeDtypeStruct((n_b, s_len, A_Q), F32),
                   jax.ShapeDtypeStruct((n_b, s_len, A_KV * ns), BF16)],
        compiler_params=_cparams(("arbitrary", "arbitrary")),
        name="nsa_cmp",
    )(q_exp, ckv)


def _nsa_sw_kernel(q_ref, kv_ref, sel_ref, et_ref, oslc_ref, owin_ref,
                   m_ref, l_ref, acc_ref, *, tq, tk, ns, nkw):
    i = pl.program_id(1)
    q0 = i * tq
    trow = lax.broadcasted_iota(jnp.int32, (tq, 1), 0)
    qg = [_group_q(q_ref, g, tq) for g in range(A_KV)]

    m_ref[...] = jnp.full(m_ref.shape, NEG, F32)
    l_ref[...] = jnp.zeros(l_ref.shape, F32)
    acc_ref[...] = jnp.zeros(acc_ref.shape, F32)

    def body(j, carry):
        k0 = pl.multiple_of(j * tk, tk)
        krel = (k0 - q0) + lax.broadcasted_iota(jnp.int32, (1, tk), 1)
        causal = krel <= trow
        krelf = krel.astype(F32)
        et = et_ref[pl.ds(k0, tk), :]
        for g in range(A_KV):
            p = g // 2
            kk = kv_ref[pl.ds(k0, tk), p * LANES:(p + 1) * LANES]
            vv = kv_ref[pl.ds(k0, tk), 256 + p * LANES:256 + (p + 1) * LANES]
            s = _dot_nt(qg[g], kk)
            selk = _dot_nt(sel_ref[:, g * ns:(g + 1) * ns], et)
            valid = (selk > 0.5) & causal
            parts = []
            for r in range(A_REP):
                sr = s[r * tq:(r + 1) * tq, :] + SLOPES_A[g * A_REP + r] * krelf
                parts.append(jnp.where(valid, sr, NEG))
            sm = jnp.concatenate(parts, axis=0)
            m_old = m_ref[g]
            m_new = jnp.maximum(m_old, jnp.max(sm, axis=-1, keepdims=True))
            pe = jnp.exp(sm - m_new)
            alpha = jnp.exp(m_old - m_new)
            l_ref[g] = alpha * l_ref[g] + jnp.sum(pe, axis=-1, keepdims=True)
            acc_ref[g] = alpha * acc_ref[g] + _dot(pe.astype(BF16), vv)
            m_ref[g] = m_new
        return carry

    lax.fori_loop(0, (q0 + tq + tk - 1) // tk, body, 0)
    pieces = []
    for g in range(A_KV):
        og = acc_ref[g] / l_ref[g]
        for r in range(A_REP):
            pieces.append((og[r * tq:(r + 1) * tq, :], g % 2))
    oslc_ref[...] = _assemble(pieces, tq)

    w0 = pl.multiple_of(jnp.maximum(q0 - WIN_A, 0), tq)
    wrel = (w0 - q0) + lax.broadcasted_iota(jnp.int32, (1, nkw), 1)
    dist = trow - wrel
    wvalid = (dist >= 0) & (dist <= WIN_A)
    wrelf = wrel.astype(F32)
    pieces = []
    for g in range(A_KV):
        p = g // 2
        kk = kv_ref[pl.ds(w0, nkw), 512 + p * LANES:512 + (p + 1) * LANES]
        vv = kv_ref[pl.ds(w0, nkw), 768 + p * LANES:768 + (p + 1) * LANES]
        s = _dot_nt(qg[g], kk)
        parts = []
        for r in range(A_REP):
            sr = s[r * tq:(r + 1) * tq, :] + SLOPES_A[g * A_REP + r] * wrelf
            parts.append(jnp.where(wvalid, sr, NEG))
        sm = jnp.concatenate(parts, axis=0)
        m = jnp.max(sm, axis=-1, keepdims=True)
        pe = jnp.exp(sm - m)
        den = jnp.sum(pe, axis=-1, keepdims=True)
        og = _dot(pe.astype(BF16), vv) / den
        for r in range(A_REP):
            pieces.append((og[r * tq:(r + 1) * tq, :], g % 2))
    owin_ref[...] = _assemble(pieces, tq)


def _nsa_sw(q_exp, kvb, sel, et, n_b, s_len, tq, tk):
    ns = s_len // SLC_LEN
    nkw = WIN_A + tq
    assert s_len >= nkw and s_len % tk == 0 and tk % tq == 0
    return pl.pallas_call(
        functools.partial(_nsa_sw_kernel, tq=tq, tk=tk, ns=ns, nkw=nkw),
        grid=(n_b, s_len // tq),
        in_specs=[pl.BlockSpec((None, tq, A_HEADS * LANES), lambda b, i: (b, i, 0)),
                  pl.BlockSpec((None, s_len, 1024), lambda b, i: (b, 0, 0)),
                  pl.BlockSpec((None, tq, A_KV * ns), lambda b, i: (b, i, 0)),
                  pl.BlockSpec((s_len, ns), lambda b, i: (0, 0))],
        out_specs=[pl.BlockSpec((None, tq, A_Q), lambda b, i: (b, i, 0)),
                   pl.BlockSpec((None, tq, A_Q), lambda b, i: (b, i, 0))],
        out_shape=[jax.ShapeDtypeStruct((n_b, s_len, A_Q), F32),
                   jax.ShapeDtypeStruct((n_b, s_len, A_Q), F32)],
        scratch_shapes=[pltpu.VMEM((A_KV, A_REP * tq, 1), F32),
                        pltpu.VMEM((A_KV, A_REP * tq, 1), F32),
                        pltpu.VMEM((A_KV, A_REP * tq, LANES), F32)],
        compiler_params=_cparams(("arbitrary", "arbitrary")),
        name="nsa_sw",
    )(q_exp, kvb, sel, et)


def _mem_kernel(qm_ref, mkv_ref, o_ref, *, tt):
    lane = lax.broadcasted_iota(jnp.int32, (tt, LANES), 1)
    cols = []
    for c in range(MEM_HEADS // 2):
        qc = qm_ref[:, c * LANES:(c + 1) * LANES]
        mk = mkv_ref[:, c * LANES:(c + 1) * LANES].astype(BF16)
        mv = mkv_ref[:, MEM_W + c * LANES:MEM_W + (c + 1) * LANES].astype(BF16)
        halves = []
        for k in (0, 1):
            qh = jnp.where((lane < HEAD_DIM) == (k == 0), qc, 0.0).astype(BF16)
            s = _dot_nt(qh, mk) * QSCALE
            m = jnp.max(s, axis=-1, keepdims=True)
            e = jnp.exp(s - m)
            den = jnp.sum(e, axis=-1, keepdims=True)
            halves.append(_dot(e.astype(BF16), mv) / den)
        cols.append(_pick_half(lane, halves[0], halves[1]))
    o_ref[...] = jnp.concatenate(cols, axis=1)


def _mem_attend(rest, qm_blk, mkv3, mkv_blk, n, t, tt):
    mem = mkv3.shape[1]
    steps = t // tt
    return pl.pallas_call(
        functools.partial(_mem_kernel, tt=tt),
        grid=(n, steps),
        in_specs=[pl.BlockSpec((tt, MEM_W), lambda b, i: (b * steps + i, qm_blk)),
                  pl.BlockSpec((None, mem, 2 * MEM_W), lambda b, i: (b, 0, mkv_blk))],
        out_specs=pl.BlockSpec((tt, MEM_W), lambda b, i: (b * steps + i, 0)),
        out_shape=jax.ShapeDtypeStruct((n * t, MEM_W), F32),
        compiler_params=_cparams(("arbitrary", "arbitrary")),
        name="mem_attend",
    )(rest, mkv3)


def _post(x_ref, y, gp_ref, out_ref):
    ms = jnp.mean(y * y, axis=-1, keepdims=True)
    out_ref[...] = x_ref[...] + (y * lax.rsqrt(ms + EPS)) * gp_ref[...]


def _merge_a_kernel(x_ref, oc_ref, os_ref, ow_ref, z_ref, zm_ref, gate_ref, om_ref,
                    w_ref, gp_ref, eg_ref, out_ref):
    gs = 1.0 / (1.0 + jnp.exp(-gate_ref[...]))
    hi = gs.astype(BF16)
    r1 = gs - hi.astype(F32)
    mid = r1.astype(BF16)
    lo = (r1 - mid.astype(F32)).astype(BF16)
    eg = eg_ref[...]
    gexp = _dot(hi, eg) + _dot(mid, eg) + _dot(lo, eg)
    o = jnp.zeros(oc_ref.shape, F32)
    for b, ob_ref in enumerate((oc_ref, os_ref, ow_ref)):
        o = o + (ob_ref[...] * _silu(z_ref[:, b * A_Q:(b + 1) * A_Q])) * gexp[:, b * A_Q:(b + 1) * A_Q]
    om = om_ref[...] * _silu(zm_ref[...])
    y = _dot(o.astype(BF16), w_ref[0:A_Q, :]) + _dot(om.astype(BF16), w_ref[A_Q:A_Q + MEM_W, :])
    _post(x_ref, y, gp_ref, out_ref)


def _merge_a(x2d, o_cmp, o_slc, o_win, rest, om, w_bf, g_post, egate, tm=256):
    m, d = x2d.shape
    tm = min(tm, m)
    row = lambda i: (i, 0)
    return pl.pallas_call(
        _merge_a_kernel,
        grid=(m // tm,),
        in_specs=[pl.BlockSpec((tm, d), row),
                  pl.BlockSpec((tm, A_Q), row), pl.BlockSpec((tm, A_Q), row), pl.BlockSpec((tm, A_Q), row),
                  pl.BlockSpec((tm, 3 * A_Q), row),
                  pl.BlockSpec((tm, MEM_W), lambda i: (i, 10)),
                  pl.BlockSpec((tm, LANES), lambda i: (i, 22)),
                  pl.BlockSpec((tm, MEM_W), row),
                  pl.BlockSpec((A_Q + MEM_W, d), lambda i: (0, 0)),
                  pl.BlockSpec((1, d), lambda i: (0, 0)),
                  pl.BlockSpec((LANES, 3 * A_Q), lambda i: (0, 0))],
        out_specs=pl.BlockSpec((tm, d), row),
        out_shape=jax.ShapeDtypeStruct((m, d), F32),
        compiler_params=_cparams(("arbitrary",)),
        name="merge_a",
    )(x2d, o_cmp, o_slc, o_win, rest, rest, rest, om, w_bf, g_post.reshape(1, d), egate)


def _merge_b_kernel(x_ref, o0_ref, o1_ref, o2_ref, l0_ref, l1_ref, l2_ref, z_ref, zm_ref, om_ref,
                    w_ref, gp_ref, out_ref):
    l0, l1, l2 = l0_ref[...], l1_ref[...], l2_ref[...]
    mx = jnp.maximum(jnp.maximum(l0, l1), l2)
    e0, e1, e2 = jnp.exp(l0 - mx), jnp.exp(l1 - mx), jnp.exp(l2 - mx)
    den = e0 + e1 + e2
    o = (e0 / den) * o0_ref[...] + (e1 / den) * o1_ref[...] + (e2 / den) * o2_ref[...]
    o = o * _silu(z_ref[...])
    om = om_ref[...] * _silu(zm_ref[...])
    y = _dot(o.astype(BF16), w_ref[0:B_O, :]) + _dot(om.astype(BF16), w_ref[B_O:B_O + MEM_W, :])
    _post(x_ref, y, gp_ref, out_ref)


def _merge_b(x2d, outs, lses, rest, om, w_bf, g_post, tm=256):
    m, d = x2d.shape
    tm = min(tm, m)
    row = lambda i: (i, 0)
    o_spec = pl.BlockSpec((tm, B_O), row)
    return pl.pallas_call(
        _merge_b_kernel,
        grid=(m // tm,),
        in_specs=[pl.BlockSpec((tm, d), row)] + [o_spec] * 6 +
                 [pl.BlockSpec((tm, B_O), row),
                  pl.BlockSpec((tm, MEM_W), lambda i: (i, 3)),
                  pl.BlockSpec((tm, MEM_W), row),
                  pl.BlockSpec((B_O + MEM_W, d), lambda i: (0, 0)),
                  pl.BlockSpec((1, d), lambda i: (0, 0))],
        out_specs=pl.BlockSpec((tm, d), row),
        out_shape=jax.ShapeDtypeStruct((m, d), F32),
        compiler_params=_cparams(("arbitrary",)),
        name="merge_b",
    )(x2d, *outs, *lses, rest, rest, om, w_bf, g_post.reshape(1, d))


def _dil_kernel(q_ref, kc_ref, kp_ref, o_ref, lse_ref, *, tn, wr):
    i = pl.program_id(2)
    nk = wr + tn
    t = lax.broadcasted_iota(jnp.int32, (tn, 1), 0)
    k = lax.broadcasted_iota(jnp.int32, (1, nk), 1)
    dist = t - k + wr
    valid = (dist >= 0) & (dist <= wr) & ((k >= wr) | (i > 0))
    distf = dist.astype(F32)
    lane = lax.broadcasted_iota(jnp.int32, (tn, LANES), 1)
    o_cols, l_cols = [], []
    for h in range(B_KV):
        p = h // 2
        kk = jnp.concatenate([kp_ref[:, p * LANES:(p + 1) * LANES], kc_ref[:, p * LANES:(p + 1) * LANES]], axis=0)
        vv = jnp.concatenate([kp_ref[:, 256 + p * LANES:256 + (p + 1) * LANES],
                              kc_ref[:, 256 + p * LANES:256 + (p + 1) * LANES]], axis=0)
        qh = jnp.concatenate([q_ref[:, (h * B_REP + j) * LANES:(h * B_REP + j + 1) * LANES]
                              for j in range(B_REP)], axis=0)
        s = _dot_nt(qh, kk)
        parts = []
        for j in range(B_REP):
            sj = s[j * tn:(j + 1) * tn, :] - SLOPES_B[h * B_REP + j] * distf
            parts.append(jnp.where(valid, sj, NEG))
        sm = jnp.concatenate(parts, axis=0)
        m = jnp.max(sm, axis=-1, keepdims=True)
        e = jnp.exp(sm - m)
        den = jnp.sum(e, axis=-1, keepdims=True)
        og = _dot(e.astype(BF16), vv) / den
        lse = m + jnp.log(den)
        halves = []
        for j in range(B_REP):
            piece = og[j * tn:(j + 1) * tn, :]
            if (h % 2) != j:
                piece = pltpu.roll(piece, HEAD_DIM, 1)
            halves.append(piece)
        o_cols.append(_pick_half(lane, halves[0], halves[1]))
        l_cols.append(_pick_half(lane, jnp.broadcast_to(lse[0:tn], (tn, LANES)),
                                 jnp.broadcast_to(lse[tn:2 * tn], (tn, LANES))))
    o_ref[...] = jnp.concatenate(o_cols, axis=1)
    lse_ref[...] = jnp.concatenate(l_cols, axis=1)


def _dilated_prompt(q_exp, kvb, gi, n_b, s_len, tn):
    win, dil = B_GROUPS[gi]
    wr = win // dil
    n = s_len // dil
    tn = min(tn, n)
    assert n % tn == 0 and tn % wr == 0
    qv = q_exp.reshape(n_b, n, dil * N_B_GROUPS * B_HPG * LANES)
    kv = kvb.reshape(n_b, n, dil * 512)
    ratio = tn // wr
    o, lse = pl.pallas_call(
        functools.partial(_dil_kernel, tn=tn, wr=wr),
        grid=(n_b, dil, n // tn),
        in_specs=[pl.BlockSpec((None, tn, B_HPG * LANES), lambda b, c, i: (b, i, c * N_B_GROUPS + gi)),
                  pl.BlockSpec((None, tn, 512), lambda b, c, i: (b, i, c)),
                  pl.BlockSpec((None, wr, 512), lambda b, c, i: (b, jnp.maximum(i * ratio - 1, 0), c))],
        out_specs=[pl.BlockSpec((None, tn, B_O), lambda b, c, i: (b, i, c)),
                   pl.BlockSpec((None, tn, B_O), lambda b, c, i: (b, i, c))],
        out_shape=[jax.ShapeDtypeStruct((n_b, n, dil * B_O), F32),
                   jax.ShapeDtypeStruct((n_b, n, dil * B_O), F32)],
        compiler_params=_cparams(("arbitrary", "arbitrary", "arbitrary")),
        name="dilated_prompt",
    )(qv, kv, kv)
    return o.reshape(n_b * s_len, B_O), lse.reshape(n_b * s_len, B_O)


def _dil_sample_kernel(q_ref, kn_ref, cache_ref, o_ref, lse_ref, kall_ref, *, t_new, wb, pad):
    kall_ref[0:wb, :] = cache_ref[...].astype(BF16)
    tail = jnp.concatenate([kn_ref[...], jnp.zeros((pad - t_new, 512), F32)], axis=0)
    kall_ref[wb:wb + pad, :] = tail.astype(BF16)
    rows = B_KV * B_REP * t_new
    rid = lax.broadcasted_iota(jnp.int32, (rows, 1), 0)
    pos = wb + (rid & (t_new - 1))
    lane = lax.broadcasted_iota(jnp.int32, (t_new, LANES), 1)
    zero = jnp.zeros((t_new, LANES), BF16)
    for gi, (win, dil) in enumerate(B_GROUPS):
        lo = wb - min(win, wb)
        nk = wb + pad - lo
        blocks = []
        for h in range(B_KV):
            for j in range(B_REP):
                c = gi * B_HPG + h * B_REP + j
                slot = q_ref[:, c * LANES:(c + 1) * LANES]
                blocks.append(jnp.concatenate([slot, zero] if h < 2 else [zero, slot], axis=1))
        qbd = jnp.concatenate(blocks, axis=0)
        s = _dot_nt(qbd, kall_ref[lo:lo + nk, 0:256])
        kidx = lo + lax.broadcasted_iota(jnp.int32, (1, nk), 1)
        d = pos - kidx
        valid = (d >= 0) & (d <= win) & ((d & (dil - 1)) == 0)
        df = d.astype(F32)
        parts = []
        for hj in range(B_HPG):
            sl = s[hj * t_new:(hj + 1) * t_new, :] - (SLOPES_B[hj] / dil) * df[hj * t_new:(hj + 1) * t_new, :]
            parts.append(jnp.where(valid[hj * t_new:(hj + 1) * t_new, :], sl, NEG))
        sm = jnp.concatenate(parts, axis=0)
        m = jnp.max(sm, axis=-1, keepdims=True)
        e = jnp.exp(sm - m)
        den = jnp.sum(e, axis=-1, keepdims=True)
        ow = _dot(e.astype(BF16), kall_ref[lo:lo + nk, 256:512]) / den
        lse = m + jnp.log(den)
        for h in range(B_KV):
            halves, lhalves = [], []
            for j in range(B_REP):
                r0 = (h * B_REP + j) * t_new
                piece = ow[r0:r0 + t_new, (h // 2) * LANES:(h // 2 + 1) * LANES]
                if (h % 2) != j:
                    piece = pltpu.roll(piece, HEAD_DIM, 1)
                halves.append(piece)
                lhalves.append(jnp.broadcast_to(lse[r0:r0 + t_new], (t_new, LANES)))
            c0 = gi * B_O + h * LANES
            o_ref[:, c0:c0 + LANES] = _pick_half(lane, halves[0], halves[1])
            lse_ref[:, c0:c0 + LANES] = _pick_half(lane, lhalves[0], lhalves[1])


def _dilated_sample(q_exp, kv_new, cache_dil, n, t_new):
    wb = cache_dil.shape[1]
    pad = LANES
    assert t_new & (t_new - 1) == 0 and t_new <= pad
    cache3 = cache_dil.reshape(n, wb, 512)
    o, lse = pl.pallas_call(
        functools.partial(_dil_sample_kernel, t_new=t_new, wb=wb, pad=pad),
        grid=(n,),
        in_specs=[pl.BlockSpec((t_new, N_B_GROUPS * B_HPG * LANES), lambda s: (s, 0)),
                  pl.BlockSpec((t_new, 512), lambda s: (s, 0)),
                  pl.BlockSpec((None, wb, 512), lambda s: (s, 0, 0))],
        out_specs=[pl.BlockSpec((t_new, N_B_GROUPS * B_O), lambda s: (s, 0)),
                   pl.BlockSpec((t_new, N_B_GROUPS * B_O), lambda s: (s, 0))],
        out_shape=[jax.ShapeDtypeStruct((n * t_new, N_B_GROUPS * B_O), F32),
                   jax.ShapeDtypeStruct((n * t_new, N_B_GROUPS * B_O), F32)],
        scratch_shapes=[pltpu.VMEM((wb + pad, 512), BF16)],
        compiler_params=_cparams(("arbitrary",)),
        name="dilated_sample",
    )(q_exp, kv_new, cache3)
    return ([o[:, g * B_O:(g + 1) * B_O] for g in range(N_B_GROUPS)],
            [lse[:, g * B_O:(g + 1) * B_O] for g in range(N_B_GROUPS)])


def _nsa_sample_kernel(pt_ref, q_ref, *refs, n_pages, t_new, past, wbuf, ns, nslot, nk_pad, nw_pad):
    ck_refs = refs[0:n_pages]
    sp_refs = refs[n_pages:2 * n_pages]
    kvn_ref, win_ref, es_ref = refs[2 * n_pages:2 * n_pages + 3]
    oc_ref, os_ref, ow_ref = refs[2 * n_pages + 3:2 * n_pages + 6]
    ck_s, kall, kwall = refs[2 * n_pages + 6:]
    del pt_ref
    page = sp_refs[0].shape[0]
    hs = nslot // 2
    per_page = page // CMP_LEN
    ck_s[...] = jnp.zeros(ck_s.shape, F32)
    for p in range(n_pages):
        for n in range(per_page):
            c = p * per_page + n
            slot = (c % 2) * hs + c // 2
            ck_s[slot:slot + 1, :] = ck_refs[p][n:n + 1, :]
    for p in range(n_pages):
        kall[p * page:(p + 1) * page, :] = sp_refs[p][...].astype(BF16)
    pad_k = nk_pad - past
    kall[past:nk_pad, :] = jnp.concatenate(
        [kvn_ref[:, 512:1024], jnp.zeros((pad_k - t_new, 512), F32)], axis=0).astype(BF16)
    kwall[0:wbuf, :] = win_ref[...].astype(BF16)
    pad_w = nw_pad - wbuf
    kwall[wbuf:nw_pad, :] = jnp.concatenate(
        [kvn_ref[:, 1024:1536], jnp.zeros((pad_w - t_new, 512), F32)], axis=0).astype(BF16)

    rows = A_HEADS * t_new
    zero = jnp.zeros((t_new, LANES), BF16)
    blocks = []
    for g in range(A_KV):
        for r in range(A_REP):
            h = g * A_REP + r
            slot = q_ref[:, h * LANES:(h + 1) * LANES]
            blocks.append(jnp.concatenate([slot, zero] if g < 2 else [zero, slot], axis=1))
    qbd = jnp.concatenate(blocks, axis=0)
    rid = lax.broadcasted_iota(jnp.int32, (rows, 1), 0)
    pos = past + (rid & (t_new - 1))
    posf = pos.astype(F32)

    def head_rows(fn):
        return jnp.concatenate([fn(h) for h in range(A_HEADS)], axis=0)

    def finish(ow):
        pieces = []
        for g in range(A_KV):
            for r in range(A_REP):
                r0 = (g * A_REP + r) * t_new
                pieces.append((ow[r0:r0 + t_new, (g // 2) * LANES:(g // 2 + 1) * LANES], g % 2))
        return _assemble(pieces, t_new)

    n_c = n_pages * per_page
    slot_i = lax.broadcasted_iota(jnp.int32, (1, nslot), 1)
    sl_lo = slot_i & (hs - 1)
    cidx = 2 * sl_lo + jnp.where(slot_i >= hs, 1, 0)
    svalid = (sl_lo < n_c // 2) & (cidx * CMP_LEN + (CMP_LEN - 1) <= pos)
    dist = posf - (cidx.astype(F32) * CMP_LEN + 0.5 * (CMP_LEN - 1))
    s = _dot_nt(qbd, ck_s[:, 0:256].astype(BF16))
    sm = head_rows(lambda h: jnp.where(
        svalid[h * t_new:(h + 1) * t_new], s[h * t_new:(h + 1) * t_new] - SLOPES_A[h] * dist[h * t_new:(h + 1) * t_new],
        -jnp.inf))
    m = jnp.max(sm, axis=-1, keepdims=True)
    m = jnp.where(m == -jnp.inf, 0.0, m)
    e = jnp.exp(sm - m)
    den = jnp.sum(e, axis=-1, keepdims=True)
    pc = e / jnp.where(den > 0, den, 1.0)
    oc_ref[...] = finish(_dot(pc.astype(BF16), ck_s[:, 256:512].astype(BF16)))
    g_rows = A_KV * t_new
    ps = jnp.concatenate(
        [sum(pc[(g * A_REP + r) * t_new:(g * A_REP + r + 1) * t_new] for r in range(A_REP)) for g in range(A_KV)],
        axis=0)
    imp = ps + pltpu.roll(ps, hs, 1)
    blk = lax.broadcasted_iota(jnp.int32, (1, nslot), 1)
    gid = lax.broadcasted_iota(jnp.int32, (g_rows, 1), 0)
    cur = jnp.right_shift(past + (gid & (t_new - 1)), 6)
    imp = jnp.where(blk < n_c // 2, imp, 0.0)
    imp = jnp.where((blk == 0) | (blk == cur) | (blk == cur - 1), A_REP + 1.0, imp)
    imp = jnp.where(blk <= cur, imp, -1.0)
    imp = jnp.where(blk < ns, imp, -3.0)
    cnt = jnp.zeros((g_rows, nslot), F32)
    for j in range(ns):
        col = imp[:, j:j + 1]
        ahead = (col > imp) | ((col == imp) & (blk > j))
        cnt = cnt + jnp.where(ahead, 1.0, 0.0)
    sel = jnp.where((cnt < min(N_SEL, ns)) & (blk < ns), 1.0, 0.0).astype(BF16)
    selk = _dot(sel, es_ref[...])
    kpos = lax.broadcasted_iota(jnp.int32, (1, nk_pad), 1)
    kposf = kpos.astype(F32)
    s = _dot_nt(qbd, kall[:, 0:256])

    def sel_row(h):
        g = h // A_REP
        vg = (selk[g * t_new:(g + 1) * t_new] > 0.5) & (kpos <= pos[h * t_new:(h + 1) * t_new])
        sh = s[h * t_new:(h + 1) * t_new] - SLOPES_A[h] * (posf[h * t_new:(h + 1) * t_new] - kposf)
        return jnp.where(vg, sh, NEG)

    sm = head_rows(sel_row)
    m = jnp.max(sm, axis=-1, keepdims=True)
    e = jnp.exp(sm - m)
    den = jnp.sum(e, axis=-1, keepdims=True)
    os_ref[...] = finish(_dot(e.astype(BF16), kall[:, 256:512]) / den)
    wpos = (past - wbuf) + lax.broadcasted_iota(jnp.int32, (1, nw_pad), 1)
    wd = pos - wpos
    wvalid = (wd >= 0) & (wd <= WIN_A) & (wpos >= 0)
    wdf = wd.astype(F32)
    s = _dot_nt(qbd, kwall[:, 0:256])
    sm = head_rows(lambda h: jnp.where(
        wvalid[h * t_new:(h + 1) * t_new], s[h * t_new:(h + 1) * t_new] - SLOPES_A[h] * wdf[h * t_new:(h + 1) * t_new], NEG))
    m = jnp.max(sm, axis=-1, keepdims=True)
    e = jnp.exp(sm - m)
    den = jnp.sum(e, axis=-1, keepdims=True)
    ow_ref[...] = finish(_dot(e.astype(BF16), kwall[:, 256:512]) / den)


def _nsa_sample(q_exp, ckv_pool, cache_slc3, kv_new, cache_win3, page_table, layer, n, t_new):
    n_pages = page_table.shape[1]
    page = cache_slc3.shape[1]
    past = n_pages * page
    wbuf = cache_win3.shape[1]
    per_page = page // CMP_LEN
    tk = past + t_new
    ns = -(-tk // SLC_LEN)
    assert past % CMP_LEN == 0 and t_new < CMP_LEN and t_new & (t_new - 1) == 0 and past >= wbuf
    hs = HEAD_DIM
    while hs < max(n_pages * per_page // 2, ns):
        hs *= 2
    nslot = 2 * hs
    nk_pad = past + LANES
    nw_pad = wbuf + LANES
    es = (np.arange(nslot)[:, None] == (np.arange(nk_pad)[None, :] // SLC_LEN)).astype(np.float32)
    es = jnp.asarray(es, BF16)
    ck_specs = [pl.BlockSpec((None, per_page, 512), functools.partial(lambda s, pt, p: (pt[s, p], 0, 0), p=p))
                for p in range(n_pages)]
    sp_specs = [pl.BlockSpec((None, page, 512), functools.partial(lambda s, pt, p: (pt[s, p], 0, layer), p=p))
                for p in range(n_pages)]
    grid_spec = pltpu.PrefetchScalarGridSpec(
        num_scalar_prefetch=1,
        grid=(n,),
        in_specs=[pl.BlockSpec((t_new, A_HEADS * LANES), lambda s, pt: (s, 0))] + ck_specs + sp_specs +
                 [pl.BlockSpec((t_new, 3 * A_KVW), lambda s, pt: (s, 0)),
                  pl.BlockSpec((None, wbuf, 512), lambda s, pt: (s, 0, layer)),
                  pl.BlockSpec((nslot, nk_pad), lambda s, pt: (0, 0))],
        out_specs=[pl.BlockSpec((t_new, A_Q), lambda s, pt: (s, 0))] * 3,
        scratch_shapes=[pltpu.VMEM((nslot, 512), F32),
                        pltpu.VMEM((nk_pad, 512), BF16),
                        pltpu.VMEM((nw_pad, 512), BF16)],
    )
    return pl.pallas_call(
        functools.partial(_nsa_sample_kernel, n_pages=n_pages, t_new=t_new, past=past, wbuf=wbuf,
                          ns=ns, nslot=nslot, nk_pad=nk_pad, nw_pad=nw_pad),
        grid_spec=grid_spec,
        out_shape=[jax.ShapeDtypeStruct((n * t_new, A_Q), F32)] * 3,
        compiler_params=_cparams(("arbitrary",)),
        name="nsa_sample",
    )(page_table, q_exp, *([ckv_pool] * n_pages), *([cache_slc3] * n_pages), kv_new, cache_win3, es)


def _take_cols(w, src):
    src = np.asarray(src)
    cols = jnp.take(w, jnp.asarray(np.maximum(src, 0)), axis=1)
    return jnp.where(jnp.asarray(src >= 0)[None, :], cols, 0.0).astype(BF16)


def _a_in_cols():
    q0, kc0, ks0, kw0 = 0, A_Q, A_Q + A_KVW, A_Q + 2 * A_KVW
    gate0 = A_Q + 3 * A_KVW
    z0 = gate0 + 3 * A_HEADS
    qm0 = z0 + 3 * A_Q
    zm0 = qm0 + MEM_W
    src = list(range(kc0, kc0 + 3 * A_KVW))
    for h in range(A_HEADS):
        g = h // A_REP
        slot = [-1] * LANES
        for d in range(HEAD_DIM):
            slot[(g % 2) * HEAD_DIM + d] = q0 + h * HEAD_DIM + d
        src += slot
    src += list(range(z0, z0 + 3 * A_Q))
    src += list(range(qm0, qm0 + MEM_W)) + list(range(zm0, zm0 + MEM_W))
    src += list(range(gate0, gate0 + 3 * A_HEADS)) + [-1] * (LANES - 3 * A_HEADS)
    return src


def _b_in_cols():
    src = []
    for gi in range(N_B_GROUPS):
        for h in range(B_KV):
            for j in range(B_REP):
                slot = [-1] * LANES
                for d in range(HEAD_DIM):
                    slot[(h % 2) * HEAD_DIM + d] = gi * B_O + (h * B_REP + j) * HEAD_DIM + d
                src += slot
    src += list(range(B_Q, B_Q + B_O + 2 * MEM_W))
    return src


def _blockdiag2(w):
    z = jnp.zeros_like(w)
    top = jnp.concatenate([w, z], axis=-1)
    bot = jnp.concatenate([z, w], axis=-1)
    return jnp.concatenate([top, bot], axis=-2)


def kernel(x_prompt, x_sample, mem_prompt, cache_cmp_kv, cache_slc_kv, cache_win_kv, cache_dil_kv, cache_mem_kv,
           page_table, g_pre, g_post, g_mem, w_mem_kv, w_in_a, w_out_a, cmp_pos, cmp_w1, cmp_w2,
           g_kv_b, w_kv_b, w_in_b, w_out_b):
    n_b, s_len, d = x_prompt.shape
    n_s, t_s, _ = x_sample.shape
    depth = g_pre.shape[0]
    n_a = w_in_a.shape[0]
    n_pool, page = cache_cmp_kv.shape[:2]
    mem_len = mem_prompt.shape[1]
    wbuf_a = cache_win_kv.shape[1]
    wbuf_b = cache_dil_kv.shape[1]
    ns_p = s_len // SLC_LEN

    xp = x_prompt.reshape(n_b * s_len, d)
    xs = x_sample.reshape(n_s * t_s, d)
    mem2d = mem_prompt.reshape(n_b * mem_len, d)

    a_cols = _a_in_cols()
    b_cols = _b_in_cols()
    a_outs = [(0, 3 * A_KVW, F32, 1.0), (A_KVW, 2 * A_KVW, BF16, 1.0),
              (3 * A_KVW, A_HEADS * LANES, BF16, QSCALE),
              (3 * A_KVW + A_HEADS * LANES, 3 * A_Q + 2 * MEM_W + LANES, F32, 1.0)]
    nqb = N_B_GROUPS * B_HPG * LANES
    b_outs = [(0, nqb, BF16, QSCALE), (nqb, B_O + 2 * MEM_W, F32, 1.0)]
    egate = np.zeros((LANES, 3 * A_Q), np.float32)
    for b in range(3):
        for h in range(A_HEADS):
            egate[b * A_HEADS + h, b * A_Q + h * HEAD_DIM:b * A_Q + (h + 1) * HEAD_DIM] = 1.0
    egate = jnp.asarray(egate, BF16)
    et = jnp.asarray((np.arange(s_len)[:, None] // SLC_LEN == np.arange(ns_p)[None, :]).astype(np.float32), BF16)

    cache_cmp2 = cache_cmp_kv.reshape(n_pool * (page // CMP_LEN), CMP_LEN * n_a * A_KVW)
    cache_slc3 = cache_slc_kv.reshape(n_pool, page, n_a * A_KVW)
    cache_win3 = cache_win_kv.reshape(n_s, wbuf_a, n_a * A_KVW)
    cache_mem3 = cache_mem_kv.reshape(n_s, mem_len, depth * 2 * MEM_W)

    kv_p, kv_s, mkv_list = [], [], []
    kvb_p = kvb_s = kvbb_p = None
    for l in range(depth):
        mkv = _proj(mem2d, g_mem[l], w_mem_kv[l].astype(BF16), [(0, 2 * MEM_W, F32, 1.0)])[0]
        mkv_list.append(mkv)
        mkv3 = mkv.reshape(n_b, mem_len, 2 * MEM_W)
        if l < n_a:
            w_in = _take_cols(w_in_a[l], a_cols)
            w_out = w_out_a[l].astype(BF16)
            pos2 = jnp.concatenate([cmp_pos[l], cmp_pos[l]], axis=-1).transpose(1, 0, 2)
            w1bd = _blockdiag2(cmp_w1[l]).transpose(1, 0, 2, 3).astype(BF16)
            w2bd = _blockdiag2(cmp_w2[l]).astype(BF16)
            kv, kvb, q, rest = _proj(xp, g_pre[l], w_in, a_outs)
            kv_p.append(kv)
            ckv = _compress(kv.reshape(n_b * s_len // CMP_LEN, CMP_LEN * 3 * A_KVW), 3, 0, pos2, w1bd, w2bd, 256)
            nc = s_len // CMP_LEN
            ckv = ckv.reshape(n_b, nc // 2, 2, 512).transpose(0, 2, 1, 3).reshape(n_b, nc, 512)
            o_cmp, sel = _nsa_cmp(q.reshape(n_b, s_len, -1), ckv, n_b, s_len, 128)
            o_slc, o_win = _nsa_sw(q.reshape(n_b, s_len, -1), kvb.reshape(n_b, s_len, -1), sel, et,
                                   n_b, s_len, 128, min(512, s_len))
            om = _mem_attend(rest, 9, mkv3, 0, n_b, s_len, min(512, s_len))
            xp = _merge_a(xp, o_cmp.reshape(-1, A_Q), o_slc.reshape(-1, A_Q), o_win.reshape(-1, A_Q),
                          rest, om, w_out, g_post[l], egate)
            kv, _, q, rest = _proj(xs, g_pre[l], w_in, a_outs)
            kv_s.append(kv)
            ckv_pool = _compress(cache_cmp2, n_a, l, pos2, w1bd, w2bd, 1024)
            ckv_pool = ckv_pool.reshape(n_pool, page // CMP_LEN, 512)
            o_cmp, o_slc, o_win = _nsa_sample(q, ckv_pool, cache_slc3, kv, cache_win3, page_table, l, n_s, t_s)
            om = _mem_attend(rest, 9, cache_mem3, l, n_s, t_s, t_s)
            xs = _merge_a(xs, o_cmp, o_slc, o_win, rest, om, w_out, g_post[l], egate)
            if l == n_a - 1:
                w_kv = w_kv_b.astype(BF16)
                kvb_p, kvbb_p = _proj(xp, g_kv_b, w_kv, [(0, 512, F32, 1.0), (0, 512, BF16, 1.0)])
                kvb_s = _proj(xs, g_kv_b, w_kv, [(0, 512, F32, 1.0)])[0]
        else:
            lb = l - n_a
            w_in = _take_cols(w_in_b[lb], b_cols)
            w_out = w_out_b[lb].astype(BF16)
            q, rest = _proj(xp, g_pre[l], w_in, b_outs)
            outs, lses = [], []
            for gi in range(N_B_GROUPS):
                o, lse = _dilated_prompt(q.reshape(n_b, s_len, -1), kvbb_p.reshape(n_b, s_len, 512), gi, n_b, s_len, 256)
                outs.append(o)
                lses.append(lse)
            om = _mem_attend(rest, 2, mkv3, 0, n_b, s_len, min(512, s_len))
            xp = _merge_b(xp, outs, lses, rest, om, w_out, g_post[l])
            q, rest = _proj(xs, g_pre[l], w_in, b_outs)
            outs, lses = _dilated_sample(q, kvb_s, cache_dil_kv, n_s, t_s)
            om = _mem_attend(rest, 2, cache_mem3, l, n_s, t_s, t_s)
            xs = _merge_b(xs, outs, lses, rest, om, w_out, g_post[l])

    def kv_stack(kvs, n, t, j):
        return jnp.stack([k[:, j * A_KVW:(j + 1) * A_KVW].reshape(n, t, 2, A_KV, HEAD_DIM) for k in kvs], 2)

    new_cmp_p = kv_stack(kv_p, n_b, s_len, 0)
    new_slc_p = kv_stack(kv_p, n_b, s_len, 1)
    win_p = kv_stack(kv_p, n_b, s_len, 2)
    new_win_p = win_p[:, -min(WIN_A, s_len):]
    new_cmp_s = kv_stack(kv_s, n_s, t_s, 0)
    new_slc_s = kv_stack(kv_s, n_s, t_s, 1)
    win_s = kv_stack(kv_s, n_s, t_s, 2)
    new_win_s = jnp.concatenate([cache_win_kv, win_s], 1)[:, -wbuf_a:]
    kvb_p5 = kvb_p.reshape(n_b, s_len, 2, B_KV, HEAD_DIM)
    new_dil_p = kvb_p5[:, -min(B_GROUPS[-1][0], s_len):]
    new_dil_s = jnp.concatenate([cache_dil_kv, kvb_s.reshape(n_s, t_s, 2, B_KV, HEAD_DIM)], 1)[:, -wbuf_b:]
    new_mem_p = jnp.stack([m.reshape(n_b, mem_len, 2, MEM_HEADS, HEAD_DIM) for m in mkv_list], 2)
    return (xp.reshape(n_b, s_len, d), xs.reshape(n_s, t_s, d), new_cmp_p, new_cmp_s, new_slc_p, new_slc_s,
            new_win_p, new_win_s, new_dil_p, new_dil_s, new_mem_p)
```

```python
import functools
import numpy as np
import jax
import jax.numpy as jnp
from jax import lax
from jax.experimental import pallas as pl
from jax.experimental.pallas import tpu as pltpu

F32 = jnp.float32
BF16 = jnp.bfloat16

HEAD_DIM = 64
LANES = 128
A_HEADS = 12
A_KV = 4
A_REP = A_HEADS // A_KV
CMP_LEN = 32
SLC_LEN = 64
N_SEL = 16
WIN_A = 512
B_GROUPS = ((128, 1), (512, 4), (2048, 16))
N_B_GROUPS = 3
B_KV = 4
B_REP = 2
B_HPG = B_KV * B_REP
MEM_HEADS = 4
EPS = 1e-6
A_Q = A_HEADS * HEAD_DIM
A_KVW = 2 * A_KV * HEAD_DIM
MEM_W = MEM_HEADS * HEAD_DIM
B_Q = N_B_GROUPS * B_HPG * HEAD_DIM
B_O = B_HPG * HEAD_DIM
QSCALE = HEAD_DIM ** -0.5
NEG = -1e30
VMEM_LIMIT = 56 * 1024 * 1024
TM = 256
TQ = 128
TK = 512
CM_TILE = 256


def _alibi(n):
    return [float(2.0 ** (-8.0 * i / n)) for i in range(1, n + 1)]


SLOPES_A = _alibi(A_HEADS)
SLOPES_B = _alibi(B_HPG)


def _cparams(sem):
    return pltpu.CompilerParams(dimension_semantics=sem, vmem_limit_bytes=VMEM_LIMIT)


def _silu(x):
    return x * (1.0 / (1.0 + jnp.exp(-x)))


def _dot_nt(a, b):
    return lax.dot_general(a, b, (((1,), (1,)), ((), ())), preferred_element_type=F32)


def _dot(a, b):
    return jnp.dot(a, b, preferred_element_type=F32)


def _split3(x):
    hi = x.astype(BF16)
    r1 = x - hi.astype(F32)
    mid = r1.astype(BF16)
    lo = (r1 - mid.astype(F32)).astype(BF16)
    return hi, mid, lo


def _pick_half(lane, a, b):
    return jnp.where(lane < HEAD_DIM, a, b)


def _assemble(pieces, rows):
    lane = lax.broadcasted_iota(jnp.int32, (rows, LANES), 1)
    cols = []
    for c in range(len(pieces) // 2):
        halves = []
        for k in (0, 1):
            arr, useful = pieces[2 * c + k]
            if useful != k:
                arr = pltpu.roll(arr, HEAD_DIM, 1)
            halves.append(arr)
        cols.append(_pick_half(lane, halves[0], halves[1]))
    return jnp.concatenate(cols, axis=1)


def _proj_kernel(x_ref, g_ref, *refs, n_w, n_p, plan):
    w_refs = refs[:n_w]
    p_refs = refs[n_w:n_w + n_p]
    out_refs = refs[n_w + n_p:]
    x = x_ref[...]
    ms = jnp.mean(x * x, axis=-1, keepdims=True)
    xn = ((x * lax.rsqrt(ms + EPS)) * g_ref[...]).astype(BF16)
    xs = [xn] + [_dot(p_ref[...], xn).astype(BF16) for p_ref in p_refs]
    for (transposed, wi, c0, c1, xi, stores) in plan:
        if transposed:
            y = _dot_nt(w_refs[wi][c0:c1, :], xs[xi])
        else:
            y = _dot(xs[xi], w_refs[wi][:, c0:c1])
        for store in stores:
            store(out_refs, y)


def _st_rows(oi, o0, scale=1.0):
    def store(out_refs, y):
        v = y if scale == 1.0 else y * scale
        out_refs[oi][:, o0:o0 + y.shape[1]] = v.astype(out_refs[oi].dtype)
    return store


def _st_t(oi, o0):
    def store(out_refs, y):
        out_refs[oi][o0:o0 + y.shape[0], :] = y.astype(out_refs[oi].dtype)
    return store


def _st_ttile(oi, o0):
    def store(out_refs, y):
        for u in range(y.shape[1] // LANES):
            out_refs[oi][u, o0:o0 + y.shape[0], :] = y[:, u * LANES:(u + 1) * LANES].astype(out_refs[oi].dtype)
    return store


def _st_cm(oi, o0, dil, scale=1.0):
    def store(out_refs, y):
        v = y if scale == 1.0 else y * scale
        v = v.astype(out_refs[oi].dtype)
        out_refs[oi][:, :, o0:o0 + y.shape[1]] = v.reshape(dil, y.shape[0] // dil, y.shape[1])
    return store


def _chunks(c0, c1, step=512):
    return [(a, min(a + step, c1)) for a in range(c0, c1, step)]


def _proj(x2d, g, weights, perms, plan, out_shapes, out_specs, tm=TM):
    m, d = x2d.shape
    tm = min(tm, m)
    const2 = lambda i: (0, 0)
    return pl.pallas_call(
        functools.partial(_proj_kernel, n_w=len(weights), n_p=len(perms), plan=tuple(plan)),
        grid=(m // tm,),
        in_specs=[pl.BlockSpec((tm, d), lambda i: (i, 0)), pl.BlockSpec((1, d), const2)] +
                 [pl.BlockSpec(w.shape, const2) for w in weights] +
                 [pl.BlockSpec(p.shape, const2) for p in perms],
        out_specs=out_specs,
        out_shape=out_shapes,
        compiler_params=_cparams(("arbitrary",)),
        name="proj",
    )(x2d, g.reshape(1, d), *weights, *perms)


def _proj_rows(x2d, g, w_bf, outs):
    m = x2d.shape[0]
    tm = min(TM, m)
    plan = []
    for oi, (c0, width, _, scale) in enumerate(outs):
        for a, b in _chunks(c0, c0 + width):
            plan.append((False, 0, a, b, 0, (_st_rows(oi, a - c0, scale),)))
    return _proj(x2d, g, [w_bf], [], plan,
                 [jax.ShapeDtypeStruct((m, o[1]), o[2]) for o in outs],
                 [pl.BlockSpec((tm, o[1]), lambda i: (i, 0)) for o in outs])


def _t_spec(width, spb, tm):
    return pl.BlockSpec((None, width, tm), lambda i: (i // spb, 0, i % spb))


def _cm_spec(dil, width):
    return pl.BlockSpec((None, dil, CM_TILE // dil, width), lambda i: (i, 0, 0, 0))


def _compress_pool_kernel(x_ref, pos_ref, w1_ref, w2_ref, o_ref, acc_ref):
    l = pl.program_id(1)

    @pl.when(l == 0)
    def _():
        acc_ref[...] = jnp.zeros_like(acc_ref)

    x = x_ref[...]
    for kv in range(2):
        for p in range(2):
            c0 = kv * 256 + p * LANES
            xb = (x[:, c0:c0 + LANES] + pos_ref[kv:kv + 1, :]).astype(BF16)
            acc_ref[2 * kv + p] += _dot(xb, w1_ref[kv])

    @pl.when(l == CMP_LEN - 1)
    def _():
        for kv in range(2):
            for p in range(2):
                c0 = kv * 256 + p * LANES
                h = _silu(acc_ref[2 * kv + p]).astype(BF16)
                o_ref[:, c0:c0 + LANES] = _dot(h, w2_ref[kv])


def _compress_pool(src2d, col_blocks, col_off, pos2, w1bd, w2bd, r):
    nb = src2d.shape[0]
    r = min(r, nb)
    return pl.pallas_call(
        _compress_pool_kernel,
        grid=(nb // r, CMP_LEN),
        in_specs=[pl.BlockSpec((r, 512), lambda i, l: (i, l * col_blocks + col_off)),
                  pl.BlockSpec((None, 2, LANES), lambda i, l: (l, 0, 0)),
                  pl.BlockSpec((None, 2, LANES, LANES), lambda i, l: (l, 0, 0, 0)),
                  pl.BlockSpec((2, LANES, LANES), lambda i, l: (0, 0, 0))],
        out_specs=pl.BlockSpec((r, 512), lambda i, l: (i, 0)),
        out_shape=jax.ShapeDtypeStruct((nb, 512), F32),
        scratch_shapes=[pltpu.VMEM((4, r, LANES), F32)],
        compiler_params=_cparams(("arbitrary", "arbitrary")),
        name="compress_pool",
    )(src2d, pos2, w1bd, w2bd)


def _compress_prompt_kernel(s0, s1, s2, s3, pos_ref, w1_ref, w2_ref, w2t_ref, ckt_ref, cv_ref, *, nc):
    half = nc // 2
    slabs = (s0, s1, s2, s3)
    accs = [jnp.zeros((nc, LANES), F32) for _ in range(4)]
    for l in range(CMP_LEN):
        for idx in range(4):
            kv = idx // 2
            xe = slabs[idx][pl.ds(l, half, stride=2 * CMP_LEN), :]
            xo = slabs[idx][pl.ds(CMP_LEN + l, half, stride=2 * CMP_LEN), :]
            xb = (jnp.concatenate([xe, xo], axis=0) + pos_ref[l, kv:kv + 1, :]).astype(BF16)
            accs[idx] = accs[idx] + _dot(xb, w1_ref[l, kv])
    for idx in range(4):
        kv, p = idx // 2, idx % 2
        h = _silu(accs[idx]).astype(BF16)
        if kv == 0:
            ckt_ref[p * LANES:(p + 1) * LANES, :] = _dot_nt(w2t_ref[0], h)
        else:
            cv_ref[:, p * LANES:(p + 1) * LANES] = _dot(h, w2_ref[1])


def _compress_prompt(slabs, pos2, w1bd, w2bd, w2bdt, n_b, s_len):
    nc = s_len // CMP_LEN
    slab_spec = pl.BlockSpec((s_len, LANES), lambda b: (b, 0))
    return pl.pallas_call(
        functools.partial(_compress_prompt_kernel, nc=nc),
        grid=(n_b,),
        in_specs=[slab_spec] * 4 +
                 [pl.BlockSpec(pos2.shape, lambda b: (0, 0, 0)),
                  pl.BlockSpec(w1bd.shape, lambda b: (0, 0, 0, 0)),
                  pl.BlockSpec(w2bd.shape, lambda b: (0, 0, 0)),
                  pl.BlockSpec(w2bdt.shape, lambda b: (0, 0, 0))],
        out_specs=[pl.BlockSpec((None, 256, nc), lambda b: (b, 0, 0)),
                   pl.BlockSpec((None, nc, 256), lambda b: (b, 0, 0))],
        out_shape=[jax.ShapeDtypeStruct((n_b, 256, nc), F32),
                   jax.ShapeDtypeStruct((n_b, nc, 256), F32)],
        compiler_params=_cparams(("arbitrary",)),
        name="compress_prompt",
    )(*slabs, pos2, w1bd, w2bd, w2bdt)


def _topk_mask_t(imp_t, k):
    nb, cols = imp_t.shape
    row = lax.broadcasted_iota(jnp.int32, (nb, cols), 0)
    work = imp_t
    sel = jnp.zeros((nb, cols), F32)
    for _ in range(k):
        m = jnp.max(work, axis=0, keepdims=True)
        idx = jnp.min(jnp.where(work == m, row, nb), axis=0, keepdims=True)
        hit = row == idx
        sel = jnp.where(hit, 1.0, sel)
        work = jnp.where(hit, -3.0, work)
    return sel


def _group_q(q_ref, g):
    return jnp.concatenate(
        [q_ref[:, (g * A_REP + r) * LANES:(g * A_REP + r + 1) * LANES] for r in range(A_REP)], axis=0)


def _nsa_cmp_kernel(q_ref, ckt_ref, cv_ref, any_ref, o_ref, sel_ref, flag_ref, *, tq, nc, ns):
    i = pl.program_id(1)
    q0 = i * tq
    half = nc // 2
    pos = q0 + lax.broadcasted_iota(jnp.int32, (tq, 1), 0)
    posf = pos.astype(F32)
    tok = lax.broadcasted_iota(jnp.int32, (1, nc), 1)
    cidx = jnp.where(tok < half, 2 * tok, 2 * (tok - half) + 1)
    c_end = cidx * CMP_LEN + (CMP_LEN - 1)
    c_mid = cidx.astype(F32) * CMP_LEN + 0.5 * (CMP_LEN - 1)
    cmask = c_end <= pos
    dist = posf - c_mid
    blk = lax.broadcasted_iota(jnp.int32, (1, ns), 1)
    cur = jnp.right_shift(pos, 6)
    forced = (blk == 0) | (blk == cur) | (blk == cur - 1)
    allowed = blk <= cur
    pieces = []
    anys = []
    for g in range(A_KV):
        p = g // 2
        ckt = ckt_ref[p * LANES:(p + 1) * LANES, :].astype(BF16)
        cv = cv_ref[:, p * LANES:(p + 1) * LANES].astype(BF16)
        s = _dot(_group_q(q_ref, g), ckt)
        parts = []
        for r in range(A_REP):
            sr = s[r * tq:(r + 1) * tq, :] - SLOPES_A[g * A_REP + r] * dist
            parts.append(jnp.where(cmask, sr, -jnp.inf))
        sm = jnp.concatenate(parts, axis=0)
        m = jnp.max(sm, axis=-1, keepdims=True)
        m = jnp.where(m == -jnp.inf, 0.0, m)
        e = jnp.exp(sm - m)
        den = jnp.sum(e, axis=-1, keepdims=True)
        pc = e / jnp.where(den > 0, den, 1.0)
        og = _dot(pc.astype(BF16), cv)
        for r in range(A_REP):
            pieces.append((og[r * tq:(r + 1) * tq, :], g % 2))
        ps = pc[0:tq] + pc[tq:2 * tq] + pc[2 * tq:3 * tq]
        imp = ps[:, 0:half] + ps[:, half:nc]
        imp = jnp.where(forced, A_REP + 1.0, imp)
        imp = jnp.where(allowed, imp, -1.0)
        sel = _topk_mask_t(imp.T, min(N_SEL, ns)).T
        sel_ref[:, g * ns:(g + 1) * ns] = sel.astype(BF16)
        anys.append(jnp.max(sel, axis=0, keepdims=True))
    o_ref[...] = _assemble(pieces, tq)
    rows = jnp.concatenate(anys + [jnp.zeros((8 - A_KV, ns), F32)], axis=0).astype(BF16)
    flag_ref[...] = _dot(rows, any_ref[...])


def _nsa_cmp(q_exp, ckt, cv, anymat, n_b, s_len, tq):
    nc = ckt.shape[2]
    ns = s_len // SLC_LEN
    assert nc == 2 * ns
    return pl.pallas_call(
        functools.partial(_nsa_cmp_kernel, tq=tq, nc=nc, ns=ns),
        grid=(n_b, s_len // tq),
        in_specs=[pl.BlockSpec((None, tq, A_HEADS * LANES), lambda b, i: (b, i, 0)),
                  pl.BlockSpec((None, 256, nc), lambda b, i: (b, 0, 0)),
                  pl.BlockSpec((None, nc, 256), lambda b, i: (b, 0, 0)),
                  pl.BlockSpec(anymat.shape, lambda b, i: (0, 0))],
        out_specs=[pl.BlockSpec((None, tq, A_Q), lambda b, i: (b, i, 0)),
                   pl.BlockSpec((None, tq, A_KV * ns), lambda b, i: (b, i, 0)),
                   pl.BlockSpec((None, None, 8, LANES), lambda b, i: (b, i, 0, 0))],
        out_shape=[jax.ShapeDtypeStruct((n_b, s_len, A_Q), F32),
                   jax.ShapeDtypeStruct((n_b, s_len, A_KV * ns), BF16),
                   jax.ShapeDtypeStruct((n_b, s_len // tq, 8, LANES), F32)],
        compiler_params=_cparams(("arbitrary", "arbitrary")),
        name="nsa_cmp",
    )(q_exp, ckt, cv, anymat)


def _nsa_sw_kernel(fl_ref, q_ref, kv_ref, sel_ref, e_ref, oslc_ref, owin_ref,
                   m_ref, l_ref, acc_ref, *, tq, tk, ns, nq, nkt):
    b = pl.program_id(0)
    i = pl.program_id(1)
    q0 = i * tq
    per = tk // LANES
    trow = lax.broadcasted_iota(jnp.int32, (tq, 1), 0)
    qg = [_group_q(q_ref, g) for g in range(A_KV)]

    m_ref[...] = jnp.full(m_ref.shape, NEG, F32)
    l_ref[...] = jnp.zeros(l_ref.shape, F32)
    acc_ref[...] = jnp.zeros(acc_ref.shape, F32)

    def tiles(j, r0):
        return jnp.concatenate([kv_ref[j * per + u, r0:r0 + LANES, :] for u in range(per)], axis=1)

    def body(j, carry):
        k0 = j * tk
        krel = (k0 - q0) + lax.broadcasted_iota(jnp.int32, (1, tk), 1)
        causal = krel <= trow
        krelf = krel.astype(F32)
        for g in range(A_KV):
            p = g // 2

            @pl.when(fl_ref[((b * nq + i) * A_KV + g) * nkt + j] != 0)
            def _(g=g, p=p):
                s = _dot(qg[g], tiles(j, p * LANES))
                selk = _dot(sel_ref[:, g * ns:(g + 1) * ns], e_ref[j])
                valid = (selk > 0.5) & causal
                parts = []
                for r in range(A_REP):
                    sr = s[r * tq:(r + 1) * tq, :] + SLOPES_A[g * A_REP + r] * krelf
                    parts.append(jnp.where(valid, sr, NEG))
                sm = jnp.concatenate(parts, axis=0)
                m_old = m_ref[g]
                m_new = jnp.maximum(m_old, jnp.max(sm, axis=-1, keepdims=True))
                pe = jnp.exp(sm - m_new)
                alpha = jnp.exp(m_old - m_new)
                l_ref[g] = alpha * l_ref[g] + jnp.sum(pe, axis=-1, keepdims=True)
                acc_ref[g] = alpha * acc_ref[g] + _dot_nt(pe.astype(BF16), tiles(j, 256 + p * LANES))
                m_ref[g] = m_new
        return carry

    lax.fori_loop(0, (q0 + tq + tk - 1) // tk, body, 0)
    pieces = []
    for g in range(A_KV):
        og = acc_ref[g] / l_ref[g]
        for r in range(A_REP):
            pieces.append((og[r * tq:(r + 1) * tq, :], g % 2))
    oslc_ref[...] = _assemble(pieces, tq)

    nwt = WIN_A // LANES + tq // LANES
    t0 = jnp.maximum(i * (tq // LANES) - WIN_A // LANES, 0)
    nkw = nwt * LANES
    wrel = (t0 * LANES - q0) + lax.broadcasted_iota(jnp.int32, (1, nkw), 1)
    dist = trow - wrel
    wvalid = (dist >= 0) & (dist <= WIN_A)
    wrelf = wrel.astype(F32)
    pieces = []
    for g in range(A_KV):
        p = g // 2
        kk = jnp.concatenate([kv_ref[t0 + u, 512 + p * LANES:512 + (p + 1) * LANES, :] for u in range(nwt)], axis=1)
        vv = jnp.concatenate([kv_ref[t0 + u, 768 + p * LANES:768 + (p + 1) * LANES, :] for u in range(nwt)], axis=1)
        s = _dot(qg[g], kk)
        parts = []
        for r in range(A_REP):
            sr = s[r * tq:(r + 1) * tq, :] + SLOPES_A[g * A_REP + r] * wrelf
            parts.append(jnp.where(wvalid, sr, NEG))
        sm = jnp.concatenate(parts, axis=0)
        m = jnp.max(sm, axis=-1, keepdims=True)
        pe = jnp.exp(sm - m)
        den = jnp.sum(pe, axis=-1, keepdims=True)
        og = _dot_nt(pe.astype(BF16), vv) / den
        for r in range(A_REP):
            pieces.append((og[r * tq:(r + 1) * tq, :], g % 2))
    owin_ref[...] = _assemble(pieces, tq)


def _nsa_sw(flags, q_exp, kvt, sel, etile, n_b, s_len, tq, tk):
    ns = s_len // SLC_LEN
    nq = s_len // tq
    nkt = s_len // tk
    assert s_len >= WIN_A + tq and tk % tq == 0 and tq % LANES == 0
    grid_spec = pltpu.PrefetchScalarGridSpec(
        num_scalar_prefetch=1,
        grid=(n_b, nq),
        in_specs=[pl.BlockSpec((None, tq, A_HEADS * LANES), lambda b, i, fl: (b, i, 0)),
                  pl.BlockSpec((None, s_len // LANES, 1024, LANES), lambda b, i, fl: (b, 0, 0, 0)),
                  pl.BlockSpec((None, tq, A_KV * ns), lambda b, i, fl: (b, i, 0)),
                  pl.BlockSpec(etile.shape, lambda b, i, fl: (0, 0, 0))],
        out_specs=[pl.BlockSpec((None, tq, A_Q), lambda b, i, fl: (b, i, 0)),
                   pl.BlockSpec((None, tq, A_Q), lambda b, i, fl: (b, i, 0))],
        scratch_shapes=[pltpu.VMEM((A_KV, A_REP * tq, 1), F32),
                        pltpu.VMEM((A_KV, A_REP * tq, 1), F32),
                        pltpu.VMEM((A_KV, A_REP * tq, LANES), F32)],
    )
    return pl.pallas_call(
        functools.partial(_nsa_sw_kernel, tq=tq, tk=tk, ns=ns, nq=nq, nkt=nkt),
        grid_spec=grid_spec,
        out_shape=[jax.ShapeDtypeStruct((n_b, s_len, A_Q), F32),
                   jax.ShapeDtypeStruct((n_b, s_len, A_Q), F32)],
        compiler_params=_cparams(("arbitrary", "arbitrary")),
        name="nsa_sw",
    )(flags, q_exp, kvt, sel, etile)


def _mem_kernel(qm_ref, mkv_ref, o_ref, *, tt):
    lane = lax.broadcasted_iota(jnp.int32, (tt, LANES), 1)
    cols = []
    for c in range(MEM_HEADS // 2):
        qc = qm_ref[:, c * LANES:(c + 1) * LANES]
        mkt = mkv_ref[c * LANES:(c + 1) * LANES, :].astype(BF16)
        mvt = mkv_ref[MEM_W + c * LANES:MEM_W + (c + 1) * LANES, :].astype(BF16)
        halves = []
        for k in (0, 1):
            qh = jnp.where((lane < HEAD_DIM) == (k == 0), qc, 0.0).astype(BF16)
            s = _dot(qh, mkt) * QSCALE
            m = jnp.max(s, axis=-1, keepdims=True)
            e = jnp.exp(s - m)
            den = jnp.sum(e, axis=-1, keepdims=True)
            halves.append(_dot_nt(e.astype(BF16), mvt) / den)
        cols.append(_pick_half(lane, halves[0], halves[1]))
    o_ref[...] = jnp.concatenate(cols, axis=1)


def _mem_attend(rest, qm_blk, mkvt4, layer, n, t, tt):
    mem = mkvt4.shape[3]
    steps = t // tt
    return pl.pallas_call(
        functools.partial(_mem_kernel, tt=tt),
        grid=(n, steps),
        in_specs=[pl.BlockSpec((tt, MEM_W), lambda b, i: (b * steps + i, qm_blk)),
                  pl.BlockSpec((None, None, 2 * MEM_W, mem), lambda b, i: (b, layer, 0, 0))],
        out_specs=pl.BlockSpec((tt, MEM_W), lambda b, i: (b * steps + i, 0)),
        out_shape=jax.ShapeDtypeStruct((n * t, MEM_W), F32),
        compiler_params=_cparams(("arbitrary", "arbitrary")),
        name="mem_attend",
    )(rest, mkvt4)


def _post(x_ref, y, gp_ref, out_ref):
    ms = jnp.mean(y * y, axis=-1, keepdims=True)
    out_ref[...] = x_ref[...] + (y * lax.rsqrt(ms + EPS)) * gp_ref[...]


def _merge_a_kernel(x_ref, oc_ref, os_ref, ow_ref, z_ref, zm_ref, gate_ref, om_ref,
                    w_ref, gp_ref, eg_ref, out_ref):
    gs = 1.0 / (1.0 + jnp.exp(-gate_ref[...]))
    eg = eg_ref[...]
    gexp = sum(_dot(t, eg) for t in _split3(gs))
    o = jnp.zeros(oc_ref.shape, F32)
    for b, ob_ref in enumerate((oc_ref, os_ref, ow_ref)):
        o = o + (ob_ref[...] * _silu(z_ref[:, b * A_Q:(b + 1) * A_Q])) * gexp[:, b * A_Q:(b + 1) * A_Q]
    om = om_ref[...] * _silu(zm_ref[...])
    y = _dot(o.astype(BF16), w_ref[0:A_Q, :]) + _dot(om.astype(BF16), w_ref[A_Q:A_Q + MEM_W, :])
    _post(x_ref, y, gp_ref, out_ref)


def _merge_a(x2d, o_cmp, o_slc, o_win, rest, om, w_bf, g_post, egate, tm=TM):
    m, d = x2d.shape
    tm = min(tm, m)
    row = lambda i: (i, 0)
    return pl.pallas_call(
        _merge_a_kernel,
        grid=(m // tm,),
        in_specs=[pl.BlockSpec((tm, d), row),
                  pl.BlockSpec((tm, A_Q), row), pl.BlockSpec((tm, A_Q), row), pl.BlockSpec((tm, A_Q), row),
                  pl.BlockSpec((tm, 3 * A_Q), row),
                  pl.BlockSpec((tm, MEM_W), lambda i: (i, 10)),
                  pl.BlockSpec((tm, LANES), lambda i: (i, 22)),
                  pl.BlockSpec((tm, MEM_W), row),
                  pl.BlockSpec((A_Q + MEM_W, d), lambda i: (0, 0)),
                  pl.BlockSpec((1, d), lambda i: (0, 0)),
                  pl.BlockSpec((LANES, 3 * A_Q), lambda i: (0, 0))],
        out_specs=pl.BlockSpec((tm, d), row),
        out_shape=jax.ShapeDtypeStruct((m, d), F32),
        compiler_params=_cparams(("arbitrary",)),
        name="merge_a",
    )(x2d, o_cmp, o_slc, o_win, rest, rest, rest, om, w_bf, g_post.reshape(1, d), egate)


def _merge_b_kernel(x_ref, o0_ref, o1_ref, o2_ref, l0_ref, l1_ref, l2_ref, z_ref, zm_ref, om_ref,
                    w_ref, gp_ref, *rest, n_u):
    u_refs = rest[:n_u]
    out_ref = rest[n_u]

    def nat(ref, ui):
        v = ref[...]
        if v.ndim == 2:
            return v
        v = v.reshape(v.shape[0] * v.shape[1], v.shape[2])
        u = u_refs[ui][...]
        return sum(_dot(u, t) for t in _split3(v))

    o0, o1, o2 = nat(o0_ref, 0), nat(o1_ref, 0), nat(o2_ref, 1)
    l0, l1, l2 = nat(l0_ref, 0), nat(l1_ref, 0), nat(l2_ref, 1)
    mx = jnp.maximum(jnp.maximum(l0, l1), l2)
    e0, e1, e2 = jnp.exp(l0 - mx), jnp.exp(l1 - mx), jnp.exp(l2 - mx)
    den = e0 + e1 + e2
    o = (e0 / den) * o0 + (e1 / den) * o1 + (e2 / den) * o2
    o = o * _silu(z_ref[...])
    om = om_ref[...] * _silu(zm_ref[...])
    y = _dot(o.astype(BF16), w_ref[0:B_O, :]) + _dot(om.astype(BF16), w_ref[B_O:B_O + MEM_W, :])
    _post(x_ref, y, gp_ref, out_ref)


def _merge_b(x2d, outs, lses, rest, om, w_bf, g_post, unperms, tm=TM):
    m, d = x2d.shape
    tm = min(tm, m)
    row = lambda i: (i, 0)

    def spec(a):
        if a.ndim == 2:
            return pl.BlockSpec((tm, B_O), row)
        assert tm == CM_TILE
        return pl.BlockSpec((None,) + a.shape[1:], lambda i: (i, 0, 0, 0))

    return pl.pallas_call(
        functools.partial(_merge_b_kernel, n_u=len(unperms)),
        grid=(m // tm,),
        in_specs=[pl.BlockSpec((tm, d), row)] + [spec(a) for a in outs] + [spec(a) for a in lses] +
                 [pl.BlockSpec((tm, B_O), row),
                  pl.BlockSpec((tm, MEM_W), lambda i: (i, 3)),
                  pl.BlockSpec((tm, MEM_W), row),
                  pl.BlockSpec((B_O + MEM_W, d), lambda i: (0, 0)),
                  pl.BlockSpec((1, d), lambda i: (0, 0))] +
                 [pl.BlockSpec(u.shape, lambda i: (0, 0)) for u in unperms],
        out_specs=pl.BlockSpec((tm, d), row),
        out_shape=jax.ShapeDtypeStruct((m, d), F32),
        compiler_params=_cparams(("arbitrary",)),
        name="merge_b",
    )(x2d, *outs, *lses, rest, rest, om, w_bf, g_post.reshape(1, d), *unperms)


def _dil_kernel(q_ref, kc_ref, kp_ref, o_ref, lse_ref, *, tn, wr):
    i = pl.program_id(2)
    nk = wr + tn

    def rows(ref):
        v = ref[...]
        return v if v.ndim == 2 else v.reshape(v.shape[0] * v.shape[1], v.shape[2])

    q, kc, kp = rows(q_ref), rows(kc_ref), rows(kp_ref)
    t = lax.broadcasted_iota(jnp.int32, (tn, 1), 0)
    k = lax.broadcasted_iota(jnp.int32, (1, nk), 1)
    dist = t - k + wr
    valid = (dist >= 0) & (dist <= wr) & ((k >= wr) | (i > 0))
    distf = dist.astype(F32)
    lane = lax.broadcasted_iota(jnp.int32, (tn, LANES), 1)
    o_cols, l_cols = [], []
    for h in range(B_KV):
        p = h // 2
        kk = jnp.concatenate([kp[:, p * LANES:(p + 1) * LANES], kc[:, p * LANES:(p + 1) * LANES]], axis=0)
        vv = jnp.concatenate([kp[:, 256 + p * LANES:256 + (p + 1) * LANES],
                              kc[:, 256 + p * LANES:256 + (p + 1) * LANES]], axis=0)
        qh = jnp.concatenate([q[:, (h * B_REP + j) * LANES:(h * B_REP + j + 1) * LANES]
                              for j in range(B_REP)], axis=0)
        s = _dot_nt(qh, kk)
        parts = []
        for j in range(B_REP):
            sj = s[j * tn:(j + 1) * tn, :] - SLOPES_B[h * B_REP + j] * distf
            parts.append(jnp.where(valid, sj, NEG))
        sm = jnp.concatenate(parts, axis=0)
        m = jnp.max(sm, axis=-1, keepdims=True)
        e = jnp.exp(sm - m)
        den = jnp.sum(e, axis=-1, keepdims=True)
        og = _dot(e.astype(BF16), vv) / den
        lse = m + jnp.log(den)
        halves = []
        for j in range(B_REP):
            piece = og[j * tn:(j + 1) * tn, :]
            if (h % 2) != j:
                piece = pltpu.roll(piece, HEAD_DIM, 1)
            halves.append(piece)
        o_cols.append(_pick_half(lane, halves[0], halves[1]))
        l_cols.append(_pick_half(lane, jnp.broadcast_to(lse[0:tn], (tn, LANES)),
                                 jnp.broadcast_to(lse[tn:2 * tn], (tn, LANES))))
    o = jnp.concatenate(o_cols, axis=1)
    l = jnp.concatenate(l_cols, axis=1)
    o_ref[...] = o.reshape(o_ref.shape)
    lse_ref[...] = l.reshape(lse_ref.shape)


def _dilated_prompt(q, kv, gi, n_b, s_len):
    win, dil = B_GROUPS[gi]
    wr = win // dil
    n = s_len // dil
    tn = min(CM_TILE, n)
    assert n % tn == 0 and tn % wr == 0
    ratio = tn // wr
    qw = B_HPG * LANES
    if dil == 1:
        in_specs = [pl.BlockSpec((None, tn, qw), lambda b, c, i: (b, i, 0)),
                    pl.BlockSpec((None, tn, 512), lambda b, c, i: (b, i, 0)),
                    pl.BlockSpec((None, wr, 512), lambda b, c, i: (b, jnp.maximum(i * ratio - 1, 0), 0))]
        out_spec = pl.BlockSpec((None, tn, B_O), lambda b, c, i: (b, i, 0))
        out_shape = jax.ShapeDtypeStruct((n_b, s_len, B_O), F32)
    else:
        rpt = CM_TILE // dil
        nt, npv = tn // rpt, wr // rpt
        in_specs = [pl.BlockSpec((None, nt, None, rpt, qw), lambda b, c, i: (b, i, c, 0, 0)),
                    pl.BlockSpec((None, nt, None, rpt, 512), lambda b, c, i: (b, i, c, 0, 0)),
                    pl.BlockSpec((None, npv, None, rpt, 512),
                                 lambda b, c, i: (b, jnp.maximum(i * ratio - 1, 0), c, 0, 0))]
        out_spec = pl.BlockSpec((None, nt, None, rpt, B_O), lambda b, c, i: (b, i, c, 0, 0))
        out_shape = jax.ShapeDtypeStruct((n_b, s_len // CM_TILE, dil, rpt, B_O), F32)
    return pl.pallas_call(
        functools.partial(_dil_kernel, tn=tn, wr=wr),
        grid=(n_b, dil, n // tn),
        in_specs=in_specs,
        out_specs=[out_spec, out_spec],
        out_shape=[out_shape, out_shape],
        compiler_params=_cparams(("arbitrary", "arbitrary", "arbitrary")),
        name="dilated_prompt",
    )(q, kv, kv)


def _dil_sample_kernel(q_ref, kn_ref, cache_ref, o_ref, lse_ref, *, t_new, wb):
    rows = B_KV * B_REP * t_new
    rid = lax.broadcasted_iota(jnp.int32, (rows, 1), 0)
    pos = wb + (rid & (t_new - 1))
    lane = lax.broadcasted_iota(jnp.int32, (t_new, LANES), 1)
    zero = jnp.zeros((t_new, LANES), BF16)
    new = jnp.concatenate([kn_ref[...], jnp.zeros((LANES - t_new, 512), F32)], axis=0).astype(BF16)
    nidx = wb + lax.broadcasted_iota(jnp.int32, (1, LANES), 1)
    dn = pos - nidx
    for gi, (win, dil) in enumerate(B_GROUPS):
        lo = wb - min(win, wb)
        nk = wb - lo
        blocks = []
        for h in range(B_KV):
            for j in range(B_REP):
                c = gi * B_HPG + h * B_REP + j
                slot = q_ref[:, c * LANES:(c + 1) * LANES]
                blocks.append(jnp.concatenate([slot, zero] if h < 2 else [zero, slot], axis=1))
        qbd = jnp.concatenate(blocks, axis=0)
        kt = cache_ref[0:256, lo:wb].astype(BF16)
        vt = cache_ref[256:512, lo:wb].astype(BF16)
        s_c = _dot(qbd, kt)
        s_n = _dot_nt(qbd, new[:, 0:256])
        kidx = lo + lax.broadcasted_iota(jnp.int32, (1, nk), 1)
        dc = pos - kidx
        vc = (dc <= win) & ((dc & (dil - 1)) == 0)
        vn = (dn >= 0) & ((dn & (dil - 1)) == 0)
        dcf, dnf = dc.astype(F32), dn.astype(F32)
        pc, pn = [], []
        for hj in range(B_HPG):
            r = slice(hj * t_new, (hj + 1) * t_new)
            sl = SLOPES_B[hj] / dil
            pc.append(jnp.where(vc[r], s_c[r] - sl * dcf[r], NEG))
            pn.append(jnp.where(vn[r], s_n[r] - sl * dnf[r], NEG))
        sc, sn = jnp.concatenate(pc, axis=0), jnp.concatenate(pn, axis=0)
        m = jnp.maximum(jnp.max(sc, axis=-1, keepdims=True), jnp.max(sn, axis=-1, keepdims=True))
        ec, en = jnp.exp(sc - m), jnp.exp(sn - m)
        den = jnp.sum(ec, axis=-1, keepdims=True) + jnp.sum(en, axis=-1, keepdims=True)
        ow = (_dot_nt(ec.astype(BF16), vt) + _dot(en.astype(BF16), new[:, 256:512])) / den
        lse = m + jnp.log(den)
        for h in range(B_KV):
            halves, lhalves = [], []
            for j in range(B_REP):
                r0 = (h * B_REP + j) * t_new
                piece = ow[r0:r0 + t_new, (h // 2) * LANES:(h // 2 + 1) * LANES]
                if (h % 2) != j:
                    piece = pltpu.roll(piece, HEAD_DIM, 1)
                halves.append(piece)
                lhalves.append(jnp.broadcast_to(lse[r0:r0 + t_new], (t_new, LANES)))
            c0 = gi * B_O + h * LANES
            o_ref[:, c0:c0 + LANES] = _pick_half(lane, halves[0], halves[1])
            lse_ref[:, c0:c0 + LANES] = _pick_half(lane, lhalves[0], lhalves[1])


def _dilated_sample(q_exp, kv_new, cache_t, n, t_new):
    wb = cache_t.shape[2]
    assert t_new & (t_new - 1) == 0 and t_new <= LANES and all(w <= wb for w, _ in B_GROUPS)
    o, lse = pl.pallas_call(
        functools.partial(_dil_sample_kernel, t_new=t_new, wb=wb),
        grid=(n,),
        in_specs=[pl.BlockSpec((t_new, N_B_GROUPS * B_HPG * LANES), lambda s: (s, 0)),
                  pl.BlockSpec((t_new, 512), lambda s: (s, 0)),
                  pl.BlockSpec((None, 512, wb), lambda s: (s, 0, 0))],
        out_specs=[pl.BlockSpec((t_new, N_B_GROUPS * B_O), lambda s: (s, 0)),
                   pl.BlockSpec((t_new, N_B_GROUPS * B_O), lambda s: (s, 0))],
        out_shape=[jax.ShapeDtypeStruct((n * t_new, N_B_GROUPS * B_O), F32),
                   jax.ShapeDtypeStruct((n * t_new, N_B_GROUPS * B_O), F32)],
        compiler_params=_cparams(("arbitrary",)),
        name="dilated_sample",
    )(q_exp, kv_new, cache_t)
    return ([o[:, g * B_O:(g + 1) * B_O] for g in range(N_B_GROUPS)],
            [lse[:, g * B_O:(g + 1) * B_O] for g in range(N_B_GROUPS)])


def _nsa_sample_kernel(pt_ref, q_ref, *refs, n_pages, t_new, past, wbuf, ns, nslot):
    ck_refs = refs[0:n_pages]
    sp_refs = refs[n_pages:2 * n_pages]
    kvn_ref, win_ref, es_ref = refs[2 * n_pages:2 * n_pages + 3]
    oc_ref, os_ref, ow_ref = refs[2 * n_pages + 3:2 * n_pages + 6]
    ck_s, kall = refs[2 * n_pages + 6:]
    del pt_ref
    page = sp_refs[0].shape[1]
    hs = nslot // 2
    per_page = page // CMP_LEN
    ck_s[...] = jnp.zeros(ck_s.shape, F32)
    for p in range(n_pages):
        for n in range(per_page):
            c = p * per_page + n
            slot = (c % 2) * hs + c // 2
            ck_s[slot:slot + 1, :] = ck_refs[p][n:n + 1, :]
    for p in range(n_pages):
        kall[:, p * page:(p + 1) * page] = sp_refs[p][...].astype(BF16)
    new = jnp.concatenate([kvn_ref[...], jnp.zeros((LANES - t_new, 3 * A_KVW), F32)], axis=0).astype(BF16)

    rows = A_HEADS * t_new
    zero = jnp.zeros((t_new, LANES), BF16)
    blocks = []
    for g in range(A_KV):
        for r in range(A_REP):
            h = g * A_REP + r
            slot = q_ref[:, h * LANES:(h + 1) * LANES]
            blocks.append(jnp.concatenate([slot, zero] if g < 2 else [zero, slot], axis=1))
    qbd = jnp.concatenate(blocks, axis=0)
    rid = lax.broadcasted_iota(jnp.int32, (rows, 1), 0)
    pos = past + (rid & (t_new - 1))
    posf = pos.astype(F32)
    npos = past + lax.broadcasted_iota(jnp.int32, (1, LANES), 1)
    nd = pos - npos
    ndf = nd.astype(F32)

    def hs_(h):
        return slice(h * t_new, (h + 1) * t_new)

    def head_rows(fn):
        return jnp.concatenate([fn(h) for h in range(A_HEADS)], axis=0)

    def finish(ow):
        pieces = []
        for g in range(A_KV):
            for r in range(A_REP):
                r0 = (g * A_REP + r) * t_new
                pieces.append((ow[r0:r0 + t_new, (g // 2) * LANES:(g // 2 + 1) * LANES], g % 2))
        return _assemble(pieces, t_new)

    n_c = n_pages * per_page
    slot_i = lax.broadcasted_iota(jnp.int32, (1, nslot), 1)
    sl_lo = slot_i & (hs - 1)
    cidx = 2 * sl_lo + jnp.where(slot_i >= hs, 1, 0)
    svalid = (sl_lo < n_c // 2) & (cidx * CMP_LEN + (CMP_LEN - 1) <= pos)
    dist = posf - (cidx.astype(F32) * CMP_LEN + 0.5 * (CMP_LEN - 1))
    s = _dot_nt(qbd, ck_s[:, 0:256].astype(BF16))
    sm = head_rows(lambda h: jnp.where(svalid[hs_(h)], s[hs_(h)] - SLOPES_A[h] * dist[hs_(h)], -jnp.inf))
    m = jnp.max(sm, axis=-1, keepdims=True)
    m = jnp.where(m == -jnp.inf, 0.0, m)
    e = jnp.exp(sm - m)
    den = jnp.sum(e, axis=-1, keepdims=True)
    pc = e / jnp.where(den > 0, den, 1.0)
    oc_ref[...] = finish(_dot(pc.astype(BF16), ck_s[:, 256:512].astype(BF16)))
    g_rows = A_KV * t_new
    ps = jnp.concatenate(
        [sum(pc[hs_(g * A_REP + r)] for r in range(A_REP)) for g in range(A_KV)], axis=0)
    imp = ps + pltpu.roll(ps, hs, 1)
    blk = lax.broadcasted_iota(jnp.int32, (1, nslot), 1)
    gid = lax.broadcasted_iota(jnp.int32, (g_rows, 1), 0)
    cur = jnp.right_shift(past + (gid & (t_new - 1)), 6)
    imp = jnp.where(blk < n_c // 2, imp, 0.0)
    imp = jnp.where((blk == 0) | (blk == cur) | (blk == cur - 1), A_REP + 1.0, imp)
    imp = jnp.where(blk <= cur, imp, -1.0)
    imp = jnp.where(blk < ns, imp, -3.0)
    cnt = jnp.zeros((g_rows, nslot), F32)
    for j in range(ns):
        col = imp[:, j:j + 1]
        ahead = (col > imp) | ((col == imp) & (blk > j))
        cnt = cnt + jnp.where(ahead, 1.0, 0.0)
    sel = jnp.where((cnt < min(N_SEL, ns)) & (blk < ns), 1.0, 0.0).astype(BF16)
    selk = _dot(sel, es_ref[...])
    kpos = lax.broadcasted_iota(jnp.int32, (1, past), 1)
    kposf = kpos.astype(F32)
    s_c = _dot(qbd, kall[0:256, :])
    s_n = _dot_nt(qbd, new[:, 512:768])

    def sel_c(h):
        g = h // A_REP
        return jnp.where(selk[g * t_new:(g + 1) * t_new, 0:past] > 0.5,
                         s_c[hs_(h)] - SLOPES_A[h] * (posf[hs_(h)] - kposf), NEG)

    def sel_n(h):
        g = h // A_REP
        ok = (selk[g * t_new:(g + 1) * t_new, past:past + LANES] > 0.5) & (nd[hs_(h)] >= 0)
        return jnp.where(ok, s_n[hs_(h)] - SLOPES_A[h] * ndf[hs_(h)], NEG)

    sc, sn = head_rows(sel_c), head_rows(sel_n)
    m = jnp.maximum(jnp.max(sc, axis=-1, keepdims=True), jnp.max(sn, axis=-1, keepdims=True))
    ec, en = jnp.exp(sc - m), jnp.exp(sn - m)
    den = jnp.sum(ec, axis=-1, keepdims=True) + jnp.sum(en, axis=-1, keepdims=True)
    os_ref[...] = finish((_dot_nt(ec.astype(BF16), kall[256:512, :]) + _dot(en.astype(BF16), new[:, 768:1024])) / den)
    wpos = (past - wbuf) + lax.broadcasted_iota(jnp.int32, (1, wbuf), 1)
    wd = pos - wpos
    wvalid = (wd <= WIN_A) & (wpos >= 0)
    wdf = wd.astype(F32)
    s_c = _dot(qbd, win_ref[0:256, :].astype(BF16))
    s_n = _dot_nt(qbd, new[:, 1024:1280])
    sc = head_rows(lambda h: jnp.where(wvalid[hs_(h)], s_c[hs_(h)] - SLOPES_A[h] * wdf[hs_(h)], NEG))
    sn = head_rows(lambda h: jnp.where(nd[hs_(h)] >= 0, s_n[hs_(h)] - SLOPES_A[h] * ndf[hs_(h)], NEG))
    m = jnp.maximum(jnp.max(sc, axis=-1, keepdims=True), jnp.max(sn, axis=-1, keepdims=True))
    ec, en = jnp.exp(sc - m), jnp.exp(sn - m)
    den = jnp.sum(ec, axis=-1, keepdims=True) + jnp.sum(en, axis=-1, keepdims=True)
    ow_ref[...] = finish((_dot_nt(ec.astype(BF16), win_ref[256:512, :].astype(BF16)) +
                          _dot(en.astype(BF16), new[:, 1280:1536])) / den)


def _nsa_sample(q_exp, ckv_pool, slc_t, kv_new, win_t, page_table, layer, n, t_new):
    n_pages = page_table.shape[1]
    page = slc_t.shape[3]
    past = n_pages * page
    wbuf = win_t.shape[3]
    per_page = page // CMP_LEN
    tk = past + t_new
    ns = -(-tk // SLC_LEN)
    assert past % CMP_LEN == 0 and t_new < CMP_LEN and t_new & (t_new - 1) == 0 and past >= wbuf
    assert wbuf >= WIN_A and t_new <= SLC_LEN
    hs = HEAD_DIM
    while hs < max(n_pages * per_page // 2, ns):
        hs *= 2
    nslot = 2 * hs
    es = (np.arange(nslot)[:, None] == (np.arange(past + LANES)[None, :] // SLC_LEN)).astype(np.float32)
    es[:, past + t_new:] = 0.0
    es = jnp.asarray(es, BF16)
    ck_specs = [pl.BlockSpec((None, per_page, 512), functools.partial(lambda s, pt, p: (pt[s, p], 0, 0), p=p))
                for p in range(n_pages)]
    sp_specs = [pl.BlockSpec((None, None, 512, page), functools.partial(lambda s, pt, p: (pt[s, p], layer, 0, 0), p=p))
                for p in range(n_pages)]
    grid_spec = pltpu.PrefetchScalarGridSpec(
        num_scalar_prefetch=1,
        grid=(n,),
        in_specs=[pl.BlockSpec((t_new, A_HEADS * LANES), lambda s, pt: (s, 0))] + ck_specs + sp_specs +
                 [pl.BlockSpec((t_new, 3 * A_KVW), lambda s, pt: (s, 0)),
                  pl.BlockSpec((None, None, 512, wbuf), lambda s, pt: (s, layer, 0, 0)),
                  pl.BlockSpec(es.shape, lambda s, pt: (0, 0))],
        out_specs=[pl.BlockSpec((t_new, A_Q), lambda s, pt: (s, 0))] * 3,
        scratch_shapes=[pltpu.VMEM((nslot, 512), F32),
                        pltpu.VMEM((512, past), BF16)],
    )
    return pl.pallas_call(
        functools.partial(_nsa_sample_kernel, n_pages=n_pages, t_new=t_new, past=past, wbuf=wbuf,
                          ns=ns, nslot=nslot),
        grid_spec=grid_spec,
        out_shape=[jax.ShapeDtypeStruct((n * t_new, A_Q), F32)] * 3,
        compiler_params=_cparams(("arbitrary",)),
        name="nsa_sample",
    )(page_table, q_exp, *([ckv_pool] * n_pages), *([slc_t] * n_pages), kv_new, win_t, es)


def _take_cols(w, src):
    src = np.asarray(src)
    cols = jnp.take(w, jnp.asarray(np.maximum(src, 0)), axis=1)
    return jnp.where(jnp.asarray(src >= 0)[None, :], cols, 0.0).astype(BF16)


A_Q0, A_KC0, A_KS0, A_KW0 = 0, A_Q, A_Q + A_KVW, A_Q + 2 * A_KVW
A_GATE0 = A_Q + 3 * A_KVW
A_Z0 = A_GATE0 + 3 * A_HEADS
A_QM0 = A_Z0 + 3 * A_Q
A_ZM0 = A_QM0 + MEM_W
A_QW = A_HEADS * LANES
A_RESTW = 3 * A_Q + 2 * MEM_W + LANES


def _a_q_cols():
    src = []
    for h in range(A_HEADS):
        g = h // A_REP
        slot = [-1] * LANES
        for d in range(HEAD_DIM):
            slot[(g % 2) * HEAD_DIM + d] = A_Q0 + h * HEAD_DIM + d
        src += slot
    return src


def _a_rest_cols():
    return (list(range(A_Z0, A_Z0 + 3 * A_Q)) + list(range(A_QM0, A_QM0 + MEM_W)) +
            list(range(A_ZM0, A_ZM0 + MEM_W)) + list(range(A_GATE0, A_GATE0 + 3 * A_HEADS)) +
            [-1] * (LANES - 3 * A_HEADS))


def _b_q_cols():
    src = []
    for gi in range(N_B_GROUPS):
        for h in range(B_KV):
            for j in range(B_REP):
                slot = [-1] * LANES
                for d in range(HEAD_DIM):
                    slot[(h % 2) * HEAD_DIM + d] = gi * B_O + (h * B_REP + j) * HEAD_DIM + d
                src += slot
    return src


def _blockdiag2(w):
    z = jnp.zeros_like(w)
    top = jnp.concatenate([w, z], axis=-1)
    bot = jnp.concatenate([z, w], axis=-1)
    return jnp.concatenate([top, bot], axis=-2)


def _class_perm(dil):
    p = np.zeros((CM_TILE, CM_TILE), np.float32)
    s = np.arange(CM_TILE)
    p[(s % dil) * (CM_TILE // dil) + s // dil, s] = 1.0
    return p


def kernel(x_prompt, x_sample, mem_prompt, cache_cmp_kv, cache_slc_kv, cache_win_kv, cache_dil_kv, cache_mem_kv,
           page_table, g_pre, g_post, g_mem, w_mem_kv, w_in_a, w_out_a, cmp_pos, cmp_w1, cmp_w2,
           g_kv_b, w_kv_b, w_in_b, w_out_b):
    n_b, s_len, d = x_prompt.shape
    n_s, t_s, _ = x_sample.shape
    depth = g_pre.shape[0]
    n_a = w_in_a.shape[0]
    n_pool, page = cache_cmp_kv.shape[:2]
    mem_len = mem_prompt.shape[1]
    wbuf_a = cache_win_kv.shape[1]
    wbuf_b = cache_dil_kv.shape[1]
    ns_p = s_len // SLC_LEN
    m_p = n_b * s_len
    assert s_len % TM == 0 and mem_len % TM == 0 or mem_len == TM
    spb = s_len // TM
    nkt = s_len // TK

    xp = x_prompt.reshape(m_p, d)
    xs = x_sample.reshape(n_s * t_s, d)
    mem2d = mem_prompt.reshape(n_b * mem_len, d)

    egate = np.zeros((LANES, 3 * A_Q), np.float32)
    for b in range(3):
        for h in range(A_HEADS):
            egate[b * A_HEADS + h, b * A_Q + h * HEAD_DIM:b * A_Q + (h + 1) * HEAD_DIM] = 1.0
    egate = jnp.asarray(egate, BF16)
    etile = (np.arange(ns_p)[None, :, None] ==
             (np.arange(nkt)[:, None, None] * TK + np.arange(TK)[None, None, :]) // SLC_LEN)
    etile = jnp.asarray(etile.astype(np.float32), BF16)
    anymat = np.zeros((ns_p, LANES), np.float32)
    anymat[np.arange(ns_p), np.arange(ns_p) // (TK // SLC_LEN)] = 1.0
    anymat = jnp.asarray(anymat, BF16)
    perms = [jnp.asarray(_class_perm(dil), BF16) for _, dil in B_GROUPS[1:]]
    unperms = [jnp.asarray(_class_perm(dil).T, BF16) for _, dil in B_GROUPS[1:]]

    cache_cmp2 = cache_cmp_kv.reshape(n_pool * (page // CMP_LEN), CMP_LEN * n_a * A_KVW)
    slc_t = jnp.transpose(cache_slc_kv, (0, 2, 3, 4, 5, 1)).reshape(n_pool, n_a, A_KVW, page)
    win_t = jnp.transpose(cache_win_kv, (0, 2, 3, 4, 5, 1)).reshape(n_s, n_a, A_KVW, wbuf_a)
    dil_t = jnp.transpose(cache_dil_kv, (0, 2, 3, 4, 1)).reshape(n_s, 2 * B_KV * HEAD_DIM, wbuf_b)
    mem_t = jnp.transpose(cache_mem_kv, (0, 2, 3, 4, 5, 1)).reshape(n_s, depth, 2 * MEM_W, mem_len)

    a_q_cols, a_rest_cols, b_q_cols = _a_q_cols(), _a_rest_cols(), _b_q_cols()
    row_spec = lambda w: pl.BlockSpec((TM, w), lambda i: (i, 0))

    kct_l, kst_l, kwt_l, kv_s, mkvt_l = [], [], [], [], []
    kvbt_p = kvb_s = None
    kvb_cm = None
    for l in range(depth):
        mkvt = _proj(mem2d, g_mem[l], [w_mem_kv[l].T.astype(BF16)], [],
                     [(True, 0, a, b, 0, (_st_t(0, a),)) for a, b in _chunks(0, 2 * MEM_W)],
                     [jax.ShapeDtypeStruct((n_b, 2 * MEM_W, mem_len), F32)],
                     [_t_spec(2 * MEM_W, mem_len // min(TM, mem_len), min(TM, mem_len))])[0]
        mkvt_l.append(mkvt)
        mkvt4 = mkvt.reshape(n_b, 1, 2 * MEM_W, mem_len)
        if l < n_a:
            w = w_in_a[l]
            w_n = jnp.concatenate([w[:, A_KC0:A_KC0 + A_KVW].astype(BF16), _take_cols(w, a_q_cols),
                                   _take_cols(w, a_rest_cols)], axis=1)
            w_t = w[:, A_KC0:A_KC0 + 3 * A_KVW].T.astype(BF16)
            w_out = w_out_a[l].astype(BF16)
            pos2 = jnp.concatenate([cmp_pos[l], cmp_pos[l]], axis=-1).transpose(1, 0, 2)
            w1bd = _blockdiag2(cmp_w1[l]).transpose(1, 0, 2, 3).astype(BF16)
            w2bd = _blockdiag2(cmp_w2[l]).astype(BF16)
            w2bdt = jnp.swapaxes(w2bd, 1, 2)
            plan = []
            for j in range(3):
                stores = [_st_t(j, 0)]
                if j > 0:
                    stores.append(_st_ttile(3, (j - 1) * A_KVW))
                plan.append((True, 1, j * A_KVW, (j + 1) * A_KVW, 0, tuple(stores)))
            for j in range(4):
                plan.append((False, 0, j * LANES, (j + 1) * LANES, 0, (_st_rows(4 + j, 0),)))
            for a, b in _chunks(A_KVW, A_KVW + A_QW):
                plan.append((False, 0, a, b, 0, (_st_rows(8, a - A_KVW, QSCALE),)))
            for a, b in _chunks(A_KVW + A_QW, A_KVW + A_QW + A_RESTW):
                plan.append((False, 0, a, b, 0, (_st_rows(9, a - A_KVW - A_QW),)))
            shapes = ([jax.ShapeDtypeStruct((n_b, A_KVW, s_len), F32)] * 3 +
                      [jax.ShapeDtypeStruct((n_b, s_len // LANES, 2 * A_KVW, LANES), BF16)] +
                      [jax.ShapeDtypeStruct((m_p, LANES), F32)] * 4 +
                      [jax.ShapeDtypeStruct((m_p, A_QW), BF16), jax.ShapeDtypeStruct((m_p, A_RESTW), F32)])
            specs = ([_t_spec(A_KVW, spb, TM)] * 3 +
                     [pl.BlockSpec((None, TM // LANES, 2 * A_KVW, LANES), lambda i: (i // spb, i % spb, 0, 0))] +
                     [row_spec(LANES)] * 4 + [row_spec(A_QW), row_spec(A_RESTW)])
            outs = _proj(xp, g_pre[l], [w_n, w_t], [], plan, shapes, specs)
            kct, kst, kwt, kvt = outs[0:4]
            slabs, q, rest = outs[4:8], outs[8], outs[9]
            kct_l.append(kct)
            kst_l.append(kst)
            kwt_l.append(kwt)
            ckt, cv = _compress_prompt(slabs, pos2, w1bd, w2bd, w2bdt, n_b, s_len)
            q3 = q.reshape(n_b, s_len, A_QW)
            o_cmp, sel, flags = _nsa_cmp(q3, ckt, cv, anymat, n_b, s_len, TQ)
            flags = (flags[:, :, 0:A_KV, 0:nkt] > 0.5).astype(jnp.int32).reshape(-1)
            o_slc, o_win = _nsa_sw(flags, q3, kvt, sel, etile, n_b, s_len, TQ, TK)
            om = _mem_attend(rest, 9, mkvt4, 0, n_b, s_len, min(512, s_len))
            xp = _merge_a(xp, o_cmp.reshape(-1, A_Q), o_slc.reshape(-1, A_Q), o_win.reshape(-1, A_Q),
                          rest, om, w_out, g_post[l], egate)
            w_rows = jnp.concatenate([w[:, A_KC0:A_KC0 + 3 * A_KVW].astype(BF16), w_n[:, A_KVW:]], axis=1)
            kv, q, rest = _proj_rows(xs, g_pre[l], w_rows,
                                     [(0, 3 * A_KVW, F32, 1.0), (3 * A_KVW, A_QW, BF16, QSCALE),
                                      (3 * A_KVW + A_QW, A_RESTW, F32, 1.0)])
            kv_s.append(kv)
            ckv_pool = _compress_pool(cache_cmp2, n_a, l, pos2, w1bd, w2bd, 1024)
            ckv_pool = ckv_pool.reshape(n_pool, page // CMP_LEN, 512)
            o_cmp, o_slc, o_win = _nsa_sample(q, ckv_pool, slc_t, kv, win_t, page_table, l, n_s, t_s)
            om = _mem_attend(rest, 9, mem_t, l, n_s, t_s, t_s)
            xs = _merge_a(xs, o_cmp, o_slc, o_win, rest, om, w_out, g_post[l], egate)
            if l == n_a - 1:
                w_kv = w_kv_b.astype(BF16)
                plan = [(True, 1, 0, 512, 0, (_st_t(0, 0),)),
                        (False, 0, 0, 512, 0, (_st_rows(1, 0),)),
                        (False, 0, 0, 512, 1, (_st_cm(2, 0, B_GROUPS[1][1]),)),
                        (False, 0, 0, 512, 2, (_st_cm(3, 0, B_GROUPS[2][1]),))]
                shapes = [jax.ShapeDtypeStruct((n_b, 512, s_len), F32), jax.ShapeDtypeStruct((m_p, 512), BF16)]
                specs = [_t_spec(512, spb, TM), row_spec(512)]
                for _, dil in B_GROUPS[1:]:
                    shapes.append(jax.ShapeDtypeStruct((m_p // CM_TILE, dil, CM_TILE // dil, 512), BF16))
                    specs.append(_cm_spec(dil, 512))
                kvbt_p, kv0, kv1, kv2 = _proj(xp, g_kv_b, [w_kv, w_kv_b.T.astype(BF16)], perms, plan, shapes, specs)
                kvb_cm = [kv0.reshape(n_b, s_len, 512)] + [
                    a.reshape((n_b, s_len // CM_TILE) + a.shape[1:]) for a in (kv1, kv2)]
                kvb_s = _proj_rows(xs, g_kv_b, w_kv, [(0, 512, F32, 1.0)])[0]
        else:
            lb = l - n_a
            w = w_in_b[lb]
            w_n = jnp.concatenate([_take_cols(w, b_q_cols), w[:, B_Q:].astype(BF16)], axis=1)
            w_out = w_out_b[lb].astype(BF16)
            qw = B_HPG * LANES
            restw = B_O + 2 * MEM_W
            plan, shapes, specs = [], [], []
            for gi, (_, dil) in enumerate(B_GROUPS):
                for a, b in _chunks(gi * qw, (gi + 1) * qw):
                    st = _st_rows(gi, a - gi * qw, QSCALE) if dil == 1 else _st_cm(gi, a - gi * qw, dil, QSCALE)
                    plan.append((False, 0, a, b, gi, (st,)))
                if dil == 1:
                    shapes.append(jax.ShapeDtypeStruct((m_p, qw), BF16))
                    specs.append(row_spec(qw))
                else:
                    shapes.append(jax.ShapeDtypeStruct((m_p // CM_TILE, dil, CM_TILE // dil, qw), BF16))
                    specs.append(_cm_spec(dil, qw))
            for a, b in _chunks(3 * qw, 3 * qw + restw):
                plan.append((False, 0, a, b, 0, (_st_rows(3, a - 3 * qw),)))
            shapes.append(jax.ShapeDtypeStruct((m_p, restw), F32))
            specs.append(row_spec(restw))
            q0, q1, q2, rest = _proj(xp, g_pre[l], [w_n], perms, plan, shapes, specs)
            qs = [q0.reshape(n_b, s_len, qw)] + [a.reshape((n_b, s_len // CM_TILE) + a.shape[1:]) for a in (q1, q2)]
            outs, lses = [], []
            for gi in range(N_B_GROUPS):
                o, lse = _dilated_prompt(qs[gi], kvb_cm[gi], gi, n_b, s_len)
                if gi == 0:
                    o, lse = o.reshape(m_p, B_O), lse.reshape(m_p, B_O)
                else:
                    o, lse = (a.reshape((m_p // CM_TILE,) + a.shape[2:]) for a in (o, lse))
                outs.append(o)
                lses.append(lse)
            om = _mem_attend(rest, 2, mkvt4, 0, n_b, s_len, min(512, s_len))
            xp = _merge_b(xp, outs, lses, rest, om, w_out, g_post[l], unperms)
            q, rest = _proj_rows(xs, g_pre[l], w_n, [(0, 3 * qw, BF16, QSCALE), (3 * qw, restw, F32, 1.0)])
            outs, lses = _dilated_sample(q, kvb_s, dil_t, n_s, t_s)
            om = _mem_attend(rest, 2, mem_t, l, n_s, t_s, t_s)
            xs = _merge_b(xs, outs, lses, rest, om, w_out, g_post[l], [])

    def from_t(arrs, n, t):
        a = jnp.stack(arrs, 1).reshape(n, len(arrs), 2, A_KV, HEAD_DIM, t)
        return jnp.transpose(a, (0, 5, 1, 2, 3, 4))

    def kv_stack(kvs, n, t, j):
        return jnp.stack([k[:, j * A_KVW:(j + 1) * A_KVW].reshape(n, t, 2, A_KV, HEAD_DIM) for k in kvs], 2)

    new_cmp_p = from_t(kct_l, n_b, s_len)
    new_slc_p = from_t(kst_l, n_b, s_len)
    wa = min(WIN_A, s_len)
    new_win_p = from_t([k[:, :, s_len - wa:] for k in kwt_l], n_b, wa)
    new_cmp_s = kv_stack(kv_s, n_s, t_s, 0)
    new_slc_s = kv_stack(kv_s, n_s, t_s, 1)
    win_s = kv_stack(kv_s, n_s, t_s, 2)
    new_win_s = jnp.concatenate([cache_win_kv, win_s], 1)[:, -wbuf_a:]
    wb = min(B_GROUPS[-1][0], s_len)
    new_dil_p = jnp.transpose(kvbt_p[:, :, s_len - wb:].reshape(n_b, 2, B_KV, HEAD_DIM, wb), (0, 4, 1, 2, 3))
    new_dil_s = jnp.concatenate([cache_dil_kv, kvb_s.reshape(n_s, t_s, 2, B_KV, HEAD_DIM)], 1)[:, -wbuf_b:]
    new_mem_p = jnp.transpose(jnp.stack(mkvt_l, 1).reshape(n_b, depth, 2, MEM_HEADS, HEAD_DIM, mem_len),
                              (0, 5, 1, 2, 3, 4))
    return (xp.reshape(n_b, s_len, d), xs.reshape(n_s, t_s, d), new_cmp_p, new_cmp_s, new_slc_p, new_slc_s,
            new_win_p, new_win_s, new_dil_p, new_dil_s, new_mem_p)
```

```python
import functools
import numpy as np
import jax
import jax.numpy as jnp
from jax import lax
from jax.experimental import pallas as pl
from jax.experimental.pallas import tpu as pltpu

F32 = jnp.float32
BF16 = jnp.bfloat16

HEAD_DIM = 64
LANES = 128
A_HEADS = 12
A_KV = 4
A_REP = A_HEADS // A_KV
CMP_LEN = 32
SLC_LEN = 64
N_SEL = 16
WIN_A = 512
B_GROUPS = ((128, 1), (512, 4), (2048, 16))
N_B_GROUPS = 3
B_KV = 4
B_REP = 2
B_HPG = B_KV * B_REP
MEM_HEADS = 4
EPS = 1e-6
A_Q = A_HEADS * HEAD_DIM
A_KVW = 2 * A_KV * HEAD_DIM
MEM_W = MEM_HEADS * HEAD_DIM
B_Q = N_B_GROUPS * B_HPG * HEAD_DIM
B_O = B_HPG * HEAD_DIM
QSCALE = HEAD_DIM ** -0.5
NEG = -1e30
VMEM_LIMIT = 56 * 1024 * 1024
TM = 256
TQ = 128
TK = 256
CM_TILE = 256


def _alibi(n):
    return [float(2.0 ** (-8.0 * i / n)) for i in range(1, n + 1)]


SLOPES_A = _alibi(A_HEADS)
SLOPES_B = _alibi(B_HPG)


def _cparams(sem):
    return pltpu.CompilerParams(dimension_semantics=sem, vmem_limit_bytes=VMEM_LIMIT)


def _silu(x):
    return x * (1.0 / (1.0 + jnp.exp(-x)))


def _dot_nt(a, b):
    return lax.dot_general(a, b, (((1,), (1,)), ((), ())), preferred_element_type=F32)


def _dot(a, b):
    return jnp.dot(a, b, preferred_element_type=F32)


def _split3(x):
    hi = x.astype(BF16)
    r1 = x - hi.astype(F32)
    mid = r1.astype(BF16)
    lo = (r1 - mid.astype(F32)).astype(BF16)
    return hi, mid, lo


def _pick_half(lane, a, b):
    return jnp.where(lane < HEAD_DIM, a, b)


def _assemble(pieces, rows):
    lane = lax.broadcasted_iota(jnp.int32, (rows, LANES), 1)
    cols = []
    for c in range(len(pieces) // 2):
        halves = []
        for k in (0, 1):
            arr, useful = pieces[2 * c + k]
            if useful != k:
                arr = pltpu.roll(arr, HEAD_DIM, 1)
            halves.append(arr)
        cols.append(_pick_half(lane, halves[0], halves[1]))
    return jnp.concatenate(cols, axis=1)


def _proj_kernel(x_ref, g_ref, *refs, n_w, n_p, plan):
    w_refs = refs[:n_w]
    p_refs = refs[n_w:n_w + n_p]
    out_refs = refs[n_w + n_p:]
    x = x_ref[...]
    ms = jnp.mean(x * x, axis=-1, keepdims=True)
    xn = ((x * lax.rsqrt(ms + EPS)) * g_ref[...]).astype(BF16)
    xs = [xn] + [_dot(p_ref[...], xn).astype(BF16) for p_ref in p_refs]
    for (transposed, wi, c0, c1, xi, stores) in plan:
        if transposed:
            y = _dot_nt(w_refs[wi][c0:c1, :], xs[xi])
        else:
            y = _dot(xs[xi], w_refs[wi][:, c0:c1])
        for store in stores:
            store(out_refs, y)


def _st_rows(oi, o0, scale=1.0):
    def store(out_refs, y):
        v = y if scale == 1.0 else y * scale
        out_refs[oi][:, o0:o0 + y.shape[1]] = v.astype(out_refs[oi].dtype)
    return store


def _st_t(oi, o0):
    def store(out_refs, y):
        out_refs[oi][o0:o0 + y.shape[0], :] = y.astype(out_refs[oi].dtype)
    return store


def _st_ttile(oi, o0):
    def store(out_refs, y):
        for u in range(y.shape[1] // LANES):
            out_refs[oi][u, o0:o0 + y.shape[0], :] = y[:, u * LANES:(u + 1) * LANES].astype(out_refs[oi].dtype)
    return store


def _st_cm(oi, o0, dil, scale=1.0):
    def store(out_refs, y):
        v = y if scale == 1.0 else y * scale
        v = v.astype(out_refs[oi].dtype)
        out_refs[oi][:, :, o0:o0 + y.shape[1]] = v.reshape(dil, y.shape[0] // dil, y.shape[1])
    return store


def _chunks(c0, c1, step=512):
    return [(a, min(a + step, c1)) for a in range(c0, c1, step)]


def _proj(x2d, g, weights, perms, plan, out_shapes, out_specs, tm=TM):
    m, d = x2d.shape
    tm = min(tm, m)
    const2 = lambda i: (0, 0)
    return pl.pallas_call(
        functools.partial(_proj_kernel, n_w=len(weights), n_p=len(perms), plan=tuple(plan)),
        grid=(m // tm,),
        in_specs=[pl.BlockSpec((tm, d), lambda i: (i, 0)), pl.BlockSpec((1, d), const2)] +
                 [pl.BlockSpec(w.shape, const2) for w in weights] +
                 [pl.BlockSpec(p.shape, const2) for p in perms],
        out_specs=out_specs,
        out_shape=out_shapes,
        compiler_params=_cparams(("arbitrary",)),
        name="proj",
    )(x2d, g.reshape(1, d), *weights, *perms)


def _proj_rows(x2d, g, w_bf, outs):
    m = x2d.shape[0]
    tm = min(TM, m)
    plan = []
    for oi, (c0, width, _, scale) in enumerate(outs):
        for a, b in _chunks(c0, c0 + width):
            plan.append((False, 0, a, b, 0, (_st_rows(oi, a - c0, scale),)))
    return _proj(x2d, g, [w_bf], [], plan,
                 [jax.ShapeDtypeStruct((m, o[1]), o[2]) for o in outs],
                 [pl.BlockSpec((tm, o[1]), lambda i: (i, 0)) for o in outs])


def _t_spec(width, spb, tm):
    return pl.BlockSpec((None, width, tm), lambda i: (i // spb, 0, i % spb))


def _cm_spec(dil, width):
    return pl.BlockSpec((None, dil, CM_TILE // dil, width), lambda i: (i, 0, 0, 0))


def _compress_pool_kernel(x_ref, posr_ref, w1_ref, w2_ref, o_ref, s0, s1, s2, s3, *, pages, pitch):
    scr = (s0, s1, s2, s3)
    page = x_ref.shape[2]
    per_page = page // CMP_LEN
    rows = per_page * pages

    def relayout(pg, carry):
        for idx in range(4):
            t = x_ref[pg, idx * LANES:(idx + 1) * LANES, :].T + posr_ref[idx // 2]
            for k in range(page // 8):
                n, l0 = (8 * k) // CMP_LEN, (8 * k) % CMP_LEN
                scr[idx][pl.ds(l0 * pitch + pg * per_page + n, 8, stride=pitch), :] = t[8 * k:8 * k + 8, :]
        return carry

    lax.fori_loop(0, pages, relayout, 0, unroll=2)
    for idx in range(4):
        kv = idx // 2
        acc = jnp.zeros((rows, LANES), F32)
        for l in range(CMP_LEN):
            xl = scr[idx][l * pitch:l * pitch + rows, :].astype(BF16)
            acc = acc + _dot(xl, w1_ref[l, kv])
        h = _silu(acc).astype(BF16)
        o_ref[:, idx * LANES:(idx + 1) * LANES] = _dot(h, w2_ref[kv])


def _compress_pool(cmp_t, layer, posr, w1bd, w2bd, pages):
    n_pool, _, _, page = cmp_t.shape
    per_page = page // CMP_LEN
    pages = max(p for p in range(1, min(pages, n_pool) + 1) if n_pool % p == 0 and (per_page * p) % 16 == 0)
    assert page % CMP_LEN == 0 and page == LANES
    rows = per_page * pages
    pitch = rows + 8
    assert rows % 16 == 0
    return pl.pallas_call(
        functools.partial(_compress_pool_kernel, pages=pages, pitch=pitch),
        grid=(n_pool // pages,),
        in_specs=[pl.BlockSpec((pages, None, 512, page), lambda i: (i, layer, 0, 0)),
                  pl.BlockSpec(posr.shape, lambda i: (0, 0, 0)),
                  pl.BlockSpec(w1bd.shape, lambda i: (0, 0, 0, 0)),
                  pl.BlockSpec(w2bd.shape, lambda i: (0, 0, 0))],
        out_specs=pl.BlockSpec((rows, 512), lambda i: (i, 0)),
        out_shape=jax.ShapeDtypeStruct((n_pool * per_page, 512), F32),
        scratch_shapes=[pltpu.VMEM((CMP_LEN * pitch, LANES), F32)] * 4,
        compiler_params=_cparams(("arbitrary",)),
        name="compress_pool",
    )(cmp_t, posr, w1bd, w2bd)


def _compress_prompt_kernel(s0, s1, s2, s3, pos_ref, w1_ref, w2_ref, w2t_ref, ckt_ref, cv_ref, *, nc):
    half = nc // 2
    slabs = (s0, s1, s2, s3)
    accs = [jnp.zeros((nc, LANES), F32) for _ in range(4)]
    for l in range(CMP_LEN):
        for idx in range(4):
            kv = idx // 2
            xe = slabs[idx][pl.ds(l, half, stride=2 * CMP_LEN), :]
            xo = slabs[idx][pl.ds(CMP_LEN + l, half, stride=2 * CMP_LEN), :]
            xb = (jnp.concatenate([xe, xo], axis=0) + pos_ref[l, kv:kv + 1, :]).astype(BF16)
            accs[idx] = accs[idx] + _dot(xb, w1_ref[l, kv])
    for idx in range(4):
        kv, p = idx // 2, idx % 2
        h = _silu(accs[idx]).astype(BF16)
        if kv == 0:
            ckt_ref[p * LANES:(p + 1) * LANES, :] = _dot_nt(w2t_ref[0], h)
        else:
            cv_ref[:, p * LANES:(p + 1) * LANES] = _dot(h, w2_ref[1])


def _compress_prompt(slabs, pos2, w1bd, w2bd, w2bdt, n_b, s_len):
    nc = s_len // CMP_LEN
    slab_spec = pl.BlockSpec((s_len, LANES), lambda b: (b, 0))
    return pl.pallas_call(
        functools.partial(_compress_prompt_kernel, nc=nc),
        grid=(n_b,),
        in_specs=[slab_spec] * 4 +
                 [pl.BlockSpec(pos2.shape, lambda b: (0, 0, 0)),
                  pl.BlockSpec(w1bd.shape, lambda b: (0, 0, 0, 0)),
                  pl.BlockSpec(w2bd.shape, lambda b: (0, 0, 0)),
                  pl.BlockSpec(w2bdt.shape, lambda b: (0, 0, 0))],
        out_specs=[pl.BlockSpec((None, 256, nc), lambda b: (b, 0, 0)),
                   pl.BlockSpec((None, nc, 256), lambda b: (b, 0, 0))],
        out_shape=[jax.ShapeDtypeStruct((n_b, 256, nc), F32),
                   jax.ShapeDtypeStruct((n_b, nc, 256), F32)],
        compiler_params=_cparams(("arbitrary",)),
        name="compress_prompt",
    )(*slabs, pos2, w1bd, w2bd, w2bdt)


def _topk_mask_t(imp_t, k):
    nb, cols = imp_t.shape
    row = lax.broadcasted_iota(jnp.int32, (nb, cols), 0)
    work = imp_t
    sel = jnp.zeros((nb, cols), F32)
    for _ in range(k):
        m = jnp.max(work, axis=0, keepdims=True)
        idx = jnp.min(jnp.where(work == m, row, nb), axis=0, keepdims=True)
        hit = row == idx
        sel = jnp.where(hit, 1.0, sel)
        work = jnp.where(hit, -3.0, work)
    return sel


def _group_q(q_ref, g):
    return jnp.concatenate(
        [q_ref[:, (g * A_REP + r) * LANES:(g * A_REP + r + 1) * LANES] for r in range(A_REP)], axis=0)


def _nsa_cmp_kernel(q_ref, ckt_ref, cv_ref, any_ref, o_ref, sel_ref, flag_ref, *, tq, nc, ns):
    i = pl.program_id(1)
    q0 = i * tq
    half = nc // 2
    pos = q0 + lax.broadcasted_iota(jnp.int32, (tq, 1), 0)
    posf = pos.astype(F32)
    tok = lax.broadcasted_iota(jnp.int32, (1, nc), 1)
    cidx = jnp.where(tok < half, 2 * tok, 2 * (tok - half) + 1)
    c_end = cidx * CMP_LEN + (CMP_LEN - 1)
    c_mid = cidx.astype(F32) * CMP_LEN + 0.5 * (CMP_LEN - 1)
    cmask = c_end <= pos
    dist = posf - c_mid
    blk = lax.broadcasted_iota(jnp.int32, (1, ns), 1)
    cur = jnp.right_shift(pos, 6)
    forced = (blk == 0) | (blk == cur) | (blk == cur - 1)
    allowed = blk <= cur
    pieces = []
    anys = []
    for g in range(A_KV):
        p = g // 2
        ckt = ckt_ref[p * LANES:(p + 1) * LANES, :].astype(BF16)
        cv = cv_ref[:, p * LANES:(p + 1) * LANES].astype(BF16)
        s = _dot(_group_q(q_ref, g), ckt)
        parts = []
        for r in range(A_REP):
            sr = s[r * tq:(r + 1) * tq, :] - SLOPES_A[g * A_REP + r] * dist
            parts.append(jnp.where(cmask, sr, -jnp.inf))
        sm = jnp.concatenate(parts, axis=0)
        m = jnp.max(sm, axis=-1, keepdims=True)
        m = jnp.where(m == -jnp.inf, 0.0, m)
        e = jnp.exp(sm - m)
        den = jnp.sum(e, axis=-1, keepdims=True)
        pc = e / jnp.where(den > 0, den, 1.0)
        og = _dot(pc.astype(BF16), cv)
        for r in range(A_REP):
            pieces.append((og[r * tq:(r + 1) * tq, :], g % 2))
        ps = pc[0:tq] + pc[tq:2 * tq] + pc[2 * tq:3 * tq]
        imp = ps[:, 0:half] + ps[:, half:nc]
        imp = jnp.where(allowed, imp, -1.0)
        imp = jnp.where(forced, -3.0, imp)
        sel = _topk_mask_t(imp.T, max(min(N_SEL, ns) - 3, 0)).T
        sel = jnp.where(forced, 1.0, sel)
        sel_ref[:, g * ns:(g + 1) * ns] = sel.astype(BF16)
        anys.append(jnp.max(sel, axis=0, keepdims=True))
    o_ref[...] = _assemble(pieces, tq)
    rows = jnp.concatenate(anys + [jnp.zeros((8 - A_KV, ns), F32)], axis=0).astype(BF16)
    flag_ref[...] = _dot(rows, any_ref[...])


def _nsa_cmp(q_exp, ckt, cv, anymat, n_b, s_len, tq):
    nc = ckt.shape[2]
    ns = s_len // SLC_LEN
    assert nc == 2 * ns
    return pl.pallas_call(
        functools.partial(_nsa_cmp_kernel, tq=tq, nc=nc, ns=ns),
        grid=(n_b, s_len // tq),
        in_specs=[pl.BlockSpec((None, tq, A_HEADS * LANES), lambda b, i: (b, i, 0)),
                  pl.BlockSpec((None, 256, nc), lambda b, i: (b, 0, 0)),
                  pl.BlockSpec((None, nc, 256), lambda b, i: (b, 0, 0)),
                  pl.BlockSpec(anymat.shape, lambda b, i: (0, 0))],
        out_specs=[pl.BlockSpec((None, tq, A_Q), lambda b, i: (b, i, 0)),
                   pl.BlockSpec((None, tq, A_KV * ns), lambda b, i: (b, i, 0)),
                   pl.BlockSpec((None, None, 8, LANES), lambda b, i: (b, i, 0, 0))],
        out_shape=[jax.ShapeDtypeStruct((n_b, s_len, A_Q), F32),
                   jax.ShapeDtypeStruct((n_b, s_len, A_KV * ns), BF16),
                   jax.ShapeDtypeStruct((n_b, s_len // tq, 8, LANES), F32)],
        compiler_params=_cparams(("arbitrary", "arbitrary")),
        name="nsa_cmp",
    )(q_exp, ckt, cv, anymat)


def _nsa_sw_kernel(fl_ref, q_ref, kv_ref, sel_ref, e_ref, oslc_ref, owin_ref,
                   m_ref, l_ref, acc_ref, *, tq, tk, ns, nq, nkt):
    b = pl.program_id(0)
    i = pl.program_id(1)
    q0 = i * tq
    per = tk // LANES
    trow = lax.broadcasted_iota(jnp.int32, (tq, 1), 0)
    qg = [_group_q(q_ref, g) for g in range(A_KV)]

    m_ref[...] = jnp.full(m_ref.shape, NEG, F32)
    l_ref[...] = jnp.zeros(l_ref.shape, F32)
    acc_ref[...] = jnp.zeros(acc_ref.shape, F32)

    def tiles(j, r0):
        return jnp.concatenate([kv_ref[j * per + u, r0:r0 + LANES, :] for u in range(per)], axis=1)

    def body(j, carry):
        k0 = j * tk
        krel = (k0 - q0) + lax.broadcasted_iota(jnp.int32, (1, tk), 1)
        causal = krel <= trow
        krelf = krel.astype(F32)
        for g in range(A_KV):
            p = g // 2

            @pl.when(fl_ref[((b * nq + i) * A_KV + g) * nkt + j] != 0)
            def _(g=g, p=p):
                s = _dot(qg[g], tiles(j, p * LANES))
                selk = _dot(sel_ref[:, g * ns:(g + 1) * ns], e_ref[j])
                valid = (selk > 0.5) & causal
                parts = []
                for r in range(A_REP):
                    sr = s[r * tq:(r + 1) * tq, :] + SLOPES_A[g * A_REP + r] * krelf
                    parts.append(jnp.where(valid, sr, NEG))
                sm = jnp.concatenate(parts, axis=0)
                m_old = m_ref[g]
                m_new = jnp.maximum(m_old, jnp.max(sm, axis=-1, keepdims=True))
                pe = jnp.exp(sm - m_new)
                alpha = jnp.exp(m_old - m_new)
                l_ref[g] = alpha * l_ref[g] + jnp.sum(pe, axis=-1, keepdims=True)
                acc_ref[g] = alpha * acc_ref[g] + _dot_nt(pe.astype(BF16), tiles(j, 256 + p * LANES))
                m_ref[g] = m_new
        return carry

    lax.fori_loop(0, (q0 + tq + tk - 1) // tk, body, 0)
    pieces = []
    for g in range(A_KV):
        og = acc_ref[g] / l_ref[g]
        for r in range(A_REP):
            pieces.append((og[r * tq:(r + 1) * tq, :], g % 2))
    oslc_ref[...] = _assemble(pieces, tq)

    nwt = WIN_A // LANES + tq // LANES
    t0 = jnp.maximum(i * (tq // LANES) - WIN_A // LANES, 0)
    nkw = nwt * LANES
    wrel = (t0 * LANES - q0) + lax.broadcasted_iota(jnp.int32, (1, nkw), 1)
    dist = trow - wrel
    wvalid = (dist >= 0) & (dist <= WIN_A)
    wrelf = wrel.astype(F32)
    pieces = []
    for g in range(A_KV):
        p = g // 2
        kk = jnp.concatenate([kv_ref[t0 + u, 512 + p * LANES:512 + (p + 1) * LANES, :] for u in range(nwt)], axis=1)
        vv = jnp.concatenate([kv_ref[t0 + u, 768 + p * LANES:768 + (p + 1) * LANES, :] for u in range(nwt)], axis=1)
        s = _dot(qg[g], kk)
        parts = []
        for r in range(A_REP):
            sr = s[r * tq:(r + 1) * tq, :] + SLOPES_A[g * A_REP + r] * wrelf
            parts.append(jnp.where(wvalid, sr, NEG))
        sm = jnp.concatenate(parts, axis=0)
        m = jnp.max(sm, axis=-1, keepdims=True)
        pe = jnp.exp(sm - m)
        den = jnp.sum(pe, axis=-1, keepdims=True)
        og = _dot_nt(pe.astype(BF16), vv) / den
        for r in range(A_REP):
            pieces.append((og[r * tq:(r + 1) * tq, :], g % 2))
    owin_ref[...] = _assemble(pieces, tq)


def _nsa_sw(flags, q_exp, kvt, sel, etile, n_b, s_len, tq, tk):
    ns = s_len // SLC_LEN
    nq = s_len // tq
    nkt = s_len // tk
    assert s_len >= WIN_A + tq and tk % tq == 0 and tq % LANES == 0
    grid_spec = pltpu.PrefetchScalarGridSpec(
        num_scalar_prefetch=1,
        grid=(n_b, nq),
        in_specs=[pl.BlockSpec((None, tq, A_HEADS * LANES), lambda b, i, fl: (b, i, 0)),
                  pl.BlockSpec((None, s_len // LANES, 1024, LANES), lambda b, i, fl: (b, 0, 0, 0)),
                  pl.BlockSpec((None, tq, A_KV * ns), lambda b, i, fl: (b, i, 0)),
                  pl.BlockSpec(etile.shape, lambda b, i, fl: (0, 0, 0))],
        out_specs=[pl.BlockSpec((None, tq, A_Q), lambda b, i, fl: (b, i, 0)),
                   pl.BlockSpec((None, tq, A_Q), lambda b, i, fl: (b, i, 0))],
        scratch_shapes=[pltpu.VMEM((A_KV, A_REP * tq, 1), F32),
                        pltpu.VMEM((A_KV, A_REP * tq, 1), F32),
                        pltpu.VMEM((A_KV, A_REP * tq, LANES), F32)],
    )
    return pl.pallas_call(
        functools.partial(_nsa_sw_kernel, tq=tq, tk=tk, ns=ns, nq=nq, nkt=nkt),
        grid_spec=grid_spec,
        out_shape=[jax.ShapeDtypeStruct((n_b, s_len, A_Q), F32),
                   jax.ShapeDtypeStruct((n_b, s_len, A_Q), F32)],
        compiler_params=_cparams(("arbitrary", "arbitrary")),
        name="nsa_sw",
    )(flags, q_exp, kvt, sel, etile)


def _mem_kernel(qm_ref, mkv_ref, o_ref, *, tt):
    lane = lax.broadcasted_iota(jnp.int32, (tt, LANES), 1)
    cols = []
    for c in range(MEM_HEADS // 2):
        qc = qm_ref[:, c * LANES:(c + 1) * LANES]
        mkt = mkv_ref[c * LANES:(c + 1) * LANES, :].astype(BF16)
        mvt = mkv_ref[MEM_W + c * LANES:MEM_W + (c + 1) * LANES, :].astype(BF16)
        halves = []
        for k in (0, 1):
            qh = jnp.where((lane < HEAD_DIM) == (k == 0), qc, 0.0).astype(BF16)
            s = _dot(qh, mkt) * QSCALE
            m = jnp.max(s, axis=-1, keepdims=True)
            e = jnp.exp(s - m)
            den = jnp.sum(e, axis=-1, keepdims=True)
            halves.append(_dot_nt(e.astype(BF16), mvt) / den)
        cols.append(_pick_half(lane, halves[0], halves[1]))
    o_ref[...] = jnp.concatenate(cols, axis=1)


def _mem_attend(rest, qm_blk, mkvt4, layer, n, t, tt):
    mem = mkvt4.shape[3]
    steps = t // tt
    return pl.pallas_call(
        functools.partial(_mem_kernel, tt=tt),
        grid=(n, steps),
        in_specs=[pl.BlockSpec((tt, MEM_W), lambda b, i: (b * steps + i, qm_blk)),
                  pl.BlockSpec((None, None, 2 * MEM_W, mem), lambda b, i: (b, layer, 0, 0))],
        out_specs=pl.BlockSpec((tt, MEM_W), lambda b, i: (b * steps + i, 0)),
        out_shape=jax.ShapeDtypeStruct((n * t, MEM_W), F32),
        compiler_params=_cparams(("arbitrary", "arbitrary")),
        name="mem_attend",
    )(rest, mkvt4)


def _post(x_ref, y, gp_ref, out_ref):
    ms = jnp.mean(y * y, axis=-1, keepdims=True)
    out_ref[...] = x_ref[...] + (y * lax.rsqrt(ms + EPS)) * gp_ref[...]


def _merge_a_kernel(x_ref, oc_ref, os_ref, ow_ref, z_ref, zm_ref, gate_ref, om_ref,
                    w_ref, gp_ref, eg_ref, out_ref):
    gs = 1.0 / (1.0 + jnp.exp(-gate_ref[...]))
    eg = eg_ref[...]
    gexp = sum(_dot(t, eg) for t in _split3(gs))
    o = jnp.zeros(oc_ref.shape, F32)
    for b, ob_ref in enumerate((oc_ref, os_ref, ow_ref)):
        o = o + (ob_ref[...] * _silu(z_ref[:, b * A_Q:(b + 1) * A_Q])) * gexp[:, b * A_Q:(b + 1) * A_Q]
    om = om_ref[...] * _silu(zm_ref[...])
    y = _dot(o.astype(BF16), w_ref[0:A_Q, :]) + _dot(om.astype(BF16), w_ref[A_Q:A_Q + MEM_W, :])
    _post(x_ref, y, gp_ref, out_ref)


def _merge_a(x2d, o_cmp, o_slc, o_win, rest, om, w_bf, g_post, egate, tm=TM):
    m, d = x2d.shape
    tm = min(tm, m)
    row = lambda i: (i, 0)
    return pl.pallas_call(
        _merge_a_kernel,
        grid=(m // tm,),
        in_specs=[pl.BlockSpec((tm, d), row),
                  pl.BlockSpec((tm, A_Q), row), pl.BlockSpec((tm, A_Q), row), pl.BlockSpec((tm, A_Q), row),
                  pl.BlockSpec((tm, 3 * A_Q), row),
                  pl.BlockSpec((tm, MEM_W), lambda i: (i, 10)),
                  pl.BlockSpec((tm, LANES), lambda i: (i, 22)),
                  pl.BlockSpec((tm, MEM_W), row),
                  pl.BlockSpec((A_Q + MEM_W, d), lambda i: (0, 0)),
                  pl.BlockSpec((1, d), lambda i: (0, 0)),
                  pl.BlockSpec((LANES, 3 * A_Q), lambda i: (0, 0))],
        out_specs=pl.BlockSpec((tm, d), row),
        out_shape=jax.ShapeDtypeStruct((m, d), F32),
        compiler_params=_cparams(("arbitrary",)),
        name="merge_a",
    )(x2d, o_cmp, o_slc, o_win, rest, rest, rest, om, w_bf, g_post.reshape(1, d), egate)


def _merge_b_kernel(x_ref, o0_ref, o1_ref, o2_ref, l0_ref, l1_ref, l2_ref, z_ref, zm_ref, om_ref,
                    w_ref, gp_ref, *rest, n_u):
    u_refs = rest[:n_u]
    out_ref = rest[n_u]

    def nat(ref, ui):
        v = ref[...]
        if v.ndim == 2:
            return v
        v = v.reshape(v.shape[0] * v.shape[1], v.shape[2])
        u = u_refs[ui][...]
        return sum(_dot(u, t) for t in _split3(v))

    o0, o1, o2 = nat(o0_ref, 0), nat(o1_ref, 0), nat(o2_ref, 1)
    l0, l1, l2 = nat(l0_ref, 0), nat(l1_ref, 0), nat(l2_ref, 1)
    mx = jnp.maximum(jnp.maximum(l0, l1), l2)
    e0, e1, e2 = jnp.exp(l0 - mx), jnp.exp(l1 - mx), jnp.exp(l2 - mx)
    den = e0 + e1 + e2
    o = (e0 / den) * o0 + (e1 / den) * o1 + (e2 / den) * o2
    o = o * _silu(z_ref[...])
    om = om_ref[...] * _silu(zm_ref[...])
    y = _dot(o.astype(BF16), w_ref[0:B_O, :]) + _dot(om.astype(BF16), w_ref[B_O:B_O + MEM_W, :])
    _post(x_ref, y, gp_ref, out_ref)


def _merge_b(x2d, outs, lses, rest, om, w_bf, g_post, unperms, tm=TM):
    m, d = x2d.shape
    tm = min(tm, m)
    row = lambda i: (i, 0)

    def spec(a):
        if a.ndim == 2:
            return pl.BlockSpec((tm, B_O), row)
        assert tm == CM_TILE
        return pl.BlockSpec((None,) + a.shape[1:], lambda i: (i, 0, 0, 0))

    return pl.pallas_call(
        functools.partial(_merge_b_kernel, n_u=len(unperms)),
        grid=(m // tm,),
        in_specs=[pl.BlockSpec((tm, d), row)] + [spec(a) for a in outs] + [spec(a) for a in lses] +
                 [pl.BlockSpec((tm, B_O), row),
                  pl.BlockSpec((tm, MEM_W), lambda i: (i, 3)),
                  pl.BlockSpec((tm, MEM_W), row),
                  pl.BlockSpec((B_O + MEM_W, d), lambda i: (0, 0)),
                  pl.BlockSpec((1, d), lambda i: (0, 0))] +
                 [pl.BlockSpec(u.shape, lambda i: (0, 0)) for u in unperms],
        out_specs=pl.BlockSpec((tm, d), row),
        out_shape=jax.ShapeDtypeStruct((m, d), F32),
        compiler_params=_cparams(("arbitrary",)),
        name="merge_b",
    )(x2d, *outs, *lses, rest, rest, om, w_bf, g_post.reshape(1, d), *unperms)


def _dil_kernel(q_ref, kc_ref, kp_ref, o_ref, lse_ref, *, tn, wr):
    i = pl.program_id(2)
    nk = wr + tn

    def rows(ref):
        v = ref[...]
        return v if v.ndim == 2 else v.reshape(v.shape[0] * v.shape[1], v.shape[2])

    q, kc, kp = rows(q_ref), rows(kc_ref), rows(kp_ref)
    t = lax.broadcasted_iota(jnp.int32, (tn, 1), 0)
    k = lax.broadcasted_iota(jnp.int32, (1, nk), 1)
    dist = t - k + wr
    valid = (dist >= 0) & (dist <= wr) & ((k >= wr) | (i > 0))
    distf = dist.astype(F32)
    lane = lax.broadcasted_iota(jnp.int32, (tn, LANES), 1)
    o_cols, l_cols = [], []
    for h in range(B_KV):
        p = h // 2
        kk = jnp.concatenate([kp[:, p * LANES:(p + 1) * LANES], kc[:, p * LANES:(p + 1) * LANES]], axis=0)
        vv = jnp.concatenate([kp[:, 256 + p * LANES:256 + (p + 1) * LANES],
                              kc[:, 256 + p * LANES:256 + (p + 1) * LANES]], axis=0)
        qh = jnp.concatenate([q[:, (h * B_REP + j) * LANES:(h * B_REP + j + 1) * LANES]
                              for j in range(B_REP)], axis=0)
        s = _dot_nt(qh, kk)
        parts = []
        for j in range(B_REP):
            sj = s[j * tn:(j + 1) * tn, :] - SLOPES_B[h * B_REP + j] * distf
            parts.append(jnp.where(valid, sj, NEG))
        sm = jnp.concatenate(parts, axis=0)
        m = jnp.max(sm, axis=-1, keepdims=True)
        e = jnp.exp(sm - m)
        den = jnp.sum(e, axis=-1, keepdims=True)
        og = _dot(e.astype(BF16), vv) / den
        lse = m + jnp.log(den)
        halves = []
        for j in range(B_REP):
            piece = og[j * tn:(j + 1) * tn, :]
            if (h % 2) != j:
                piece = pltpu.roll(piece, HEAD_DIM, 1)
            halves.append(piece)
        o_cols.append(_pick_half(lane, halves[0], halves[1]))
        l_cols.append(_pick_half(lane, jnp.broadcast_to(lse[0:tn], (tn, LANES)),
                                 jnp.broadcast_to(lse[tn:2 * tn], (tn, LANES))))
    o = jnp.concatenate(o_cols, axis=1)
    l = jnp.concatenate(l_cols, axis=1)
    o_ref[...] = o.reshape(o_ref.shape)
    lse_ref[...] = l.reshape(lse_ref.shape)


def _dilated_prompt(q, kv, gi, n_b, s_len):
    win, dil = B_GROUPS[gi]
    wr = win // dil
    n = s_len // dil
    tn = min(wr, n)
    assert n % tn == 0 and tn % wr == 0
    ratio = tn // wr
    qw = B_HPG * LANES
    if dil == 1:
        in_specs = [pl.BlockSpec((None, tn, qw), lambda b, c, i: (b, i, 0)),
                    pl.BlockSpec((None, tn, 512), lambda b, c, i: (b, i, 0)),
                    pl.BlockSpec((None, wr, 512), lambda b, c, i: (b, jnp.maximum(i * ratio - 1, 0), 0))]
        out_spec = pl.BlockSpec((None, tn, B_O), lambda b, c, i: (b, i, 0))
        out_shape = jax.ShapeDtypeStruct((n_b, s_len, B_O), F32)
    else:
        rpt = CM_TILE // dil
        nt, npv = tn // rpt, wr // rpt
        in_specs = [pl.BlockSpec((None, nt, None, rpt, qw), lambda b, c, i: (b, i, c, 0, 0)),
                    pl.BlockSpec((None, nt, None, rpt, 512), lambda b, c, i: (b, i, c, 0, 0)),
                    pl.BlockSpec((None, npv, None, rpt, 512),
                                 lambda b, c, i: (b, jnp.maximum(i * ratio - 1, 0), c, 0, 0))]
        out_spec = pl.BlockSpec((None, nt, None, rpt, B_O), lambda b, c, i: (b, i, c, 0, 0))
        out_shape = jax.ShapeDtypeStruct((n_b, s_len // CM_TILE, dil, rpt, B_O), F32)
    return pl.pallas_call(
        functools.partial(_dil_kernel, tn=tn, wr=wr),
        grid=(n_b, dil, n // tn),
        in_specs=in_specs,
        out_specs=[out_spec, out_spec],
        out_shape=[out_shape, out_shape],
        compiler_params=_cparams(("arbitrary", "arbitrary", "arbitrary")),
        name="dilated_prompt",
    )(q, kv, kv)


def _dil_sample_kernel(q_ref, kn_ref, cache_ref, o_ref, lse_ref, *, t_new, wb):
    rows = B_KV * B_REP * t_new
    rid = lax.broadcasted_iota(jnp.int32, (rows, 1), 0)
    pos = wb + (rid & (t_new - 1))
    lane = lax.broadcasted_iota(jnp.int32, (t_new, LANES), 1)
    zero = jnp.zeros((t_new, LANES), BF16)
    new = jnp.concatenate([kn_ref[...], jnp.zeros((LANES - t_new, 512), F32)], axis=0).astype(BF16)
    nidx = wb + lax.broadcasted_iota(jnp.int32, (1, LANES), 1)
    dn = pos - nidx
    for gi, (win, dil) in enumerate(B_GROUPS):
        lo = wb - min(win, wb)
        nk = wb - lo
        blocks = []
        for h in range(B_KV):
            for j in range(B_REP):
                c = gi * B_HPG + h * B_REP + j
                slot = q_ref[:, c * LANES:(c + 1) * LANES]
                blocks.append(jnp.concatenate([slot, zero] if h < 2 else [zero, slot], axis=1))
        qbd = jnp.concatenate(blocks, axis=0)
        kt = cache_ref[0:256, lo:wb].astype(BF16)
        vt = cache_ref[256:512, lo:wb].astype(BF16)
        s_c = _dot(qbd, kt)
        s_n = _dot_nt(qbd, new[:, 0:256])
        kidx = lo + lax.broadcasted_iota(jnp.int32, (1, nk), 1)
        dc = pos - kidx
        vc = (dc <= win) & ((dc & (dil - 1)) == 0)
        vn = (dn >= 0) & ((dn & (dil - 1)) == 0)
        dcf, dnf = dc.astype(F32), dn.astype(F32)
        pc, pn = [], []
        for hj in range(B_HPG):
            r = slice(hj * t_new, (hj + 1) * t_new)
            sl = SLOPES_B[hj] / dil
            pc.append(jnp.where(vc[r], s_c[r] - sl * dcf[r], NEG))
            pn.append(jnp.where(vn[r], s_n[r] - sl * dnf[r], NEG))
        sc, sn = jnp.concatenate(pc, axis=0), jnp.concatenate(pn, axis=0)
        m = jnp.maximum(jnp.max(sc, axis=-1, keepdims=True), jnp.max(sn, axis=-1, keepdims=True))
        ec, en = jnp.exp(sc - m), jnp.exp(sn - m)
        den = jnp.sum(ec, axis=-1, keepdims=True) + jnp.sum(en, axis=-1, keepdims=True)
        ow = (_dot_nt(ec.astype(BF16), vt) + _dot(en.astype(BF16), new[:, 256:512])) / den
        lse = m + jnp.log(den)
        for h in range(B_KV):
            halves, lhalves = [], []
            for j in range(B_REP):
                r0 = (h * B_REP + j) * t_new
                piece = ow[r0:r0 + t_new, (h // 2) * LANES:(h // 2 + 1) * LANES]
                if (h % 2) != j:
                    piece = pltpu.roll(piece, HEAD_DIM, 1)
                halves.append(piece)
                lhalves.append(jnp.broadcast_to(lse[r0:r0 + t_new], (t_new, LANES)))
            c0 = gi * B_O + h * LANES
            o_ref[:, c0:c0 + LANES] = _pick_half(lane, halves[0], halves[1])
            lse_ref[:, c0:c0 + LANES] = _pick_half(lane, lhalves[0], lhalves[1])


def _dilated_sample(q_exp, kv_new, cache_t, n, t_new):
    wb = cache_t.shape[2]
    assert t_new & (t_new - 1) == 0 and t_new <= LANES and all(w <= wb for w, _ in B_GROUPS)
    o, lse = pl.pallas_call(
        functools.partial(_dil_sample_kernel, t_new=t_new, wb=wb),
        grid=(n,),
        in_specs=[pl.BlockSpec((t_new, N_B_GROUPS * B_HPG * LANES), lambda s: (s, 0)),
                  pl.BlockSpec((t_new, 512), lambda s: (s, 0)),
                  pl.BlockSpec((None, 512, wb), lambda s: (s, 0, 0))],
        out_specs=[pl.BlockSpec((t_new, N_B_GROUPS * B_O), lambda s: (s, 0)),
                   pl.BlockSpec((t_new, N_B_GROUPS * B_O), lambda s: (s, 0))],
        out_shape=[jax.ShapeDtypeStruct((n * t_new, N_B_GROUPS * B_O), F32),
                   jax.ShapeDtypeStruct((n * t_new, N_B_GROUPS * B_O), F32)],
        compiler_params=_cparams(("arbitrary",)),
        name="dilated_sample",
    )(q_exp, kv_new, cache_t)
    return ([o[:, g * B_O:(g + 1) * B_O] for g in range(N_B_GROUPS)],
            [lse[:, g * B_O:(g + 1) * B_O] for g in range(N_B_GROUPS)])


def _nsa_sample_kernel(pt_ref, q_ref, *refs, n_pages, t_new, past, wbuf, ns, nslot):
    ck_refs = refs[0:n_pages]
    sp_refs = refs[n_pages:2 * n_pages]
    kvn_ref, win_ref, es_ref = refs[2 * n_pages:2 * n_pages + 3]
    oc_ref, os_ref, ow_ref = refs[2 * n_pages + 3:2 * n_pages + 6]
    ck_s, kall = refs[2 * n_pages + 6:]
    del pt_ref
    page = sp_refs[0].shape[1]
    hs = nslot // 2
    per_page = page // CMP_LEN
    ck_s[...] = jnp.zeros(ck_s.shape, F32)
    for p in range(n_pages):
        for n in range(per_page):
            c = p * per_page + n
            slot = (c % 2) * hs + c // 2
            ck_s[slot:slot + 1, :] = ck_refs[p][n:n + 1, :]
    for p in range(n_pages):
        kall[:, p * page:(p + 1) * page] = sp_refs[p][...].astype(BF16)
    new = jnp.concatenate([kvn_ref[...], jnp.zeros((LANES - t_new, 3 * A_KVW), F32)], axis=0).astype(BF16)

    rows = A_HEADS * t_new
    zero = jnp.zeros((t_new, LANES), BF16)
    blocks = []
    for g in range(A_KV):
        for r in range(A_REP):
            h = g * A_REP + r
            slot = q_ref[:, h * LANES:(h + 1) * LANES]
            blocks.append(jnp.concatenate([slot, zero] if g < 2 else [zero, slot], axis=1))
    qbd = jnp.concatenate(blocks, axis=0)
    rid = lax.broadcasted_iota(jnp.int32, (rows, 1), 0)
    pos = past + (rid & (t_new - 1))
    posf = pos.astype(F32)
    npos = past + lax.broadcasted_iota(jnp.int32, (1, LANES), 1)
    nd = pos - npos
    ndf = nd.astype(F32)

    def hs_(h):
        return slice(h * t_new, (h + 1) * t_new)

    def head_rows(fn):
        return jnp.concatenate([fn(h) for h in range(A_HEADS)], axis=0)

    def finish(ow):
        pieces = []
        for g in range(A_KV):
            for r in range(A_REP):
                r0 = (g * A_REP + r) * t_new
                pieces.append((ow[r0:r0 + t_new, (g // 2) * LANES:(g // 2 + 1) * LANES], g % 2))
        return _assemble(pieces, t_new)

    n_c = n_pages * per_page
    slot_i = lax.broadcasted_iota(jnp.int32, (1, nslot), 1)
    sl_lo = slot_i & (hs - 1)
    cidx = 2 * sl_lo + jnp.where(slot_i >= hs, 1, 0)
    svalid = (sl_lo < n_c // 2) & (cidx * CMP_LEN + (CMP_LEN - 1) <= pos)
    dist = posf - (cidx.astype(F32) * CMP_LEN + 0.5 * (CMP_LEN - 1))
    s = _dot_nt(qbd, ck_s[:, 0:256].astype(BF16))
    sm = head_rows(lambda h: jnp.where(svalid[hs_(h)], s[hs_(h)] - SLOPES_A[h] * dist[hs_(h)], -jnp.inf))
    m = jnp.max(sm, axis=-1, keepdims=True)
    m = jnp.where(m == -jnp.inf, 0.0, m)
    e = jnp.exp(sm - m)
    den = jnp.sum(e, axis=-1, keepdims=True)
    pc = e / jnp.where(den > 0, den, 1.0)
    oc_ref[...] = finish(_dot(pc.astype(BF16), ck_s[:, 256:512].astype(BF16)))
    g_rows = A_KV * t_new
    ps = jnp.concatenate(
        [sum(pc[hs_(g * A_REP + r)] for r in range(A_REP)) for g in range(A_KV)], axis=0)
    imp = ps + pltpu.roll(ps, hs, 1)
    blk = lax.broadcasted_iota(jnp.int32, (1, nslot), 1)
    gid = lax.broadcasted_iota(jnp.int32, (g_rows, 1), 0)
    cur = jnp.right_shift(past + (gid & (t_new - 1)), 6)
    imp = jnp.where(blk < n_c // 2, imp, 0.0)
    imp = jnp.where((blk == 0) | (blk == cur) | (blk == cur - 1), A_REP + 1.0, imp)
    imp = jnp.where(blk <= cur, imp, -1.0)
    imp = jnp.where(blk < ns, imp, -3.0)
    cnt = jnp.zeros((g_rows, nslot), F32)
    for j in range(ns):
        col = imp[:, j:j + 1]
        ahead = (col > imp) | ((col == imp) & (blk > j))
        cnt = cnt + jnp.where(ahead, 1.0, 0.0)
    sel = jnp.where((cnt < min(N_SEL, ns)) & (blk < ns), 1.0, 0.0).astype(BF16)
    selk = _dot(sel, es_ref[...])
    kpos = lax.broadcasted_iota(jnp.int32, (1, past), 1)
    kposf = kpos.astype(F32)
    s_c = _dot(qbd, kall[0:256, :])
    s_n = _dot_nt(qbd, new[:, 512:768])

    def sel_c(h):
        g = h // A_REP
        return jnp.where(selk[g * t_new:(g + 1) * t_new, 0:past] > 0.5,
                         s_c[hs_(h)] - SLOPES_A[h] * (posf[hs_(h)] - kposf), NEG)

    def sel_n(h):
        g = h // A_REP
        ok = (selk[g * t_new:(g + 1) * t_new, past:past + LANES] > 0.5) & (nd[hs_(h)] >= 0)
        return jnp.where(ok, s_n[hs_(h)] - SLOPES_A[h] * ndf[hs_(h)], NEG)

    sc, sn = head_rows(sel_c), head_rows(sel_n)
    m = jnp.maximum(jnp.max(sc, axis=-1, keepdims=True), jnp.max(sn, axis=-1, keepdims=True))
    ec, en = jnp.exp(sc - m), jnp.exp(sn - m)
    den = jnp.sum(ec, axis=-1, keepdims=True) + jnp.sum(en, axis=-1, keepdims=True)
    os_ref[...] = finish((_dot_nt(ec.astype(BF16), kall[256:512, :]) + _dot(en.astype(BF16), new[:, 768:1024])) / den)
    wpos = (past - wbuf) + lax.broadcasted_iota(jnp.int32, (1, wbuf), 1)
    wd = pos - wpos
    wvalid = (wd <= WIN_A) & (wpos >= 0)
    wdf = wd.astype(F32)
    s_c = _dot(qbd, win_ref[0:256, :].astype(BF16))
    s_n = _dot_nt(qbd, new[:, 1024:1280])
    sc = head_rows(lambda h: jnp.where(wvalid[hs_(h)], s_c[hs_(h)] - SLOPES_A[h] * wdf[hs_(h)], NEG))
    sn = head_rows(lambda h: jnp.where(nd[hs_(h)] >= 0, s_n[hs_(h)] - SLOPES_A[h] * ndf[hs_(h)], NEG))
    m = jnp.maximum(jnp.max(sc, axis=-1, keepdims=True), jnp.max(sn, axis=-1, keepdims=True))
    ec, en = jnp.exp(sc - m), jnp.exp(sn - m)
    den = jnp.sum(ec, axis=-1, keepdims=True) + jnp.sum(en, axis=-1, keepdims=True)
    ow_ref[...] = finish((_dot_nt(ec.astype(BF16), win_ref[256:512, :].astype(BF16)) +
                          _dot(en.astype(BF16), new[:, 1280:1536])) / den)


def _nsa_sample(q_exp, ckv_pool, slc_t, kv_new, win_t, page_table, layer, n, t_new):
    n_pages = page_table.shape[1]
    page = slc_t.shape[3]
    past = n_pages * page
    wbuf = win_t.shape[3]
    per_page = page // CMP_LEN
    tk = past + t_new
    ns = -(-tk // SLC_LEN)
    assert past % CMP_LEN == 0 and t_new < CMP_LEN and t_new & (t_new - 1) == 0 and past >= wbuf
    assert wbuf >= WIN_A and t_new <= SLC_LEN
    hs = HEAD_DIM
    while hs < max(n_pages * per_page // 2, ns):
        hs *= 2
    nslot = 2 * hs
    es = (np.arange(nslot)[:, None] == (np.arange(past + LANES)[None, :] // SLC_LEN)).astype(np.float32)
    es[:, past + t_new:] = 0.0
    es = jnp.asarray(es, BF16)
    ck_specs = [pl.BlockSpec((None, per_page, 512), functools.partial(lambda s, pt, p: (pt[s, p], 0, 0), p=p))
                for p in range(n_pages)]
    sp_specs = [pl.BlockSpec((None, None, 512, page), functools.partial(lambda s, pt, p: (pt[s, p], layer, 0, 0), p=p))
                for p in range(n_pages)]
    grid_spec = pltpu.PrefetchScalarGridSpec(
        num_scalar_prefetch=1,
        grid=(n,),
        in_specs=[pl.BlockSpec((t_new, A_HEADS * LANES), lambda s, pt: (s, 0))] + ck_specs + sp_specs +
                 [pl.BlockSpec((t_new, 3 * A_KVW), lambda s, pt: (s, 0)),
                  pl.BlockSpec((None, None, 512, wbuf), lambda s, pt: (s, layer, 0, 0)),
                  pl.BlockSpec(es.shape, lambda s, pt: (0, 0))],
        out_specs=[pl.BlockSpec((t_new, A_Q), lambda s, pt: (s, 0))] * 3,
        scratch_shapes=[pltpu.VMEM((nslot, 512), F32),
                        pltpu.VMEM((512, past), BF16)],
    )
    return pl.pallas_call(
        functools.partial(_nsa_sample_kernel, n_pages=n_pages, t_new=t_new, past=past, wbuf=wbuf,
                          ns=ns, nslot=nslot),
        grid_spec=grid_spec,
        out_shape=[jax.ShapeDtypeStruct((n * t_new, A_Q), F32)] * 3,
        compiler_params=_cparams(("arbitrary",)),
        name="nsa_sample",
    )(page_table, q_exp, *([ckv_pool] * n_pages), *([slc_t] * n_pages), kv_new, win_t, es)


def _take_cols(w, src):
    src = np.asarray(src)
    cols = jnp.take(w, jnp.asarray(np.maximum(src, 0)), axis=1)
    return jnp.where(jnp.asarray(src >= 0)[None, :], cols, 0.0).astype(BF16)


A_Q0, A_KC0, A_KS0, A_KW0 = 0, A_Q, A_Q + A_KVW, A_Q + 2 * A_KVW
A_GATE0 = A_Q + 3 * A_KVW
A_Z0 = A_GATE0 + 3 * A_HEADS
A_QM0 = A_Z0 + 3 * A_Q
A_ZM0 = A_QM0 + MEM_W
A_QW = A_HEADS * LANES
A_RESTW = 3 * A_Q + 2 * MEM_W + LANES


def _a_q_cols():
    src = []
    for h in range(A_HEADS):
        g = h // A_REP
        slot = [-1] * LANES
        for d in range(HEAD_DIM):
            slot[(g % 2) * HEAD_DIM + d] = A_Q0 + h * HEAD_DIM + d
        src += slot
    return src


def _a_rest_cols():
    return (list(range(A_Z0, A_Z0 + 3 * A_Q)) + list(range(A_QM0, A_QM0 + MEM_W)) +
            list(range(A_ZM0, A_ZM0 + MEM_W)) + list(range(A_GATE0, A_GATE0 + 3 * A_HEADS)) +
            [-1] * (LANES - 3 * A_HEADS))


def _b_q_cols():
    src = []
    for gi in range(N_B_GROUPS):
        for h in range(B_KV):
            for j in range(B_REP):
                slot = [-1] * LANES
                for d in range(HEAD_DIM):
                    slot[(h % 2) * HEAD_DIM + d] = gi * B_O + (h * B_REP + j) * HEAD_DIM + d
                src += slot
    return src


def _blockdiag2(w):
    z = jnp.zeros_like(w)
    top = jnp.concatenate([w, z], axis=-1)
    bot = jnp.concatenate([z, w], axis=-1)
    return jnp.concatenate([top, bot], axis=-2)


def _class_perm(dil):
    p = np.zeros((CM_TILE, CM_TILE), np.float32)
    s = np.arange(CM_TILE)
    p[(s % dil) * (CM_TILE // dil) + s // dil, s] = 1.0
    return p


def kernel(x_prompt, x_sample, mem_prompt, cache_cmp_kv, cache_slc_kv, cache_win_kv, cache_dil_kv, cache_mem_kv,
           page_table, g_pre, g_post, g_mem, w_mem_kv, w_in_a, w_out_a, cmp_pos, cmp_w1, cmp_w2,
           g_kv_b, w_kv_b, w_in_b, w_out_b):
    n_b, s_len, d = x_prompt.shape
    n_s, t_s, _ = x_sample.shape
    depth = g_pre.shape[0]
    n_a = w_in_a.shape[0]
    n_pool, page = cache_cmp_kv.shape[:2]
    mem_len = mem_prompt.shape[1]
    wbuf_a = cache_win_kv.shape[1]
    wbuf_b = cache_dil_kv.shape[1]
    ns_p = s_len // SLC_LEN
    m_p = n_b * s_len
    assert s_len % TM == 0 and mem_len % TM == 0 or mem_len == TM
    spb = s_len // TM
    nkt = s_len // TK

    xp = x_prompt.reshape(m_p, d)
    xs = x_sample.reshape(n_s * t_s, d)
    mem2d = mem_prompt.reshape(n_b * mem_len, d)

    egate = np.zeros((LANES, 3 * A_Q), np.float32)
    for b in range(3):
        for h in range(A_HEADS):
            egate[b * A_HEADS + h, b * A_Q + h * HEAD_DIM:b * A_Q + (h + 1) * HEAD_DIM] = 1.0
    egate = jnp.asarray(egate, BF16)
    etile = (np.arange(ns_p)[None, :, None] ==
             (np.arange(nkt)[:, None, None] * TK + np.arange(TK)[None, None, :]) // SLC_LEN)
    etile = jnp.asarray(etile.astype(np.float32), BF16)
    anymat = np.zeros((ns_p, LANES), np.float32)
    anymat[np.arange(ns_p), np.arange(ns_p) // (TK // SLC_LEN)] = 1.0
    anymat = jnp.asarray(anymat, BF16)
    perms = [jnp.asarray(_class_perm(dil), BF16) for _, dil in B_GROUPS[1:]]
    unperms = [jnp.asarray(_class_perm(dil).T, BF16) for _, dil in B_GROUPS[1:]]

    cmp_t = jnp.transpose(cache_cmp_kv, (0, 2, 3, 4, 5, 1)).reshape(n_pool, n_a, A_KVW, page)
    slc_t = jnp.transpose(cache_slc_kv, (0, 2, 3, 4, 5, 1)).reshape(n_pool, n_a, A_KVW, page)
    win_t = jnp.transpose(cache_win_kv, (0, 2, 3, 4, 5, 1)).reshape(n_s, n_a, A_KVW, wbuf_a)
    dil_t = jnp.transpose(cache_dil_kv, (0, 2, 3, 4, 1)).reshape(n_s, 2 * B_KV * HEAD_DIM, wbuf_b)
    mem_t = jnp.transpose(cache_mem_kv, (0, 2, 3, 4, 5, 1)).reshape(n_s, depth, 2 * MEM_W, mem_len)

    a_q_cols, a_rest_cols, b_q_cols = _a_q_cols(), _a_rest_cols(), _b_q_cols()
    row_spec = lambda w: pl.BlockSpec((TM, w), lambda i: (i, 0))

    kct_l, kst_l, kwt_l, kv_s, mkvt_l = [], [], [], [], []
    kvbt_p = kvb_s = None
    kvb_cm = None
    for l in range(depth):
        mkvt = _proj(mem2d, g_mem[l], [w_mem_kv[l].T.astype(BF16)], [],
                     [(True, 0, a, b, 0, (_st_t(0, a),)) for a, b in _chunks(0, 2 * MEM_W)],
                     [jax.ShapeDtypeStruct((n_b, 2 * MEM_W, mem_len), F32)],
                     [_t_spec(2 * MEM_W, mem_len // min(TM, mem_len), min(TM, mem_len))])[0]
        mkvt_l.append(mkvt)
        mkvt4 = mkvt.reshape(n_b, 1, 2 * MEM_W, mem_len)
        if l < n_a:
            w = w_in_a[l]
            w_n = jnp.concatenate([w[:, A_KC0:A_KC0 + A_KVW].astype(BF16), _take_cols(w, a_q_cols),
                                   _take_cols(w, a_rest_cols)], axis=1)
            w_t = w[:, A_KC0:A_KC0 + 3 * A_KVW].T.astype(BF16)
            w_out = w_out_a[l].astype(BF16)
            pos2 = jnp.concatenate([cmp_pos[l], cmp_pos[l]], axis=-1).transpose(1, 0, 2)
            w1bd = _blockdiag2(cmp_w1[l]).transpose(1, 0, 2, 3).astype(BF16)
            w2bd = _blockdiag2(cmp_w2[l]).astype(BF16)
            w2bdt = jnp.swapaxes(w2bd, 1, 2)
            plan = []
            for j in range(3):
                stores = [_st_t(j, 0)]
                if j > 0:
                    stores.append(_st_ttile(3, (j - 1) * A_KVW))
                plan.append((True, 1, j * A_KVW, (j + 1) * A_KVW, 0, tuple(stores)))
            for j in range(4):
                plan.append((False, 0, j * LANES, (j + 1) * LANES, 0, (_st_rows(4 + j, 0),)))
            for a, b in _chunks(A_KVW, A_KVW + A_QW):
                plan.append((False, 0, a, b, 0, (_st_rows(8, a - A_KVW, QSCALE),)))
            for a, b in _chunks(A_KVW + A_QW, A_KVW + A_QW + A_RESTW):
                plan.append((False, 0, a, b, 0, (_st_rows(9, a - A_KVW - A_QW),)))
            shapes = ([jax.ShapeDtypeStruct((n_b, A_KVW, s_len), F32)] * 3 +
                      [jax.ShapeDtypeStruct((n_b, s_len // LANES, 2 * A_KVW, LANES), BF16)] +
                      [jax.ShapeDtypeStruct((m_p, LANES), F32)] * 4 +
                      [jax.ShapeDtypeStruct((m_p, A_QW), BF16), jax.ShapeDtypeStruct((m_p, A_RESTW), F32)])
            specs = ([_t_spec(A_KVW, spb, TM)] * 3 +
                     [pl.BlockSpec((None, TM // LANES, 2 * A_KVW, LANES), lambda i: (i // spb, i % spb, 0, 0))] +
                     [row_spec(LANES)] * 4 + [row_spec(A_QW), row_spec(A_RESTW)])
            outs = _proj(xp, g_pre[l], [w_n, w_t], [], plan, shapes, specs)
            kct, kst, kwt, kvt = outs[0:4]
            slabs, q, rest = outs[4:8], outs[8], outs[9]
            kct_l.append(kct)
            kst_l.append(kst)
            kwt_l.append(kwt)
            ckt, cv = _compress_prompt(slabs, pos2, w1bd, w2bd, w2bdt, n_b, s_len)
            q3 = q.reshape(n_b, s_len, A_QW)
            o_cmp, sel, flags = _nsa_cmp(q3, ckt, cv, anymat, n_b, s_len, TQ)
            flags = (flags[:, :, 0:A_KV, 0:nkt] > 0.5).astype(jnp.int32).reshape(-1)
            o_slc, o_win = _nsa_sw(flags, q3, kvt, sel, etile, n_b, s_len, TQ, TK)
            om = _mem_attend(rest, 9, mkvt4, 0, n_b, s_len, min(512, s_len))
            xp = _merge_a(xp, o_cmp.reshape(-1, A_Q), o_slc.reshape(-1, A_Q), o_win.reshape(-1, A_Q),
                          rest, om, w_out, g_post[l], egate)
            w_rows = jnp.concatenate([w[:, A_KC0:A_KC0 + 3 * A_KVW].astype(BF16), w_n[:, A_KVW:]], axis=1)
            kv, q, rest = _proj_rows(xs, g_pre[l], w_rows,
                                     [(0, 3 * A_KVW, F32, 1.0), (3 * A_KVW, A_QW, BF16, QSCALE),
                                      (3 * A_KVW + A_QW, A_RESTW, F32, 1.0)])
            kv_s.append(kv)
            posr = jnp.tile(pos2.transpose(1, 0, 2), (1, page // CMP_LEN, 1))
            ckv_pool = _compress_pool(cmp_t, l, posr, w1bd, w2bd, 32)
            ckv_pool = ckv_pool.reshape(n_pool, page // CMP_LEN, 512)
            o_cmp, o_slc, o_win = _nsa_sample(q, ckv_pool, slc_t, kv, win_t, page_table, l, n_s, t_s)
            om = _mem_attend(rest, 9, mem_t, l, n_s, t_s, t_s)
            xs = _merge_a(xs, o_cmp, o_slc, o_win, rest, om, w_out, g_post[l], egate)
            if l == n_a - 1:
                w_kv = w_kv_b.astype(BF16)
                plan = [(True, 1, 0, 512, 0, (_st_t(0, 0),)),
                        (False, 0, 0, 512, 0, (_st_rows(1, 0),)),
                        (False, 0, 0, 512, 1, (_st_cm(2, 0, B_GROUPS[1][1]),)),
                        (False, 0, 0, 512, 2, (_st_cm(3, 0, B_GROUPS[2][1]),))]
                shapes = [jax.ShapeDtypeStruct((n_b, 512, s_len), F32), jax.ShapeDtypeStruct((m_p, 512), BF16)]
                specs = [_t_spec(512, spb, TM), row_spec(512)]
                for _, dil in B_GROUPS[1:]:
                    shapes.append(jax.ShapeDtypeStruct((m_p // CM_TILE, dil, CM_TILE // dil, 512), BF16))
                    specs.append(_cm_spec(dil, 512))
                kvbt_p, kv0, kv1, kv2 = _proj(xp, g_kv_b, [w_kv, w_kv_b.T.astype(BF16)], perms, plan, shapes, specs)
                kvb_cm = [kv0.reshape(n_b, s_len, 512)] + [
                    a.reshape((n_b, s_len // CM_TILE) + a.shape[1:]) for a in (kv1, kv2)]
                kvb_s = _proj_rows(xs, g_kv_b, w_kv, [(0, 512, F32, 1.0)])[0]
        else:
            lb = l - n_a
            w = w_in_b[lb]
            w_n = jnp.concatenate([_take_cols(w, b_q_cols), w[:, B_Q:].astype(BF16)], axis=1)
            w_out = w_out_b[lb].astype(BF16)
            qw = B_HPG * LANES
            restw = B_O + 2 * MEM_W
            plan, shapes, specs = [], [], []
            for gi, (_, dil) in enumerate(B_GROUPS):
                for a, b in _chunks(gi * qw, (gi + 1) * qw):
                    st = _st_rows(gi, a - gi * qw, QSCALE) if dil == 1 else _st_cm(gi, a - gi * qw, dil, QSCALE)
                    plan.append((False, 0, a, b, gi, (st,)))
                if dil == 1:
                    shapes.append(jax.ShapeDtypeStruct((m_p, qw), BF16))
                    specs.append(row_spec(qw))
                else:
                    shapes.append(jax.ShapeDtypeStruct((m_p // CM_TILE, dil, CM_TILE // dil, qw), BF16))
                    specs.append(_cm_spec(dil, qw))
            for a, b in _chunks(3 * qw, 3 * qw + restw):
                plan.append((False, 0, a, b, 0, (_st_rows(3, a - 3 * qw),)))
            shapes.append(jax.ShapeDtypeStruct((m_p, restw), F32))
            specs.append(row_spec(restw))
            q0, q1, q2, rest = _proj(xp, g_pre[l], [w_n], perms, plan, shapes, specs)
            qs = [q0.reshape(n_b, s_len, qw)] + [a.reshape((n_b, s_len // CM_TILE) + a.shape[1:]) for a in (q1, q2)]
            outs, lses = [], []
            for gi in range(N_B_GROUPS):
                o, lse = _dilated_prompt(qs[gi], kvb_cm[gi], gi, n_b, s_len)
                if gi == 0:
                    o, lse = o.reshape(m_p, B_O), lse.reshape(m_p, B_O)
                else:
                    o, lse = (a.reshape((m_p // CM_TILE,) + a.shape[2:]) for a in (o, lse))
                outs.append(o)
                lses.append(lse)
            om = _mem_attend(rest, 2, mkvt4, 0, n_b, s_len, min(512, s_len))
            xp = _merge_b(xp, outs, lses, rest, om, w_out, g_post[l], unperms)
            q, rest = _proj_rows(xs, g_pre[l], w_n, [(0, 3 * qw, BF16, QSCALE), (3 * qw, restw, F32, 1.0)])
            outs, lses = _dilated_sample(q, kvb_s, dil_t, n_s, t_s)
            om = _mem_attend(rest, 2, mem_t, l, n_s, t_s, t_s)
            xs = _merge_b(xs, outs, lses, rest, om, w_out, g_post[l], [])

    def from_t(arrs, n, t):
        a = jnp.stack(arrs, 1).reshape(n, len(arrs), 2, A_KV, HEAD_DIM, t)
        return jnp.transpose(a, (0, 5, 1, 2, 3, 4))

    def kv_stack(kvs, n, t, j):
        return jnp.stack([k[:, j * A_KVW:(j + 1) * A_KVW].reshape(n, t, 2, A_KV, HEAD_DIM) for k in kvs], 2)

    new_cmp_p = from_t(kct_l, n_b, s_len)
    new_slc_p = from_t(kst_l, n_b, s_len)
    wa = min(WIN_A, s_len)
    new_win_p = from_t([k[:, :, s_len - wa:] for k in kwt_l], n_b, wa)
    new_cmp_s = kv_stack(kv_s, n_s, t_s, 0)
    new_slc_s = kv_stack(kv_s, n_s, t_s, 1)
    win_s = kv_stack(kv_s, n_s, t_s, 2)
    new_win_s = jnp.concatenate([cache_win_kv, win_s], 1)[:, -wbuf_a:]
    wb = min(B_GROUPS[-1][0], s_len)
    new_dil_p = jnp.transpose(kvbt_p[:, :, s_len - wb:].reshape(n_b, 2, B_KV, HEAD_DIM, wb), (0, 4, 1, 2, 3))
    new_dil_s = jnp.concatenate([cache_dil_kv, kvb_s.reshape(n_s, t_s, 2, B_KV, HEAD_DIM)], 1)[:, -wbuf_b:]
    new_mem_p = jnp.transpose(jnp.stack(mkvt_l, 1).reshape(n_b, depth, 2, MEM_HEADS, HEAD_DIM, mem_len),
                              (0, 5, 1, 2, 3, 4))
    return (xp.reshape(n_b, s_len, d), xs.reshape(n_s, t_s, d), new_cmp_p, new_cmp_s, new_slc_p, new_slc_s,
            new_win_p, new_win_s, new_dil_p, new_dil_s, new_mem_p)
```

```python
import functools
import numpy as np
import jax
import jax.numpy as jnp
from jax import lax
from jax.experimental import pallas as pl
from jax.experimental.pallas import tpu as pltpu

F32 = jnp.float32
BF16 = jnp.bfloat16

HEAD_DIM = 64
LANES = 128
A_HEADS = 12
A_KV = 4
A_REP = A_HEADS // A_KV
CMP_LEN = 32
SLC_LEN = 64
N_SEL = 16
WIN_A = 512
B_GROUPS = ((128, 1), (512, 4), (2048, 16))
N_B_GROUPS = 3
B_KV = 4
B_REP = 2
B_HPG = B_KV * B_REP
MEM_HEADS = 4
EPS = 1e-6
A_Q = A_HEADS * HEAD_DIM
A_KVW = 2 * A_KV * HEAD_DIM
MEM_W = MEM_HEADS * HEAD_DIM
B_Q = N_B_GROUPS * B_HPG * HEAD_DIM
B_O = B_HPG * HEAD_DIM
QSCALE = HEAD_DIM ** -0.5
NEG = -1e30
VMEM_LIMIT = 56 * 1024 * 1024
TM = 256
TQ = 128
TK = 512
CM_TILE = 256


def _alibi(n):
    return [float(2.0 ** (-8.0 * i / n)) for i in range(1, n + 1)]


SLOPES_A = _alibi(A_HEADS)
SLOPES_B = _alibi(B_HPG)


def _cparams(sem):
    return pltpu.CompilerParams(dimension_semantics=sem, vmem_limit_bytes=VMEM_LIMIT)


def _silu(x):
    return x * (1.0 / (1.0 + jnp.exp(-x)))


def _dot_nt(a, b):
    return lax.dot_general(a, b, (((1,), (1,)), ((), ())), preferred_element_type=F32)


def _dot(a, b):
    return jnp.dot(a, b, preferred_element_type=F32)


def _split3(x):
    hi = x.astype(BF16)
    r1 = x - hi.astype(F32)
    mid = r1.astype(BF16)
    lo = (r1 - mid.astype(F32)).astype(BF16)
    return hi, mid, lo


def _pick_half(lane, a, b):
    return jnp.where(lane < HEAD_DIM, a, b)


def _assemble(pieces, rows):
    lane = lax.broadcasted_iota(jnp.int32, (rows, LANES), 1)
    cols = []
    for c in range(len(pieces) // 2):
        halves = []
        for k in (0, 1):
            arr, useful = pieces[2 * c + k]
            if useful != k:
                arr = pltpu.roll(arr, HEAD_DIM, 1)
            halves.append(arr)
        cols.append(_pick_half(lane, halves[0], halves[1]))
    return jnp.concatenate(cols, axis=1)


def _proj_kernel(x_ref, g_ref, *refs, n_w, n_p, plan):
    w_refs = refs[:n_w]
    p_refs = refs[n_w:n_w + n_p]
    out_refs = refs[n_w + n_p:]
    x = x_ref[...]
    ms = jnp.mean(x * x, axis=-1, keepdims=True)
    xn = ((x * lax.rsqrt(ms + EPS)) * g_ref[...]).astype(BF16)
    xs = [xn] + [_dot(p_ref[...], xn).astype(BF16) for p_ref in p_refs]
    for (transposed, wi, c0, c1, xi, stores) in plan:
        if transposed:
            y = _dot_nt(w_refs[wi][c0:c1, :], xs[xi])
        else:
            y = _dot(xs[xi], w_refs[wi][:, c0:c1])
        for store in stores:
            store(out_refs, y)


def _st_rows(oi, o0, scale=1.0):
    def store(out_refs, y):
        v = y if scale == 1.0 else y * scale
        out_refs[oi][:, o0:o0 + y.shape[1]] = v.astype(out_refs[oi].dtype)
    return store


def _st_t(oi, o0):
    def store(out_refs, y):
        out_refs[oi][o0:o0 + y.shape[0], :] = y.astype(out_refs[oi].dtype)
    return store


def _st_ttile(oi, o0):
    def store(out_refs, y):
        for u in range(y.shape[1] // LANES):
            out_refs[oi][u, o0:o0 + y.shape[0], :] = y[:, u * LANES:(u + 1) * LANES].astype(out_refs[oi].dtype)
    return store


def _st_cm(oi, o0, dil, scale=1.0):
    def store(out_refs, y):
        v = y if scale == 1.0 else y * scale
        v = v.astype(out_refs[oi].dtype)
        out_refs[oi][:, :, o0:o0 + y.shape[1]] = v.reshape(dil, y.shape[0] // dil, y.shape[1])
    return store


def _chunks(c0, c1, step=512):
    return [(a, min(a + step, c1)) for a in range(c0, c1, step)]


def _proj(x2d, g, weights, perms, plan, out_shapes, out_specs, tm=TM):
    m, d = x2d.shape
    tm = min(tm, m)
    const2 = lambda i: (0, 0)
    return pl.pallas_call(
        functools.partial(_proj_kernel, n_w=len(weights), n_p=len(perms), plan=tuple(plan)),
        grid=(m // tm,),
        in_specs=[pl.BlockSpec((tm, d), lambda i: (i, 0)), pl.BlockSpec((1, d), const2)] +
                 [pl.BlockSpec(w.shape, const2) for w in weights] +
                 [pl.BlockSpec(p.shape, const2) for p in perms],
        out_specs=out_specs,
        out_shape=out_shapes,
        compiler_params=_cparams(("arbitrary",)),
        name="proj",
    )(x2d, g.reshape(1, d), *weights, *perms)


def _proj_rows(x2d, g, w_bf, outs):
    m = x2d.shape[0]
    tm = min(TM, m)
    plan = []
    for oi, (c0, width, _, scale) in enumerate(outs):
        for a, b in _chunks(c0, c0 + width):
            plan.append((False, 0, a, b, 0, (_st_rows(oi, a - c0, scale),)))
    return _proj(x2d, g, [w_bf], [], plan,
                 [jax.ShapeDtypeStruct((m, o[1]), o[2]) for o in outs],
                 [pl.BlockSpec((tm, o[1]), lambda i: (i, 0)) for o in outs])


def _t_spec(width, spb, tm):
    return pl.BlockSpec((None, width, tm), lambda i: (i // spb, 0, i % spb))


def _cm_spec(dil, width):
    return pl.BlockSpec((None, dil, CM_TILE // dil, width), lambda i: (i, 0, 0, 0))


def _compress_pool_kernel(x_ref, posr_ref, w1_ref, w2_ref, o_ref, s0, s1, s2, s3, *, pages, pitch):
    scr = (s0, s1, s2, s3)
    page = x_ref.shape[2]
    per_page = page // CMP_LEN
    rows = per_page * pages

    def relayout(pg, carry):
        for idx in range(4):
            t = x_ref[pg, idx * LANES:(idx + 1) * LANES, :].T + posr_ref[idx // 2]
            for k in range(page // 8):
                n, l0 = (8 * k) // CMP_LEN, (8 * k) % CMP_LEN
                scr[idx][pl.ds(l0 * pitch + pg * per_page + n, 8, stride=pitch), :] = t[8 * k:8 * k + 8, :]
        return carry

    lax.fori_loop(0, pages, relayout, 0, unroll=2)
    for idx in range(4):
        kv = idx // 2
        acc = jnp.zeros((rows, LANES), F32)
        for l in range(CMP_LEN):
            xl = scr[idx][l * pitch:l * pitch + rows, :].astype(BF16)
            acc = acc + _dot(xl, w1_ref[l, kv])
        h = _silu(acc).astype(BF16)
        o_ref[:, idx * LANES:(idx + 1) * LANES] = _dot(h, w2_ref[kv])


def _compress_pool(cmp_t, layer, posr, w1bd, w2bd, pages):
    n_pool, _, _, page = cmp_t.shape
    per_page = page // CMP_LEN
    pages = max(p for p in range(1, min(pages, n_pool) + 1) if n_pool % p == 0 and (per_page * p) % 16 == 0)
    assert page % CMP_LEN == 0 and page == LANES
    rows = per_page * pages
    pitch = rows + 8
    assert rows % 16 == 0
    return pl.pallas_call(
        functools.partial(_compress_pool_kernel, pages=pages, pitch=pitch),
        grid=(n_pool // pages,),
        in_specs=[pl.BlockSpec((pages, None, 512, page), lambda i: (i, layer, 0, 0)),
                  pl.BlockSpec(posr.shape, lambda i: (0, 0, 0)),
                  pl.BlockSpec(w1bd.shape, lambda i: (0, 0, 0, 0)),
                  pl.BlockSpec(w2bd.shape, lambda i: (0, 0, 0))],
        out_specs=pl.BlockSpec((rows, 512), lambda i: (i, 0)),
        out_shape=jax.ShapeDtypeStruct((n_pool * per_page, 512), F32),
        scratch_shapes=[pltpu.VMEM((CMP_LEN * pitch, LANES), F32)] * 4,
        compiler_params=_cparams(("arbitrary",)),
        name="compress_pool",
    )(cmp_t, posr, w1bd, w2bd)


def _compress_prompt_kernel(s0, s1, s2, s3, pos_ref, w1_ref, w2_ref, w2t_ref, ckt_ref, cv_ref, *, nc):
    half = nc // 2
    slabs = (s0, s1, s2, s3)
    accs = [jnp.zeros((nc, LANES), F32) for _ in range(4)]
    for l in range(CMP_LEN):
        for idx in range(4):
            kv = idx // 2
            xe = slabs[idx][pl.ds(l, half, stride=2 * CMP_LEN), :]
            xo = slabs[idx][pl.ds(CMP_LEN + l, half, stride=2 * CMP_LEN), :]
            xb = (jnp.concatenate([xe, xo], axis=0) + pos_ref[l, kv:kv + 1, :]).astype(BF16)
            accs[idx] = accs[idx] + _dot(xb, w1_ref[l, kv])
    for idx in range(4):
        kv, p = idx // 2, idx % 2
        h = _silu(accs[idx]).astype(BF16)
        if kv == 0:
            ckt_ref[p * LANES:(p + 1) * LANES, :] = _dot_nt(w2t_ref[0], h)
        else:
            cv_ref[:, p * LANES:(p + 1) * LANES] = _dot(h, w2_ref[1])


def _compress_prompt(slabs, pos2, w1bd, w2bd, w2bdt, n_b, s_len):
    nc = s_len // CMP_LEN
    slab_spec = pl.BlockSpec((s_len, LANES), lambda b: (b, 0))
    return pl.pallas_call(
        functools.partial(_compress_prompt_kernel, nc=nc),
        grid=(n_b,),
        in_specs=[slab_spec] * 4 +
                 [pl.BlockSpec(pos2.shape, lambda b: (0, 0, 0)),
                  pl.BlockSpec(w1bd.shape, lambda b: (0, 0, 0, 0)),
                  pl.BlockSpec(w2bd.shape, lambda b: (0, 0, 0)),
                  pl.BlockSpec(w2bdt.shape, lambda b: (0, 0, 0))],
        out_specs=[pl.BlockSpec((None, 256, nc), lambda b: (b, 0, 0)),
                   pl.BlockSpec((None, nc, 256), lambda b: (b, 0, 0))],
        out_shape=[jax.ShapeDtypeStruct((n_b, 256, nc), F32),
                   jax.ShapeDtypeStruct((n_b, nc, 256), F32)],
        compiler_params=_cparams(("arbitrary",)),
        name="compress_prompt",
    )(*slabs, pos2, w1bd, w2bd, w2bdt)


def _topk_mask_t(imp_t, k):
    nb, cols = imp_t.shape
    row = lax.broadcasted_iota(jnp.int32, (nb, cols), 0)
    work = imp_t
    sel = jnp.zeros((nb, cols), F32)
    for _ in range(k):
        m = jnp.max(work, axis=0, keepdims=True)
        idx = jnp.min(jnp.where(work == m, row, nb), axis=0, keepdims=True)
        hit = row == idx
        sel = jnp.where(hit, 1.0, sel)
        work = jnp.where(hit, -3.0, work)
    return sel


def _group_q(q_ref, g):
    return jnp.concatenate(
        [q_ref[:, (g * A_REP + r) * LANES:(g * A_REP + r + 1) * LANES] for r in range(A_REP)], axis=0)


def _nsa_cmp_kernel(q_ref, ckt_ref, cv_ref, any_ref, o_ref, sel_ref, flag_ref, *, tq, nc, ns):
    i = pl.program_id(1)
    q0 = i * tq
    half = nc // 2
    pos = q0 + lax.broadcasted_iota(jnp.int32, (tq, 1), 0)
    posf = pos.astype(F32)
    tok = lax.broadcasted_iota(jnp.int32, (1, nc), 1)
    cidx = jnp.where(tok < half, 2 * tok, 2 * (tok - half) + 1)
    c_end = cidx * CMP_LEN + (CMP_LEN - 1)
    c_mid = cidx.astype(F32) * CMP_LEN + 0.5 * (CMP_LEN - 1)
    cmask = c_end <= pos
    dist = posf - c_mid
    blk = lax.broadcasted_iota(jnp.int32, (1, ns), 1)
    cur = jnp.right_shift(pos, 6)
    forced = (blk == 0) | (blk == cur) | (blk == cur - 1)
    allowed = blk <= cur
    pieces = []
    anys = []
    for p in range(A_KV // 2):
        ckt = ckt_ref[p * LANES:(p + 1) * LANES, :].astype(BF16)
        cv = cv_ref[:, p * LANES:(p + 1) * LANES].astype(BF16)
        qq = jnp.concatenate([_group_q(q_ref, 2 * p), _group_q(q_ref, 2 * p + 1)], axis=0)
        s = _dot(qq, ckt)
        parts = []
        for hh in range(2 * A_REP):
            sr = s[hh * tq:(hh + 1) * tq, :] - SLOPES_A[2 * p * A_REP + hh] * dist
            parts.append(jnp.where(cmask, sr, -jnp.inf))
        sm = jnp.concatenate(parts, axis=0)
        m = jnp.max(sm, axis=-1, keepdims=True)
        m = jnp.where(m == -jnp.inf, 0.0, m)
        e = jnp.exp(sm - m)
        den = jnp.sum(e, axis=-1, keepdims=True)
        pc = e / jnp.where(den > 0, den, 1.0)
        og = _dot(pc.astype(BF16), cv)
        imps = []
        for k in range(2):
            for r in range(A_REP):
                pieces.append((og[(k * A_REP + r) * tq:(k * A_REP + r + 1) * tq, :], k))
            ps = sum(pc[(k * A_REP + r) * tq:(k * A_REP + r + 1) * tq] for r in range(A_REP))
            imp = ps[:, 0:half] + ps[:, half:nc]
            imp = jnp.where(allowed, imp, -1.0)
            imps.append(jnp.where(forced, -3.0, imp).T)
        sel_t = _topk_mask_t(jnp.concatenate(imps, axis=1), max(min(N_SEL, ns) - 3, 0))
        for k in range(2):
            g = 2 * p + k
            sel = jnp.where(forced, 1.0, sel_t[:, k * tq:(k + 1) * tq].T)
            sel_ref[:, g * ns:(g + 1) * ns] = sel.astype(BF16)
            anys.append(jnp.max(sel, axis=0, keepdims=True))
    o_ref[...] = _assemble(pieces, tq)
    rows = jnp.concatenate(anys + [jnp.zeros((8 - A_KV, ns), F32)], axis=0).astype(BF16)
    flag_ref[...] = _dot(rows, any_ref[...])


def _nsa_cmp(q_exp, ckt, cv, anymat, n_b, s_len, tq):
    nc = ckt.shape[2]
    ns = s_len // SLC_LEN
    assert nc == 2 * ns
    return pl.pallas_call(
        functools.partial(_nsa_cmp_kernel, tq=tq, nc=nc, ns=ns),
        grid=(n_b, s_len // tq),
        in_specs=[pl.BlockSpec((None, tq, A_HEADS * LANES), lambda b, i: (b, i, 0)),
                  pl.BlockSpec((None, 256, nc), lambda b, i: (b, 0, 0)),
                  pl.BlockSpec((None, nc, 256), lambda b, i: (b, 0, 0)),
                  pl.BlockSpec(anymat.shape, lambda b, i: (0, 0))],
        out_specs=[pl.BlockSpec((None, tq, A_Q), lambda b, i: (b, i, 0)),
                   pl.BlockSpec((None, tq, A_KV * ns), lambda b, i: (b, i, 0)),
                   pl.BlockSpec((None, None, 8, LANES), lambda b, i: (b, i, 0, 0))],
        out_shape=[jax.ShapeDtypeStruct((n_b, s_len, A_Q), F32),
                   jax.ShapeDtypeStruct((n_b, s_len, A_KV * ns), BF16),
                   jax.ShapeDtypeStruct((n_b, s_len // tq, 8, LANES), F32)],
        compiler_params=_cparams(("arbitrary", "arbitrary")),
        name="nsa_cmp",
    )(q_exp, ckt, cv, anymat)


def _nsa_sw_kernel(fl_ref, q_ref, kv_ref, sel_ref, e_ref, oslc_ref, owin_ref,
                   m_ref, l_ref, acc_ref, *, tq, tk, ns, nq, nkt):
    b = pl.program_id(0)
    i = pl.program_id(1)
    q0 = i * tq
    per = tk // LANES
    trow = lax.broadcasted_iota(jnp.int32, (tq, 1), 0)
    n_pair = A_KV // 2
    hpp = 2 * A_REP
    qp = [jnp.concatenate([_group_q(q_ref, 2 * p), _group_q(q_ref, 2 * p + 1)], axis=0) for p in range(n_pair)]

    m_ref[...] = jnp.full(m_ref.shape, NEG, F32)
    l_ref[...] = jnp.zeros(l_ref.shape, F32)
    acc_ref[...] = jnp.zeros(acc_ref.shape, F32)

    def tiles(j, r0):
        return jnp.concatenate([kv_ref[j * per + u, r0:r0 + LANES, :] for u in range(per)], axis=1)

    def body(j, carry):
        k0 = j * tk
        krel = (k0 - q0) + lax.broadcasted_iota(jnp.int32, (1, tk), 1)
        causal = krel <= trow
        krelf = krel.astype(F32)
        for p in range(n_pair):
            f0 = fl_ref[((b * nq + i) * A_KV + 2 * p) * nkt + j]
            f1 = fl_ref[((b * nq + i) * A_KV + 2 * p + 1) * nkt + j]

            @pl.when((f0 | f1) != 0)
            def _(p=p):
                s = _dot(qp[p], tiles(j, p * LANES))
                selk = _dot(jnp.concatenate([sel_ref[:, (2 * p + k) * ns:(2 * p + k + 1) * ns] for k in range(2)],
                                            axis=0), e_ref[j])
                valid = [(selk[k * tq:(k + 1) * tq, :] > 0.5) & causal for k in range(2)]
                parts = []
                for hh in range(hpp):
                    sr = s[hh * tq:(hh + 1) * tq, :] + SLOPES_A[2 * p * A_REP + hh] * krelf
                    parts.append(jnp.where(valid[hh // A_REP], sr, NEG))
                sm = jnp.concatenate(parts, axis=0)
                m_old = m_ref[p]
                m_new = jnp.maximum(m_old, jnp.max(sm, axis=-1, keepdims=True))
                pe = jnp.exp(sm - m_new)
                alpha = jnp.exp(m_old - m_new)
                l_ref[p] = alpha * l_ref[p] + jnp.sum(pe, axis=-1, keepdims=True)
                acc_ref[p] = alpha * acc_ref[p] + _dot_nt(pe.astype(BF16), tiles(j, 256 + p * LANES))
                m_ref[p] = m_new
        return carry

    lax.fori_loop(0, (q0 + tq + tk - 1) // tk, body, 0)
    pieces = []
    for p in range(n_pair):
        og = acc_ref[p] / l_ref[p]
        for hh in range(hpp):
            pieces.append((og[hh * tq:(hh + 1) * tq, :], hh // A_REP))
    oslc_ref[...] = _assemble(pieces, tq)

    nwt = WIN_A // LANES + tq // LANES
    t0 = jnp.maximum(i * (tq // LANES) - WIN_A // LANES, 0)
    nkw = nwt * LANES
    wrel = (t0 * LANES - q0) + lax.broadcasted_iota(jnp.int32, (1, nkw), 1)
    dist = trow - wrel
    wvalid = (dist >= 0) & (dist <= WIN_A)
    wrelf = wrel.astype(F32)
    pieces = []
    for p in range(n_pair):
        kk = jnp.concatenate([kv_ref[t0 + u, 512 + p * LANES:512 + (p + 1) * LANES, :] for u in range(nwt)], axis=1)
        vv = jnp.concatenate([kv_ref[t0 + u, 768 + p * LANES:768 + (p + 1) * LANES, :] for u in range(nwt)], axis=1)
        s = _dot(qp[p], kk)
        parts = []
        for hh in range(hpp):
            sr = s[hh * tq:(hh + 1) * tq, :] + SLOPES_A[2 * p * A_REP + hh] * wrelf
            parts.append(jnp.where(wvalid, sr, NEG))
        sm = jnp.concatenate(parts, axis=0)
        m = jnp.max(sm, axis=-1, keepdims=True)
        pe = jnp.exp(sm - m)
        den = jnp.sum(pe, axis=-1, keepdims=True)
        og = _dot_nt(pe.astype(BF16), vv) / den
        for hh in range(hpp):
            pieces.append((og[hh * tq:(hh + 1) * tq, :], hh // A_REP))
    owin_ref[...] = _assemble(pieces, tq)


def _nsa_sw(flags, q_exp, kvt, sel, etile, n_b, s_len, tq, tk):
    ns = s_len // SLC_LEN
    nq = s_len // tq
    nkt = s_len // tk
    assert s_len >= WIN_A + tq and tk % tq == 0 and tq % LANES == 0
    grid_spec = pltpu.PrefetchScalarGridSpec(
        num_scalar_prefetch=1,
        grid=(n_b, nq),
        in_specs=[pl.BlockSpec((None, tq, A_HEADS * LANES), lambda b, i, fl: (b, i, 0)),
                  pl.BlockSpec((None, s_len // LANES, 1024, LANES), lambda b, i, fl: (b, 0, 0, 0)),
                  pl.BlockSpec((None, tq, A_KV * ns), lambda b, i, fl: (b, i, 0)),
                  pl.BlockSpec(etile.shape, lambda b, i, fl: (0, 0, 0))],
        out_specs=[pl.BlockSpec((None, tq, A_Q), lambda b, i, fl: (b, i, 0)),
                   pl.BlockSpec((None, tq, A_Q), lambda b, i, fl: (b, i, 0))],
        scratch_shapes=[pltpu.VMEM((A_KV // 2, 2 * A_REP * tq, 1), F32),
                        pltpu.VMEM((A_KV // 2, 2 * A_REP * tq, 1), F32),
                        pltpu.VMEM((A_KV // 2, 2 * A_REP * tq, LANES), F32)],
    )
    return pl.pallas_call(
        functools.partial(_nsa_sw_kernel, tq=tq, tk=tk, ns=ns, nq=nq, nkt=nkt),
        grid_spec=grid_spec,
        out_shape=[jax.ShapeDtypeStruct((n_b, s_len, A_Q), F32),
                   jax.ShapeDtypeStruct((n_b, s_len, A_Q), F32)],
        compiler_params=_cparams(("arbitrary", "arbitrary")),
        name="nsa_sw",
    )(flags, q_exp, kvt, sel, etile)


def _mem_kernel(qm_ref, mkv_ref, o_ref, *, tt):
    lane = lax.broadcasted_iota(jnp.int32, (tt, LANES), 1)
    cols = []
    for c in range(MEM_HEADS // 2):
        qc = qm_ref[:, c * LANES:(c + 1) * LANES]
        mkt = mkv_ref[c * LANES:(c + 1) * LANES, :].astype(BF16)
        mvt = mkv_ref[MEM_W + c * LANES:MEM_W + (c + 1) * LANES, :].astype(BF16)
        halves = []
        for k in (0, 1):
            qh = jnp.where((lane < HEAD_DIM) == (k == 0), qc, 0.0).astype(BF16)
            s = _dot(qh, mkt) * QSCALE
            m = jnp.max(s, axis=-1, keepdims=True)
            e = jnp.exp(s - m)
            den = jnp.sum(e, axis=-1, keepdims=True)
            halves.append(_dot_nt(e.astype(BF16), mvt) / den)
        cols.append(_pick_half(lane, halves[0], halves[1]))
    o_ref[...] = jnp.concatenate(cols, axis=1)


def _mem_attend(rest, qm_blk, mkvt4, layer, n, t, tt):
    mem = mkvt4.shape[3]
    steps = t // tt
    return pl.pallas_call(
        functools.partial(_mem_kernel, tt=tt),
        grid=(n, steps),
        in_specs=[pl.BlockSpec((tt, MEM_W), lambda b, i: (b * steps + i, qm_blk)),
                  pl.BlockSpec((None, None, 2 * MEM_W, mem), lambda b, i: (b, layer, 0, 0))],
        out_specs=pl.BlockSpec((tt, MEM_W), lambda b, i: (b * steps + i, 0)),
        out_shape=jax.ShapeDtypeStruct((n * t, MEM_W), F32),
        compiler_params=_cparams(("arbitrary", "arbitrary")),
        name="mem_attend",
    )(rest, mkvt4)


def _post(x_ref, y, gp_ref, out_ref):
    ms = jnp.mean(y * y, axis=-1, keepdims=True)
    out_ref[...] = x_ref[...] + (y * lax.rsqrt(ms + EPS)) * gp_ref[...]


def _merge_a_kernel(x_ref, oc_ref, os_ref, ow_ref, z_ref, zm_ref, gate_ref, om_ref,
                    w_ref, gp_ref, eg_ref, out_ref):
    gs = 1.0 / (1.0 + jnp.exp(-gate_ref[...]))
    eg = eg_ref[...]
    gexp = sum(_dot(t, eg) for t in _split3(gs))
    o = jnp.zeros(oc_ref.shape, F32)
    for b, ob_ref in enumerate((oc_ref, os_ref, ow_ref)):
        o = o + (ob_ref[...] * _silu(z_ref[:, b * A_Q:(b + 1) * A_Q])) * gexp[:, b * A_Q:(b + 1) * A_Q]
    om = om_ref[...] * _silu(zm_ref[...])
    y = _dot(o.astype(BF16), w_ref[0:A_Q, :]) + _dot(om.astype(BF16), w_ref[A_Q:A_Q + MEM_W, :])
    _post(x_ref, y, gp_ref, out_ref)


def _merge_a(x2d, o_cmp, o_slc, o_win, rest, om, w_bf, g_post, egate, tm=TM):
    m, d = x2d.shape
    tm = min(tm, m)
    row = lambda i: (i, 0)
    return pl.pallas_call(
        _merge_a_kernel,
        grid=(m // tm,),
        in_specs=[pl.BlockSpec((tm, d), row),
                  pl.BlockSpec((tm, A_Q), row), pl.BlockSpec((tm, A_Q), row), pl.BlockSpec((tm, A_Q), row),
                  pl.BlockSpec((tm, 3 * A_Q), row),
                  pl.BlockSpec((tm, MEM_W), lambda i: (i, 10)),
                  pl.BlockSpec((tm, LANES), lambda i: (i, 22)),
                  pl.BlockSpec((tm, MEM_W), row),
                  pl.BlockSpec((A_Q + MEM_W, d), lambda i: (0, 0)),
                  pl.BlockSpec((1, d), lambda i: (0, 0)),
                  pl.BlockSpec((LANES, 3 * A_Q), lambda i: (0, 0))],
        out_specs=pl.BlockSpec((tm, d), row),
        out_shape=jax.ShapeDtypeStruct((m, d), F32),
        compiler_params=_cparams(("arbitrary",)),
        name="merge_a",
    )(x2d, o_cmp, o_slc, o_win, rest, rest, rest, om, w_bf, g_post.reshape(1, d), egate)


def _merge_b_kernel(x_ref, o0_ref, o1_ref, o2_ref, l0_ref, l1_ref, l2_ref, z_ref, zm_ref, om_ref,
                    w_ref, gp_ref, *rest, n_u):
    u_refs = rest[:n_u]
    out_ref = rest[n_u]

    def nat(ref, ui):
        v = ref[...]
        if v.ndim == 2:
            return v
        v = v.reshape(v.shape[0] * v.shape[1], v.shape[2])
        u = u_refs[ui][...]
        return sum(_dot(u, t) for t in _split3(v))

    o0, o1, o2 = nat(o0_ref, 0), nat(o1_ref, 0), nat(o2_ref, 1)
    l0, l1, l2 = nat(l0_ref, 0), nat(l1_ref, 0), nat(l2_ref, 1)
    mx = jnp.maximum(jnp.maximum(l0, l1), l2)
    e0, e1, e2 = jnp.exp(l0 - mx), jnp.exp(l1 - mx), jnp.exp(l2 - mx)
    den = e0 + e1 + e2
    o = (e0 / den) * o0 + (e1 / den) * o1 + (e2 / den) * o2
    o = o * _silu(z_ref[...])
    om = om_ref[...] * _silu(zm_ref[...])
    y = _dot(o.astype(BF16), w_ref[0:B_O, :]) + _dot(om.astype(BF16), w_ref[B_O:B_O + MEM_W, :])
    _post(x_ref, y, gp_ref, out_ref)


def _merge_b(x2d, outs, lses, rest, om, w_bf, g_post, unperms, tm=TM):
    m, d = x2d.shape
    tm = min(tm, m)
    row = lambda i: (i, 0)

    def spec(a):
        if a.ndim == 2:
            return pl.BlockSpec((tm, B_O), row)
        assert tm == CM_TILE
        return pl.BlockSpec((None,) + a.shape[1:], lambda i: (i, 0, 0, 0))

    return pl.pallas_call(
        functools.partial(_merge_b_kernel, n_u=len(unperms)),
        grid=(m // tm,),
        in_specs=[pl.BlockSpec((tm, d), row)] + [spec(a) for a in outs] + [spec(a) for a in lses] +
                 [pl.BlockSpec((tm, B_O), row),
                  pl.BlockSpec((tm, MEM_W), lambda i: (i, 3)),
                  pl.BlockSpec((tm, MEM_W), row),
                  pl.BlockSpec((B_O + MEM_W, d), lambda i: (0, 0)),
                  pl.BlockSpec((1, d), lambda i: (0, 0))] +
                 [pl.BlockSpec(u.shape, lambda i: (0, 0)) for u in unperms],
        out_specs=pl.BlockSpec((tm, d), row),
        out_shape=jax.ShapeDtypeStruct((m, d), F32),
        compiler_params=_cparams(("arbitrary",)),
        name="merge_b",
    )(x2d, *outs, *lses, rest, rest, om, w_bf, g_post.reshape(1, d), *unperms)


def _dil_kernel(q_ref, kc_ref, kp_ref, o_ref, lse_ref, *, tn, wr):
    i = pl.program_id(2)
    nk = wr + tn

    def rows(ref):
        v = ref[...]
        return v if v.ndim == 2 else v.reshape(v.shape[0] * v.shape[1], v.shape[2])

    q, kc, kp = rows(q_ref), rows(kc_ref), rows(kp_ref)
    t = lax.broadcasted_iota(jnp.int32, (tn, 1), 0)
    k = lax.broadcasted_iota(jnp.int32, (1, nk), 1)
    dist = t - k + wr
    valid = (dist >= 0) & (dist <= wr) & ((k >= wr) | (i > 0))
    distf = dist.astype(F32)
    lane = lax.broadcasted_iota(jnp.int32, (tn, LANES), 1)
    o_cols, l_cols = [], []
    hpp = 2 * B_REP
    for p in range(B_KV // 2):
        kk = jnp.concatenate([kp[:, p * LANES:(p + 1) * LANES], kc[:, p * LANES:(p + 1) * LANES]], axis=0)
        vv = jnp.concatenate([kp[:, 256 + p * LANES:256 + (p + 1) * LANES],
                              kc[:, 256 + p * LANES:256 + (p + 1) * LANES]], axis=0)
        qh = jnp.concatenate([q[:, (hpp * p + hh) * LANES:(hpp * p + hh + 1) * LANES]
                              for hh in range(hpp)], axis=0)
        s = _dot_nt(qh, kk)
        parts = []
        for hh in range(hpp):
            sj = s[hh * tn:(hh + 1) * tn, :] - SLOPES_B[hpp * p + hh] * distf
            parts.append(jnp.where(valid, sj, NEG))
        sm = jnp.concatenate(parts, axis=0)
        m = jnp.max(sm, axis=-1, keepdims=True)
        e = jnp.exp(sm - m)
        den = jnp.sum(e, axis=-1, keepdims=True)
        og = _dot(e.astype(BF16), vv) / den
        lse = m + jnp.log(den)
        for k in range(2):
            halves, lhalves = [], []
            for j in range(B_REP):
                r0 = (k * B_REP + j) * tn
                piece = og[r0:r0 + tn, :]
                if k != j:
                    piece = pltpu.roll(piece, HEAD_DIM, 1)
                halves.append(piece)
                lhalves.append(jnp.broadcast_to(lse[r0:r0 + tn], (tn, LANES)))
            o_cols.append(_pick_half(lane, halves[0], halves[1]))
            l_cols.append(_pick_half(lane, lhalves[0], lhalves[1]))
    o = jnp.concatenate(o_cols, axis=1)
    l = jnp.concatenate(l_cols, axis=1)
    o_ref[...] = o.reshape(o_ref.shape)
    lse_ref[...] = l.reshape(lse_ref.shape)


def _dilated_prompt(q, kv, gi, n_b, s_len):
    win, dil = B_GROUPS[gi]
    wr = win // dil
    n = s_len // dil
    tn = min(CM_TILE, n)
    assert n % tn == 0 and tn % wr == 0
    ratio = tn // wr
    qw = B_HPG * LANES
    if dil == 1:
        in_specs = [pl.BlockSpec((None, tn, qw), lambda b, c, i: (b, i, 0)),
                    pl.BlockSpec((None, tn, 512), lambda b, c, i: (b, i, 0)),
                    pl.BlockSpec((None, wr, 512), lambda b, c, i: (b, jnp.maximum(i * ratio - 1, 0), 0))]
        out_spec = pl.BlockSpec((None, tn, B_O), lambda b, c, i: (b, i, 0))
        out_shape = jax.ShapeDtypeStruct((n_b, s_len, B_O), F32)
    else:
        rpt = CM_TILE // dil
        nt, npv = tn // rpt, wr // rpt
        in_specs = [pl.BlockSpec((None, nt, None, rpt, qw), lambda b, c, i: (b, i, c, 0, 0)),
                    pl.BlockSpec((None, nt, None, rpt, 512), lambda b, c, i: (b, i, c, 0, 0)),
                    pl.BlockSpec((None, npv, None, rpt, 512),
                                 lambda b, c, i: (b, jnp.maximum(i * ratio - 1, 0), c, 0, 0))]
        out_spec = pl.BlockSpec((None, nt, None, rpt, B_O), lambda b, c, i: (b, i, c, 0, 0))
        out_shape = jax.ShapeDtypeStruct((n_b, s_len // CM_TILE, dil, rpt, B_O), F32)
    return pl.pallas_call(
        functools.partial(_dil_kernel, tn=tn, wr=wr),
        grid=(n_b, dil, n // tn),
        in_specs=in_specs,
        out_specs=[out_spec, out_spec],
        out_shape=[out_shape, out_shape],
        compiler_params=_cparams(("arbitrary", "arbitrary", "arbitrary")),
        name="dilated_prompt",
    )(q, kv, kv)


def _dil_sample_kernel(q_ref, kn_ref, cache_ref, o_ref, lse_ref, *, t_new, wb):
    rows = B_KV * B_REP * t_new
    rid = lax.broadcasted_iota(jnp.int32, (rows, 1), 0)
    pos = wb + (rid & (t_new - 1))
    lane = lax.broadcasted_iota(jnp.int32, (t_new, LANES), 1)
    zero = jnp.zeros((t_new, LANES), BF16)
    new = jnp.concatenate([kn_ref[...], jnp.zeros((LANES - t_new, 512), F32)], axis=0).astype(BF16)
    nidx = wb + lax.broadcasted_iota(jnp.int32, (1, LANES), 1)
    dn = pos - nidx
    for gi, (win, dil) in enumerate(B_GROUPS):
        lo = wb - min(win, wb)
        nk = wb - lo
        blocks = []
        for h in range(B_KV):
            for j in range(B_REP):
                c = gi * B_HPG + h * B_REP + j
                slot = q_ref[:, c * LANES:(c + 1) * LANES]
                blocks.append(jnp.concatenate([slot, zero] if h < 2 else [zero, slot], axis=1))
        qbd = jnp.concatenate(blocks, axis=0)
        kt = cache_ref[0:256, lo:wb].astype(BF16)
        vt = cache_ref[256:512, lo:wb].astype(BF16)
        s_c = _dot(qbd, kt)
        s_n = _dot_nt(qbd, new[:, 0:256])
        kidx = lo + lax.broadcasted_iota(jnp.int32, (1, nk), 1)
        dc = pos - kidx
        vc = (dc <= win) & ((dc & (dil - 1)) == 0)
        vn = (dn >= 0) & ((dn & (dil - 1)) == 0)
        dcf, dnf = dc.astype(F32), dn.astype(F32)
        pc, pn = [], []
        for hj in range(B_HPG):
            r = slice(hj * t_new, (hj + 1) * t_new)
            sl = SLOPES_B[hj] / dil
            pc.append(jnp.where(vc[r], s_c[r] - sl * dcf[r], NEG))
            pn.append(jnp.where(vn[r], s_n[r] - sl * dnf[r], NEG))
        sc, sn = jnp.concatenate(pc, axis=0), jnp.concatenate(pn, axis=0)
        m = jnp.maximum(jnp.max(sc, axis=-1, keepdims=True), jnp.max(sn, axis=-1, keepdims=True))
        ec, en = jnp.exp(sc - m), jnp.exp(sn - m)
        den = jnp.sum(ec, axis=-1, keepdims=True) + jnp.sum(en, axis=-1, keepdims=True)
        ow = (_dot_nt(ec.astype(BF16), vt) + _dot(en.astype(BF16), new[:, 256:512])) / den
        lse = m + jnp.log(den)
        for h in range(B_KV):
            halves, lhalves = [], []
            for j in range(B_REP):
                r0 = (h * B_REP + j) * t_new
                piece = ow[r0:r0 + t_new, (h // 2) * LANES:(h // 2 + 1) * LANES]
                if (h % 2) != j:
                    piece = pltpu.roll(piece, HEAD_DIM, 1)
                halves.append(piece)
                lhalves.append(jnp.broadcast_to(lse[r0:r0 + t_new], (t_new, LANES)))
            c0 = gi * B_O + h * LANES
            o_ref[:, c0:c0 + LANES] = _pick_half(lane, halves[0], halves[1])
            lse_ref[:, c0:c0 + LANES] = _pick_half(lane, lhalves[0], lhalves[1])


def _dilated_sample(q_exp, kv_new, cache_t, n, t_new):
    wb = cache_t.shape[2]
    assert t_new & (t_new - 1) == 0 and t_new <= LANES and all(w <= wb for w, _ in B_GROUPS)
    o, lse = pl.pallas_call(
        functools.partial(_dil_sample_kernel, t_new=t_new, wb=wb),
        grid=(n,),
        in_specs=[pl.BlockSpec((t_new, N_B_GROUPS * B_HPG * LANES), lambda s: (s, 0)),
                  pl.BlockSpec((t_new, 512), lambda s: (s, 0)),
                  pl.BlockSpec((None, 512, wb), lambda s: (s, 0, 0))],
        out_specs=[pl.BlockSpec((t_new, N_B_GROUPS * B_O), lambda s: (s, 0)),
                   pl.BlockSpec((t_new, N_B_GROUPS * B_O), lambda s: (s, 0))],
        out_shape=[jax.ShapeDtypeStruct((n * t_new, N_B_GROUPS * B_O), F32),
                   jax.ShapeDtypeStruct((n * t_new, N_B_GROUPS * B_O), F32)],
        compiler_params=_cparams(("arbitrary",)),
        name="dilated_sample",
    )(q_exp, kv_new, cache_t)
    return ([o[:, g * B_O:(g + 1) * B_O] for g in range(N_B_GROUPS)],
            [lse[:, g * B_O:(g + 1) * B_O] for g in range(N_B_GROUPS)])


def _nsa_sample_kernel(pt_ref, q_ref, *refs, n_pages, t_new, past, wbuf, ns, nslot):
    ck_refs = refs[0:n_pages]
    sp_refs = refs[n_pages:2 * n_pages]
    kvn_ref, win_ref, es_ref = refs[2 * n_pages:2 * n_pages + 3]
    oc_ref, os_ref, ow_ref = refs[2 * n_pages + 3:2 * n_pages + 6]
    ck_s, kall = refs[2 * n_pages + 6:]
    del pt_ref
    page = sp_refs[0].shape[1]
    hs = nslot // 2
    per_page = page // CMP_LEN
    ck_s[...] = jnp.zeros(ck_s.shape, F32)
    for p in range(n_pages):
        for n in range(per_page):
            c = p * per_page + n
            slot = (c % 2) * hs + c // 2
            ck_s[slot:slot + 1, :] = ck_refs[p][n:n + 1, :]
    for p in range(n_pages):
        kall[:, p * page:(p + 1) * page] = sp_refs[p][...].astype(BF16)
    new = jnp.concatenate([kvn_ref[...], jnp.zeros((LANES - t_new, 3 * A_KVW), F32)], axis=0).astype(BF16)

    rows = A_HEADS * t_new
    zero = jnp.zeros((t_new, LANES), BF16)
    blocks = []
    for g in range(A_KV):
        for r in range(A_REP):
            h = g * A_REP + r
            slot = q_ref[:, h * LANES:(h + 1) * LANES]
            blocks.append(jnp.concatenate([slot, zero] if g < 2 else [zero, slot], axis=1))
    qbd = jnp.concatenate(blocks, axis=0)
    rid = lax.broadcasted_iota(jnp.int32, (rows, 1), 0)
    pos = past + (rid & (t_new - 1))
    posf = pos.astype(F32)
    npos = past + lax.broadcasted_iota(jnp.int32, (1, LANES), 1)
    nd = pos - npos
    ndf = nd.astype(F32)

    def hs_(h):
        return slice(h * t_new, (h + 1) * t_new)

    def head_rows(fn):
        return jnp.concatenate([fn(h) for h in range(A_HEADS)], axis=0)

    def finish(ow):
        pieces = []
        for g in range(A_KV):
            for r in range(A_REP):
                r0 = (g * A_REP + r) * t_new
                pieces.append((ow[r0:r0 + t_new, (g // 2) * LANES:(g // 2 + 1) * LANES], g % 2))
        return _assemble(pieces, t_new)

    n_c = n_pages * per_page
    slot_i = lax.broadcasted_iota(jnp.int32, (1, nslot), 1)
    sl_lo = slot_i & (hs - 1)
    cidx = 2 * sl_lo + jnp.where(slot_i >= hs, 1, 0)
    svalid = (sl_lo < n_c // 2) & (cidx * CMP_LEN + (CMP_LEN - 1) <= pos)
    dist = posf - (cidx.astype(F32) * CMP_LEN + 0.5 * (CMP_LEN - 1))
    s = _dot_nt(qbd, ck_s[:, 0:256].astype(BF16))
    sm = head_rows(lambda h: jnp.where(svalid[hs_(h)], s[hs_(h)] - SLOPES_A[h] * dist[hs_(h)], -jnp.inf))
    m = jnp.max(sm, axis=-1, keepdims=True)
    m = jnp.where(m == -jnp.inf, 0.0, m)
    e = jnp.exp(sm - m)
    den = jnp.sum(e, axis=-1, keepdims=True)
    pc = e / jnp.where(den > 0, den, 1.0)
    oc_ref[...] = finish(_dot(pc.astype(BF16), ck_s[:, 256:512].astype(BF16)))
    g_rows = A_KV * t_new
    ps = jnp.concatenate(
        [sum(pc[hs_(g * A_REP + r)] for r in range(A_REP)) for g in range(A_KV)], axis=0)
    imp = ps + pltpu.roll(ps, hs, 1)
    blk = lax.broadcasted_iota(jnp.int32, (1, nslot), 1)
    gid = lax.broadcasted_iota(jnp.int32, (g_rows, 1), 0)
    cur = jnp.right_shift(past + (gid & (t_new - 1)), 6)
    imp = jnp.where(blk < n_c // 2, imp, 0.0)
    imp = jnp.where((blk == 0) | (blk == cur) | (blk == cur - 1), A_REP + 1.0, imp)
    imp = jnp.where(blk <= cur, imp, -1.0)
    imp = jnp.where(blk < ns, imp, -3.0)
    cnt = jnp.zeros((g_rows, nslot), F32)
    for j in range(ns):
        col = imp[:, j:j + 1]
        ahead = (col > imp) | ((col == imp) & (blk > j))
        cnt = cnt + jnp.where(ahead, 1.0, 0.0)
    sel = jnp.where((cnt < min(N_SEL, ns)) & (blk < ns), 1.0, 0.0).astype(BF16)
    selk = _dot(sel, es_ref[...])
    kpos = lax.broadcasted_iota(jnp.int32, (1, past), 1)
    kposf = kpos.astype(F32)
    s_c = _dot(qbd, kall[0:256, :])
    s_n = _dot_nt(qbd, new[:, 512:768])

    def sel_c(h):
        g = h // A_REP
        return jnp.where(selk[g * t_new:(g + 1) * t_new, 0:past] > 0.5,
                         s_c[hs_(h)] - SLOPES_A[h] * (posf[hs_(h)] - kposf), NEG)

    def sel_n(h):
        g = h // A_REP
        ok = (selk[g * t_new:(g + 1) * t_new, past:past + LANES] > 0.5) & (nd[hs_(h)] >= 0)
        return jnp.where(ok, s_n[hs_(h)] - SLOPES_A[h] * ndf[hs_(h)], NEG)

    sc, sn = head_rows(sel_c), head_rows(sel_n)
    m = jnp.maximum(jnp.max(sc, axis=-1, keepdims=True), jnp.max(sn, axis=-1, keepdims=True))
    ec, en = jnp.exp(sc - m), jnp.exp(sn - m)
    den = jnp.sum(ec, axis=-1, keepdims=True) + jnp.sum(en, axis=-1, keepdims=True)
    os_ref[...] = finish((_dot_nt(ec.astype(BF16), kall[256:512, :]) + _dot(en.astype(BF16), new[:, 768:1024])) / den)
    wpos = (past - wbuf) + lax.broadcasted_iota(jnp.int32, (1, wbuf), 1)
    wd = pos - wpos
    wvalid = (wd <= WIN_A) & (wpos >= 0)
    wdf = wd.astype(F32)
    s_c = _dot(qbd, win_ref[0:256, :].astype(BF16))
    s_n = _dot_nt(qbd, new[:, 1024:1280])
    sc = head_rows(lambda h: jnp.where(wvalid[hs_(h)], s_c[hs_(h)] - SLOPES_A[h] * wdf[hs_(h)], NEG))
    sn = head_rows(lambda h: jnp.where(nd[hs_(h)] >= 0, s_n[hs_(h)] - SLOPES_A[h] * ndf[hs_(h)], NEG))
    m = jnp.maximum(jnp.max(sc, axis=-1, keepdims=True), jnp.max(sn, axis=-1, keepdims=True))
    ec, en = jnp.exp(sc - m), jnp.exp(sn - m)
    den = jnp.sum(ec, axis=-1, keepdims=True) + jnp.sum(en, axis=-1, keepdims=True)
    ow_ref[...] = finish((_dot_nt(ec.astype(BF16), win_ref[256:512, :].astype(BF16)) +
                          _dot(en.astype(BF16), new[:, 1280:1536])) / den)


def _nsa_sample(q_exp, ckv_pool, slc_t, kv_new, win_t, page_table, layer, n, t_new):
    n_pages = page_table.shape[1]
    page = slc_t.shape[3]
    past = n_pages * page
    wbuf = win_t.shape[3]
    per_page = page // CMP_LEN
    tk = past + t_new
    ns = -(-tk // SLC_LEN)
    assert past % CMP_LEN == 0 and t_new < CMP_LEN and t_new & (t_new - 1) == 0 and past >= wbuf
    assert wbuf >= WIN_A and t_new <= SLC_LEN
    hs = HEAD_DIM
    while hs < max(n_pages * per_page // 2, ns):
        hs *= 2
    nslot = 2 * hs
    es = (np.arange(nslot)[:, None] == (np.arange(past + LANES)[None, :] // SLC_LEN)).astype(np.float32)
    es[:, past + t_new:] = 0.0
    es = jnp.asarray(es, BF16)
    ck_specs = [pl.BlockSpec((None, per_page, 512), functools.partial(lambda s, pt, p: (pt[s, p], 0, 0), p=p))
                for p in range(n_pages)]
    sp_specs = [pl.BlockSpec((None, None, 512, page), functools.partial(lambda s, pt, p: (pt[s, p], layer, 0, 0), p=p))
                for p in range(n_pages)]
    grid_spec = pltpu.PrefetchScalarGridSpec(
        num_scalar_prefetch=1,
        grid=(n,),
        in_specs=[pl.BlockSpec((t_new, A_HEADS * LANES), lambda s, pt: (s, 0))] + ck_specs + sp_specs +
                 [pl.BlockSpec((t_new, 3 * A_KVW), lambda s, pt: (s, 0)),
                  pl.BlockSpec((None, None, 512, wbuf), lambda s, pt: (s, layer, 0, 0)),
                  pl.BlockSpec(es.shape, lambda s, pt: (0, 0))],
        out_specs=[pl.BlockSpec((t_new, A_Q), lambda s, pt: (s, 0))] * 3,
        scratch_shapes=[pltpu.VMEM((nslot, 512), F32),
                        pltpu.VMEM((512, past), BF16)],
    )
    return pl.pallas_call(
        functools.partial(_nsa_sample_kernel, n_pages=n_pages, t_new=t_new, past=past, wbuf=wbuf,
                          ns=ns, nslot=nslot),
        grid_spec=grid_spec,
        out_shape=[jax.ShapeDtypeStruct((n * t_new, A_Q), F32)] * 3,
        compiler_params=_cparams(("arbitrary",)),
        name="nsa_sample",
    )(page_table, q_exp, *([ckv_pool] * n_pages), *([slc_t] * n_pages), kv_new, win_t, es)


def _take_cols(w, src):
    src = np.asarray(src)
    cols = jnp.take(w, jnp.asarray(np.maximum(src, 0)), axis=1)
    return jnp.where(jnp.asarray(src >= 0)[None, :], cols, 0.0).astype(BF16)


A_Q0, A_KC0, A_KS0, A_KW0 = 0, A_Q, A_Q + A_KVW, A_Q + 2 * A_KVW
A_GATE0 = A_Q + 3 * A_KVW
A_Z0 = A_GATE0 + 3 * A_HEADS
A_QM0 = A_Z0 + 3 * A_Q
A_ZM0 = A_QM0 + MEM_W
A_QW = A_HEADS * LANES
A_RESTW = 3 * A_Q + 2 * MEM_W + LANES


def _a_q_cols():
    src = []
    for h in range(A_HEADS):
        g = h // A_REP
        slot = [-1] * LANES
        for d in range(HEAD_DIM):
            slot[(g % 2) * HEAD_DIM + d] = A_Q0 + h * HEAD_DIM + d
        src += slot
    return src


def _a_rest_cols():
    return (list(range(A_Z0, A_Z0 + 3 * A_Q)) + list(range(A_QM0, A_QM0 + MEM_W)) +
            list(range(A_ZM0, A_ZM0 + MEM_W)) + list(range(A_GATE0, A_GATE0 + 3 * A_HEADS)) +
            [-1] * (LANES - 3 * A_HEADS))


def _b_q_cols():
    src = []
    for gi in range(N_B_GROUPS):
        for h in range(B_KV):
            for j in range(B_REP):
                slot = [-1] * LANES
                for d in range(HEAD_DIM):
                    slot[(h % 2) * HEAD_DIM + d] = gi * B_O + (h * B_REP + j) * HEAD_DIM + d
                src += slot
    return src


def _blockdiag2(w):
    z = jnp.zeros_like(w)
    top = jnp.concatenate([w, z], axis=-1)
    bot = jnp.concatenate([z, w], axis=-1)
    return jnp.concatenate([top, bot], axis=-2)


def _class_perm(dil):
    p = np.zeros((CM_TILE, CM_TILE), np.float32)
    s = np.arange(CM_TILE)
    p[(s % dil) * (CM_TILE // dil) + s // dil, s] = 1.0
    return p


def kernel(x_prompt, x_sample, mem_prompt, cache_cmp_kv, cache_slc_kv, cache_win_kv, cache_dil_kv, cache_mem_kv,
           page_table, g_pre, g_post, g_mem, w_mem_kv, w_in_a, w_out_a, cmp_pos, cmp_w1, cmp_w2,
           g_kv_b, w_kv_b, w_in_b, w_out_b):
    n_b, s_len, d = x_prompt.shape
    n_s, t_s, _ = x_sample.shape
    depth = g_pre.shape[0]
    n_a = w_in_a.shape[0]
    n_pool, page = cache_cmp_kv.shape[:2]
    mem_len = mem_prompt.shape[1]
    wbuf_a = cache_win_kv.shape[1]
    wbuf_b = cache_dil_kv.shape[1]
    ns_p = s_len // SLC_LEN
    m_p = n_b * s_len
    assert s_len % TM == 0 and mem_len % TM == 0 or mem_len == TM
    spb = s_len // TM
    nkt = s_len // TK

    xp = x_prompt.reshape(m_p, d)
    xs = x_sample.reshape(n_s * t_s, d)
    mem2d = mem_prompt.reshape(n_b * mem_len, d)

    egate = np.zeros((LANES, 3 * A_Q), np.float32)
    for b in range(3):
        for h in range(A_HEADS):
            egate[b * A_HEADS + h, b * A_Q + h * HEAD_DIM:b * A_Q + (h + 1) * HEAD_DIM] = 1.0
    egate = jnp.asarray(egate, BF16)
    etile = (np.arange(ns_p)[None, :, None] ==
             (np.arange(nkt)[:, None, None] * TK + np.arange(TK)[None, None, :]) // SLC_LEN)
    etile = jnp.asarray(etile.astype(np.float32), BF16)
    anymat = np.zeros((ns_p, LANES), np.float32)
    anymat[np.arange(ns_p), np.arange(ns_p) // (TK // SLC_LEN)] = 1.0
    anymat = jnp.asarray(anymat, BF16)
    perms = [jnp.asarray(_class_perm(dil), BF16) for _, dil in B_GROUPS[1:]]
    unperms = [jnp.asarray(_class_perm(dil).T, BF16) for _, dil in B_GROUPS[1:]]

    cmp_t = jnp.transpose(cache_cmp_kv, (0, 2, 3, 4, 5, 1)).reshape(n_pool, n_a, A_KVW, page)
    slc_t = jnp.transpose(cache_slc_kv, (0, 2, 3, 4, 5, 1)).reshape(n_pool, n_a, A_KVW, page)
    win_t = jnp.transpose(cache_win_kv, (0, 2, 3, 4, 5, 1)).reshape(n_s, n_a, A_KVW, wbuf_a)
    dil_t = jnp.transpose(cache_dil_kv, (0, 2, 3, 4, 1)).reshape(n_s, 2 * B_KV * HEAD_DIM, wbuf_b)
    mem_t = jnp.transpose(cache_mem_kv, (0, 2, 3, 4, 5, 1)).reshape(n_s, depth, 2 * MEM_W, mem_len)

    a_q_cols, a_rest_cols, b_q_cols = _a_q_cols(), _a_rest_cols(), _b_q_cols()
    row_spec = lambda w: pl.BlockSpec((TM, w), lambda i: (i, 0))

    kct_l, kst_l, kwt_l, kv_s, mkvt_l = [], [], [], [], []
    kvbt_p = kvb_s = None
    kvb_cm = None
    for l in range(depth):
        mkvt = _proj(mem2d, g_mem[l], [w_mem_kv[l].T.astype(BF16)], [],
                     [(True, 0, a, b, 0, (_st_t(0, a),)) for a, b in _chunks(0, 2 * MEM_W)],
                     [jax.ShapeDtypeStruct((n_b, 2 * MEM_W, mem_len), F32)],
                     [_t_spec(2 * MEM_W, mem_len // min(TM, mem_len), min(TM, mem_len))])[0]
        mkvt_l.append(mkvt)
        mkvt4 = mkvt.reshape(n_b, 1, 2 * MEM_W, mem_len)
        if l < n_a:
            w = w_in_a[l]
            w_n = jnp.concatenate([w[:, A_KC0:A_KC0 + A_KVW].astype(BF16), _take_cols(w, a_q_cols),
                                   _take_cols(w, a_rest_cols)], axis=1)
            w_t = w[:, A_KC0:A_KC0 + 3 * A_KVW].T.astype(BF16)
            w_out = w_out_a[l].astype(BF16)
            pos2 = jnp.concatenate([cmp_pos[l], cmp_pos[l]], axis=-1).transpose(1, 0, 2)
            w1bd = _blockdiag2(cmp_w1[l]).transpose(1, 0, 2, 3).astype(BF16)
            w2bd = _blockdiag2(cmp_w2[l]).astype(BF16)
            w2bdt = jnp.swapaxes(w2bd, 1, 2)
            plan = []
            for j in range(3):
                stores = [_st_t(j, 0)]
                if j > 0:
                    stores.append(_st_ttile(3, (j - 1) * A_KVW))
                plan.append((True, 1, j * A_KVW, (j + 1) * A_KVW, 0, tuple(stores)))
            for j in range(4):
                plan.append((False, 0, j * LANES, (j + 1) * LANES, 0, (_st_rows(4 + j, 0),)))
            for a, b in _chunks(A_KVW, A_KVW + A_QW):
                plan.append((False, 0, a, b, 0, (_st_rows(8, a - A_KVW, QSCALE),)))
            for a, b in _chunks(A_KVW + A_QW, A_KVW + A_QW + A_RESTW):
                plan.append((False, 0, a, b, 0, (_st_rows(9, a - A_KVW - A_QW),)))
            shapes = ([jax.ShapeDtypeStruct((n_b, A_KVW, s_len), F32)] * 3 +
                      [jax.ShapeDtypeStruct((n_b, s_len // LANES, 2 * A_KVW, LANES), BF16)] +
                      [jax.ShapeDtypeStruct((m_p, LANES), F32)] * 4 +
                      [jax.ShapeDtypeStruct((m_p, A_QW), BF16), jax.ShapeDtypeStruct((m_p, A_RESTW), F32)])
            specs = ([_t_spec(A_KVW, spb, TM)] * 3 +
                     [pl.BlockSpec((None, TM // LANES, 2 * A_KVW, LANES), lambda i: (i // spb, i % spb, 0, 0))] +
                     [row_spec(LANES)] * 4 + [row_spec(A_QW), row_spec(A_RESTW)])
            outs = _proj(xp, g_pre[l], [w_n, w_t], [], plan, shapes, specs)
            kct, kst, kwt, kvt = outs[0:4]
            slabs, q, rest = outs[4:8], outs[8], outs[9]
            kct_l.append(kct)
            kst_l.append(kst)
            kwt_l.append(kwt)
            ckt, cv = _compress_prompt(slabs, pos2, w1bd, w2bd, w2bdt, n_b, s_len)
            q3 = q.reshape(n_b, s_len, A_QW)
            o_cmp, sel, flags = _nsa_cmp(q3, ckt, cv, anymat, n_b, s_len, TQ)
            flags = (flags[:, :, 0:A_KV, 0:nkt] > 0.5).astype(jnp.int32).reshape(-1)
            o_slc, o_win = _nsa_sw(flags, q3, kvt, sel, etile, n_b, s_len, TQ, TK)
            om = _mem_attend(rest, 9, mkvt4, 0, n_b, s_len, min(512, s_len))
            xp = _merge_a(xp, o_cmp.reshape(-1, A_Q), o_slc.reshape(-1, A_Q), o_win.reshape(-1, A_Q),
                          rest, om, w_out, g_post[l], egate)
            w_rows = jnp.concatenate([w[:, A_KC0:A_KC0 + 3 * A_KVW].astype(BF16), w_n[:, A_KVW:]], axis=1)
            kv, q, rest = _proj_rows(xs, g_pre[l], w_rows,
                                     [(0, 3 * A_KVW, F32, 1.0), (3 * A_KVW, A_QW, BF16, QSCALE),
                                      (3 * A_KVW + A_QW, A_RESTW, F32, 1.0)])
            kv_s.append(kv)
            posr = jnp.tile(pos2.transpose(1, 0, 2), (1, page // CMP_LEN, 1))
            ckv_pool = _compress_pool(cmp_t, l, posr, w1bd, w2bd, 32)
            ckv_pool = ckv_pool.reshape(n_pool, page // CMP_LEN, 512)
            o_cmp, o_slc, o_win = _nsa_sample(q, ckv_pool, slc_t, kv, win_t, page_table, l, n_s, t_s)
            om = _mem_attend(rest, 9, mem_t, l, n_s, t_s, t_s)
            xs = _merge_a(xs, o_cmp, o_slc, o_win, rest, om, w_out, g_post[l], egate)
            if l == n_a - 1:
                w_kv = w_kv_b.astype(BF16)
                plan = [(True, 1, 0, 512, 0, (_st_t(0, 0),)),
                        (False, 0, 0, 512, 0, (_st_rows(1, 0),)),
                        (False, 0, 0, 512, 1, (_st_cm(2, 0, B_GROUPS[1][1]),)),
                        (False, 0, 0, 512, 2, (_st_cm(3, 0, B_GROUPS[2][1]),))]
                shapes = [jax.ShapeDtypeStruct((n_b, 512, s_len), F32), jax.ShapeDtypeStruct((m_p, 512), BF16)]
                specs = [_t_spec(512, spb, TM), row_spec(512)]
                for _, dil in B_GROUPS[1:]:
                    shapes.append(jax.ShapeDtypeStruct((m_p // CM_TILE, dil, CM_TILE // dil, 512), BF16))
                    specs.append(_cm_spec(dil, 512))
                kvbt_p, kv0, kv1, kv2 = _proj(xp, g_kv_b, [w_kv, w_kv_b.T.astype(BF16)], perms, plan, shapes, specs)
                kvb_cm = [kv0.reshape(n_b, s_len, 512)] + [
                    a.reshape((n_b, s_len // CM_TILE) + a.shape[1:]) for a in (kv1, kv2)]
                kvb_s = _proj_rows(xs, g_kv_b, w_kv, [(0, 512, F32, 1.0)])[0]
        else:
            lb = l - n_a
            w = w_in_b[lb]
            w_n = jnp.concatenate([_take_cols(w, b_q_cols), w[:, B_Q:].astype(BF16)], axis=1)
            w_out = w_out_b[lb].astype(BF16)
            qw = B_HPG * LANES
            restw = B_O + 2 * MEM_W
            plan, shapes, specs = [], [], []
            for gi, (_, dil) in enumerate(B_GROUPS):
                for a, b in _chunks(gi * qw, (gi + 1) * qw):
                    st = _st_rows(gi, a - gi * qw, QSCALE) if dil == 1 else _st_cm(gi, a - gi * qw, dil, QSCALE)
                    plan.append((False, 0, a, b, gi, (st,)))
                if dil == 1:
                    shapes.append(jax.ShapeDtypeStruct((m_p, qw), BF16))
                    specs.append(row_spec(qw))
                else:
                    shapes.append(jax.ShapeDtypeStruct((m_p // CM_TILE, dil, CM_TILE // dil, qw), BF16))
                    specs.append(_cm_spec(dil, qw))
            for a, b in _chunks(3 * qw, 3 * qw + restw):
                plan.append((False, 0, a, b, 0, (_st_rows(3, a - 3 * qw),)))
            shapes.append(jax.ShapeDtypeStruct((m_p, restw), F32))
            specs.append(row_spec(restw))
            q0, q1, q2, rest = _proj(xp, g_pre[l], [w_n], perms, plan, shapes, specs)
            qs = [q0.reshape(n_b, s_len, qw)] + [a.reshape((n_b, s_len // CM_TILE) + a.shape[1:]) for a in (q1, q2)]
            outs, lses = [], []
            for gi in range(N_B_GROUPS):
                o, lse = _dilated_prompt(qs[gi], kvb_cm[gi], gi, n_b, s_len)
                if gi == 0:
                    o, lse = o.reshape(m_p, B_O), lse.reshape(m_p, B_O)
                else:
                    o, lse = (a.reshape((m_p // CM_TILE,) + a.shape[2:]) for a in (o, lse))
                outs.append(o)
                lses.append(lse)
            om = _mem_attend(rest, 2, mkvt4, 0, n_b, s_len, min(512, s_len))
            xp = _merge_b(xp, outs, lses, rest, om, w_out, g_post[l], unperms)
            q, rest = _proj_rows(xs, g_pre[l], w_n, [(0, 3 * qw, BF16, QSCALE), (3 * qw, restw, F32, 1.0)])
            outs, lses = _dilated_sample(q, kvb_s, dil_t, n_s, t_s)
            om = _mem_attend(rest, 2, mem_t, l, n_s, t_s, t_s)
            xs = _merge_b(xs, outs, lses, rest, om, w_out, g_post[l], [])

    def from_t(arrs, n, t):
        a = jnp.stack(arrs, 1).reshape(n, len(arrs), 2, A_KV, HEAD_DIM, t)
        return jnp.transpose(a, (0, 5, 1, 2, 3, 4))

    def kv_stack(kvs, n, t, j):
        return jnp.stack([k[:, j * A_KVW:(j + 1) * A_KVW].reshape(n, t, 2, A_KV, HEAD_DIM) for k in kvs], 2)

    new_cmp_p = from_t(kct_l, n_b, s_len)
    new_slc_p = from_t(kst_l, n_b, s_len)
    wa = min(WIN_A, s_len)
    new_win_p = from_t([k[:, :, s_len - wa:] for k in kwt_l], n_b, wa)
    new_cmp_s = kv_stack(kv_s, n_s, t_s, 0)
    new_slc_s = kv_stack(kv_s, n_s, t_s, 1)
    win_s = kv_stack(kv_s, n_s, t_s, 2)
    new_win_s = jnp.concatenate([cache_win_kv, win_s], 1)[:, -wbuf_a:]
    wb = min(B_GROUPS[-1][0], s_len)
    new_dil_p = jnp.transpose(kvbt_p[:, :, s_len - wb:].reshape(n_b, 2, B_KV, HEAD_DIM, wb), (0, 4, 1, 2, 3))
    new_dil_s = jnp.concatenate([cache_dil_kv, kvb_s.reshape(n_s, t_s, 2, B_KV, HEAD_DIM)], 1)[:, -wbuf_b:]
    new_mem_p = jnp.transpose(jnp.stack(mkvt_l, 1).reshape(n_b, depth, 2, MEM_HEADS, HEAD_DIM, mem_len),
                              (0, 5, 1, 2, 3, 4))
    return (xp.reshape(n_b, s_len, d), xs.reshape(n_s, t_s, d), new_cmp_p, new_cmp_s, new_slc_p, new_slc_s,
            new_win_p, new_win_s, new_dil_p, new_dil_s, new_mem_p)
```

```python
import functools
import numpy as np
import jax
import jax.numpy as jnp
from jax import lax
from jax.experimental import pallas as pl
from jax.experimental.pallas import tpu as pltpu

F32 = jnp.float32
BF16 = jnp.bfloat16

HEAD_DIM = 64
LANES = 128
A_HEADS = 12
A_KV = 4
A_REP = A_HEADS // A_KV
CMP_LEN = 32
SLC_LEN = 64
N_SEL = 16
WIN_A = 512
B_GROUPS = ((128, 1), (512, 4), (2048, 16))
N_B_GROUPS = 3
B_KV = 4
B_REP = 2
B_HPG = B_KV * B_REP
MEM_HEADS = 4
EPS = 1e-6
A_Q = A_HEADS * HEAD_DIM
A_KVW = 2 * A_KV * HEAD_DIM
MEM_W = MEM_HEADS * HEAD_DIM
B_Q = N_B_GROUPS * B_HPG * HEAD_DIM
B_O = B_HPG * HEAD_DIM
QSCALE = HEAD_DIM ** -0.5
NEG = -1e30
VMEM_LIMIT = 56 * 1024 * 1024
TM = 256
TQ = 128
TK = 512
CM_TILE = 256


def _alibi(n):
    return [float(2.0 ** (-8.0 * i / n)) for i in range(1, n + 1)]


SLOPES_A = _alibi(A_HEADS)
SLOPES_B = _alibi(B_HPG)


def _cparams(sem):
    return pltpu.CompilerParams(dimension_semantics=sem, vmem_limit_bytes=VMEM_LIMIT)


def _silu(x):
    return x * (1.0 / (1.0 + jnp.exp(-x)))


def _dot_nt(a, b):
    return lax.dot_general(a, b, (((1,), (1,)), ((), ())), preferred_element_type=F32)


def _dot(a, b):
    return jnp.dot(a, b, preferred_element_type=F32)


def _split3(x):
    hi = x.astype(BF16)
    r1 = x - hi.astype(F32)
    mid = r1.astype(BF16)
    lo = (r1 - mid.astype(F32)).astype(BF16)
    return hi, mid, lo


def _pick_half(lane, a, b):
    return jnp.where(lane < HEAD_DIM, a, b)


def _assemble(pieces, rows):
    lane = lax.broadcasted_iota(jnp.int32, (rows, LANES), 1)
    cols = []
    for c in range(len(pieces) // 2):
        halves = []
        for k in (0, 1):
            arr, useful = pieces[2 * c + k]
            if useful != k:
                arr = pltpu.roll(arr, HEAD_DIM, 1)
            halves.append(arr)
        cols.append(_pick_half(lane, halves[0], halves[1]))
    return jnp.concatenate(cols, axis=1)


def _proj_kernel(x_ref, g_ref, *refs, n_w, n_p, plan):
    w_refs = refs[:n_w]
    p_refs = refs[n_w:n_w + n_p]
    out_refs = refs[n_w + n_p:]
    x = x_ref[...]
    ms = jnp.mean(x * x, axis=-1, keepdims=True)
    xn = ((x * lax.rsqrt(ms + EPS)) * g_ref[...]).astype(BF16)
    xs = [xn] + [_dot(p_ref[...], xn).astype(BF16) for p_ref in p_refs]
    for (transposed, wi, c0, c1, xi, stores) in plan:
        if transposed:
            y = _dot_nt(w_refs[wi][c0:c1, :], xs[xi])
        else:
            y = _dot(xs[xi], w_refs[wi][:, c0:c1])
        for store in stores:
            store(out_refs, y)


def _st_rows(oi, o0, scale=1.0):
    def store(out_refs, y):
        v = y if scale == 1.0 else y * scale
        out_refs[oi][:, o0:o0 + y.shape[1]] = v.astype(out_refs[oi].dtype)
    return store


def _st_t(oi, o0):
    def store(out_refs, y):
        out_refs[oi][o0:o0 + y.shape[0], :] = y.astype(out_refs[oi].dtype)
    return store


def _st_ttile(oi, o0):
    def store(out_refs, y):
        for u in range(y.shape[1] // LANES):
            out_refs[oi][u, o0:o0 + y.shape[0], :] = y[:, u * LANES:(u + 1) * LANES].astype(out_refs[oi].dtype)
    return store


def _st_cm(oi, o0, dil, scale=1.0):
    def store(out_refs, y):
        v = y if scale == 1.0 else y * scale
        v = v.astype(out_refs[oi].dtype)
        out_refs[oi][:, :, o0:o0 + y.shape[1]] = v.reshape(dil, y.shape[0] // dil, y.shape[1])
    return store


def _chunks(c0, c1, step=512):
    return [(a, min(a + step, c1)) for a in range(c0, c1, step)]


def _proj(x2d, g, weights, perms, plan, out_shapes, out_specs, tm=TM):
    m, d = x2d.shape
    tm = min(tm, m)
    const2 = lambda i: (0, 0)
    return pl.pallas_call(
        functools.partial(_proj_kernel, n_w=len(weights), n_p=len(perms), plan=tuple(plan)),
        grid=(m // tm,),
        in_specs=[pl.BlockSpec((tm, d), lambda i: (i, 0)), pl.BlockSpec((1, d), const2)] +
                 [pl.BlockSpec(w.shape, const2) for w in weights] +
                 [pl.BlockSpec(p.shape, const2) for p in perms],
        out_specs=out_specs,
        out_shape=out_shapes,
        compiler_params=_cparams(("arbitrary",)),
        name="proj",
    )(x2d, g.reshape(1, d), *weights, *perms)


def _proj_rows(x2d, g, w_bf, outs):
    m = x2d.shape[0]
    tm = min(TM, m)
    plan = []
    for oi, (c0, width, _, scale) in enumerate(outs):
        for a, b in _chunks(c0, c0 + width):
            plan.append((False, 0, a, b, 0, (_st_rows(oi, a - c0, scale),)))
    return _proj(x2d, g, [w_bf], [], plan,
                 [jax.ShapeDtypeStruct((m, o[1]), o[2]) for o in outs],
                 [pl.BlockSpec((tm, o[1]), lambda i: (i, 0)) for o in outs])


def _t_spec(width, spb, tm):
    return pl.BlockSpec((None, width, tm), lambda i: (i // spb, 0, i % spb))


def _cm_spec(dil, width):
    return pl.BlockSpec((None, dil, CM_TILE // dil, width), lambda i: (i, 0, 0, 0))


def _compress_pool_kernel(x_ref, posr_ref, w1_ref, w2_ref, o_ref, s0, s1, s2, s3, *, pages, pitch):
    scr = (s0, s1, s2, s3)
    page = x_ref.shape[2]
    per_page = page // CMP_LEN
    rows = per_page * pages

    def relayout(pg, carry):
        for idx in range(4):
            t = x_ref[pg, idx * LANES:(idx + 1) * LANES, :].T + posr_ref[idx // 2]
            for k in range(page // 8):
                n, l0 = (8 * k) // CMP_LEN, (8 * k) % CMP_LEN
                scr[idx][pl.ds(l0 * pitch + pg * per_page + n, 8, stride=pitch), :] = t[8 * k:8 * k + 8, :]
        return carry

    lax.fori_loop(0, pages, relayout, 0, unroll=4)
    for idx in range(4):
        kv = idx // 2
        acc = jnp.zeros((rows, LANES), F32)
        for l in range(CMP_LEN):
            xl = scr[idx][l * pitch:l * pitch + rows, :].astype(BF16)
            acc = acc + _dot(xl, w1_ref[l, kv])
        h = _silu(acc).astype(BF16)
        o_ref[:, idx * LANES:(idx + 1) * LANES] = _dot(h, w2_ref[kv])


def _compress_pool(cmp_t, layer, posr, w1bd, w2bd, pages):
    n_pool, _, _, page = cmp_t.shape
    per_page = page // CMP_LEN
    pages = max(p for p in range(1, min(pages, n_pool) + 1) if n_pool % p == 0 and (per_page * p) % 16 == 0)
    assert page % CMP_LEN == 0 and page == LANES
    rows = per_page * pages
    pitch = rows + 8
    assert rows % 16 == 0
    return pl.pallas_call(
        functools.partial(_compress_pool_kernel, pages=pages, pitch=pitch),
        grid=(n_pool // pages,),
        in_specs=[pl.BlockSpec((pages, None, 512, page), lambda i: (i, layer, 0, 0)),
                  pl.BlockSpec(posr.shape, lambda i: (0, 0, 0)),
                  pl.BlockSpec(w1bd.shape, lambda i: (0, 0, 0, 0)),
                  pl.BlockSpec(w2bd.shape, lambda i: (0, 0, 0))],
        out_specs=pl.BlockSpec((rows, 512), lambda i: (i, 0)),
        out_shape=jax.ShapeDtypeStruct((n_pool * per_page, 512), F32),
        scratch_shapes=[pltpu.VMEM((CMP_LEN * pitch, LANES), F32)] * 4,
        compiler_params=_cparams(("arbitrary",)),
        name="compress_pool",
    )(cmp_t, posr, w1bd, w2bd)


def _compress_prompt_kernel(s0, s1, s2, s3, pos_ref, w1_ref, w2_ref, w2t_ref, ckt_ref, cv_ref, *, nc):
    half = nc // 2
    slabs = (s0, s1, s2, s3)
    accs = [jnp.zeros((nc, LANES), F32) for _ in range(4)]
    for l in range(CMP_LEN):
        for idx in range(4):
            kv = idx // 2
            xe = slabs[idx][pl.ds(l, half, stride=2 * CMP_LEN), :]
            xo = slabs[idx][pl.ds(CMP_LEN + l, half, stride=2 * CMP_LEN), :]
            xb = (jnp.concatenate([xe, xo], axis=0) + pos_ref[l, kv:kv + 1, :]).astype(BF16)
            accs[idx] = accs[idx] + _dot(xb, w1_ref[l, kv])
    for idx in range(4):
        kv, p = idx // 2, idx % 2
        h = _silu(accs[idx]).astype(BF16)
        if kv == 0:
            ckt_ref[p * LANES:(p + 1) * LANES, :] = _dot_nt(w2t_ref[0], h)
        else:
            cv_ref[:, p * LANES:(p + 1) * LANES] = _dot(h, w2_ref[1])


def _compress_prompt(slabs, pos2, w1bd, w2bd, w2bdt, n_b, s_len):
    nc = s_len // CMP_LEN
    slab_spec = pl.BlockSpec((s_len, LANES), lambda b: (b, 0))
    return pl.pallas_call(
        functools.partial(_compress_prompt_kernel, nc=nc),
        grid=(n_b,),
        in_specs=[slab_spec] * 4 +
                 [pl.BlockSpec(pos2.shape, lambda b: (0, 0, 0)),
                  pl.BlockSpec(w1bd.shape, lambda b: (0, 0, 0, 0)),
                  pl.BlockSpec(w2bd.shape, lambda b: (0, 0, 0)),
                  pl.BlockSpec(w2bdt.shape, lambda b: (0, 0, 0))],
        out_specs=[pl.BlockSpec((None, 256, nc), lambda b: (b, 0, 0)),
                   pl.BlockSpec((None, nc, 256), lambda b: (b, 0, 0))],
        out_shape=[jax.ShapeDtypeStruct((n_b, 256, nc), F32),
                   jax.ShapeDtypeStruct((n_b, nc, 256), F32)],
        compiler_params=_cparams(("arbitrary",)),
        name="compress_prompt",
    )(*slabs, pos2, w1bd, w2bd, w2bdt)


def _topk_mask_t(imp_t, k):
    nb, cols = imp_t.shape
    row = lax.broadcasted_iota(jnp.int32, (nb, cols), 0)
    work = imp_t
    sel = jnp.zeros((nb, cols), F32)
    for _ in range(k):
        m = jnp.max(work, axis=0, keepdims=True)
        idx = jnp.min(jnp.where(work == m, row, nb), axis=0, keepdims=True)
        hit = row == idx
        sel = jnp.where(hit, 1.0, sel)
        work = jnp.where(hit, -3.0, work)
    return sel


def _group_q(q_ref, g):
    return jnp.concatenate(
        [q_ref[:, (g * A_REP + r) * LANES:(g * A_REP + r + 1) * LANES] for r in range(A_REP)], axis=0)


def _nsa_cmp_kernel(q_ref, ckt_ref, cv_ref, any_ref, o_ref, sel_ref, flag_ref, *, tq, nc, ns):
    i = pl.program_id(1)
    q0 = i * tq
    half = nc // 2
    pos = q0 + lax.broadcasted_iota(jnp.int32, (tq, 1), 0)
    posf = pos.astype(F32)
    tok = lax.broadcasted_iota(jnp.int32, (1, nc), 1)
    cidx = jnp.where(tok < half, 2 * tok, 2 * (tok - half) + 1)
    c_end = cidx * CMP_LEN + (CMP_LEN - 1)
    c_mid = cidx.astype(F32) * CMP_LEN + 0.5 * (CMP_LEN - 1)
    cmask = c_end <= pos
    dist = posf - c_mid
    blk = lax.broadcasted_iota(jnp.int32, (1, ns), 1)
    cur = jnp.right_shift(pos, 6)
    forced = (blk == 0) | (blk == cur) | (blk == cur - 1)
    allowed = blk <= cur
    pieces = []
    anys = []
    for p in range(A_KV // 2):
        ckt = ckt_ref[p * LANES:(p + 1) * LANES, :].astype(BF16)
        cv = cv_ref[:, p * LANES:(p + 1) * LANES].astype(BF16)
        qq = jnp.concatenate([_group_q(q_ref, 2 * p), _group_q(q_ref, 2 * p + 1)], axis=0)
        s = _dot(qq, ckt)
        parts = []
        for hh in range(2 * A_REP):
            sr = s[hh * tq:(hh + 1) * tq, :] - SLOPES_A[2 * p * A_REP + hh] * dist
            parts.append(jnp.where(cmask, sr, -jnp.inf))
        sm = jnp.concatenate(parts, axis=0)
        m = jnp.max(sm, axis=-1, keepdims=True)
        m = jnp.where(m == -jnp.inf, 0.0, m)
        e = jnp.exp(sm - m)
        den = jnp.sum(e, axis=-1, keepdims=True)
        pc = e / jnp.where(den > 0, den, 1.0)
        og = _dot(pc.astype(BF16), cv)
        imps = []
        for k in range(2):
            for r in range(A_REP):
                pieces.append((og[(k * A_REP + r) * tq:(k * A_REP + r + 1) * tq, :], k))
            ps = sum(pc[(k * A_REP + r) * tq:(k * A_REP + r + 1) * tq] for r in range(A_REP))
            imp = ps[:, 0:half] + ps[:, half:nc]
            imp = jnp.where(allowed, imp, -1.0)
            imps.append(jnp.where(forced, -3.0, imp).T)
        sel_t = _topk_mask_t(jnp.concatenate(imps, axis=1), max(min(N_SEL, ns) - 3, 0))
        for k in range(2):
            g = 2 * p + k
            sel = jnp.where(forced, 1.0, sel_t[:, k * tq:(k + 1) * tq].T)
            sel_ref[:, g * ns:(g + 1) * ns] = sel.astype(BF16)
            anys.append(jnp.max(sel, axis=0, keepdims=True))
    o_ref[...] = _assemble(pieces, tq)
    rows = jnp.concatenate(anys + [jnp.zeros((8 - A_KV, ns), F32)], axis=0).astype(BF16)
    flag_ref[...] = _dot(rows, any_ref[...])


def _nsa_cmp(q_exp, ckt, cv, anymat, n_b, s_len, tq):
    nc = ckt.shape[2]
    ns = s_len // SLC_LEN
    assert nc == 2 * ns
    return pl.pallas_call(
        functools.partial(_nsa_cmp_kernel, tq=tq, nc=nc, ns=ns),
        grid=(n_b, s_len // tq),
        in_specs=[pl.BlockSpec((None, tq, A_HEADS * LANES), lambda b, i: (b, i, 0)),
                  pl.BlockSpec((None, 256, nc), lambda b, i: (b, 0, 0)),
                  pl.BlockSpec((None, nc, 256), lambda b, i: (b, 0, 0)),
                  pl.BlockSpec(anymat.shape, lambda b, i: (0, 0))],
        out_specs=[pl.BlockSpec((None, tq, A_Q), lambda b, i: (b, i, 0)),
                   pl.BlockSpec((None, tq, A_KV * ns), lambda b, i: (b, i, 0)),
                   pl.BlockSpec((None, None, 8, LANES), lambda b, i: (b, i, 0, 0))],
        out_shape=[jax.ShapeDtypeStruct((n_b, s_len, A_Q), F32),
                   jax.ShapeDtypeStruct((n_b, s_len, A_KV * ns), BF16),
                   jax.ShapeDtypeStruct((n_b, s_len // tq, 8, LANES), F32)],
        compiler_params=_cparams(("arbitrary", "arbitrary")),
        name="nsa_cmp",
    )(q_exp, ckt, cv, anymat)


def _nsa_sw_kernel(fl_ref, q_ref, kv_ref, sel_ref, e_ref, oslc_ref, owin_ref,
                   m_ref, l_ref, acc_ref, *, tq, tk, ns, nq, nkt):
    b = pl.program_id(0)
    i = pl.program_id(1)
    q0 = i * tq
    per = tk // LANES
    trow = lax.broadcasted_iota(jnp.int32, (tq, 1), 0)
    n_pair = A_KV // 2
    hpp = 2 * A_REP
    qp = [jnp.concatenate([_group_q(q_ref, 2 * p), _group_q(q_ref, 2 * p + 1)], axis=0) for p in range(n_pair)]

    m_ref[...] = jnp.full(m_ref.shape, NEG, F32)
    l_ref[...] = jnp.zeros(l_ref.shape, F32)
    acc_ref[...] = jnp.zeros(acc_ref.shape, F32)

    def tiles(j, r0):
        return jnp.concatenate([kv_ref[j * per + u, r0:r0 + LANES, :] for u in range(per)], axis=1)

    def body(j, carry):
        k0 = j * tk
        krel = (k0 - q0) + lax.broadcasted_iota(jnp.int32, (1, tk), 1)
        causal = krel <= trow
        krelf = krel.astype(F32)
        for p in range(n_pair):
            f0 = fl_ref[((b * nq + i) * A_KV + 2 * p) * nkt + j]
            f1 = fl_ref[((b * nq + i) * A_KV + 2 * p + 1) * nkt + j]

            @pl.when((f0 | f1) != 0)
            def _(p=p):
                s = _dot(qp[p], tiles(j, p * LANES))
                selk = _dot(jnp.concatenate([sel_ref[:, (2 * p + k) * ns:(2 * p + k + 1) * ns] for k in range(2)],
                                            axis=0), e_ref[j])
                valid = [(selk[k * tq:(k + 1) * tq, :] > 0.5) & causal for k in range(2)]
                parts = []
                for hh in range(hpp):
                    sr = s[hh * tq:(hh + 1) * tq, :] + SLOPES_A[2 * p * A_REP + hh] * krelf
                    parts.append(jnp.where(valid[hh // A_REP], sr, NEG))
                sm = jnp.concatenate(parts, axis=0)
                m_old = m_ref[p]
                m_new = jnp.maximum(m_old, jnp.max(sm, axis=-1, keepdims=True))
                pe = jnp.exp(sm - m_new)
                alpha = jnp.exp(m_old - m_new)
                l_ref[p] = alpha * l_ref[p] + jnp.sum(pe, axis=-1, keepdims=True)
                acc_ref[p] = alpha * acc_ref[p] + _dot_nt(pe.astype(BF16), tiles(j, 256 + p * LANES))
                m_ref[p] = m_new
        return carry

    lax.fori_loop(0, (q0 + tq + tk - 1) // tk, body, 0)
    pieces = []
    for p in range(n_pair):
        og = acc_ref[p] / l_ref[p]
        for hh in range(hpp):
            pieces.append((og[hh * tq:(hh + 1) * tq, :], hh // A_REP))
    oslc_ref[...] = _assemble(pieces, tq)

    nwt = WIN_A // LANES + tq // LANES
    t0 = jnp.maximum(i * (tq // LANES) - WIN_A // LANES, 0)
    nkw = nwt * LANES
    wrel = (t0 * LANES - q0) + lax.broadcasted_iota(jnp.int32, (1, nkw), 1)
    dist = trow - wrel
    wvalid = (dist >= 0) & (dist <= WIN_A)
    wrelf = wrel.astype(F32)
    pieces = []
    for p in range(n_pair):
        kk = jnp.concatenate([kv_ref[t0 + u, 512 + p * LANES:512 + (p + 1) * LANES, :] for u in range(nwt)], axis=1)
        vv = jnp.concatenate([kv_ref[t0 + u, 768 + p * LANES:768 + (p + 1) * LANES, :] for u in range(nwt)], axis=1)
        s = _dot(qp[p], kk)
        parts = []
        for hh in range(hpp):
            sr = s[hh * tq:(hh + 1) * tq, :] + SLOPES_A[2 * p * A_REP + hh] * wrelf
            parts.append(jnp.where(wvalid, sr, NEG))
        sm = jnp.concatenate(parts, axis=0)
        m = jnp.max(sm, axis=-1, keepdims=True)
        pe = jnp.exp(sm - m)
        den = jnp.sum(pe, axis=-1, keepdims=True)
        og = _dot_nt(pe.astype(BF16), vv) / den
        for hh in range(hpp):
            pieces.append((og[hh * tq:(hh + 1) * tq, :], hh // A_REP))
    owin_ref[...] = _assemble(pieces, tq)


def _nsa_sw(flags, q_exp, kvt, sel, etile, n_b, s_len, tq, tk):
    ns = s_len // SLC_LEN
    nq = s_len // tq
    nkt = s_len // tk
    assert s_len >= WIN_A + tq and tk % tq == 0 and tq % LANES == 0
    grid_spec = pltpu.PrefetchScalarGridSpec(
        num_scalar_prefetch=1,
        grid=(n_b, nq),
        in_specs=[pl.BlockSpec((None, tq, A_HEADS * LANES), lambda b, i, fl: (b, i, 0)),
                  pl.BlockSpec((None, s_len // LANES, 1024, LANES), lambda b, i, fl: (b, 0, 0, 0)),
                  pl.BlockSpec((None, tq, A_KV * ns), lambda b, i, fl: (b, i, 0)),
                  pl.BlockSpec(etile.shape, lambda b, i, fl: (0, 0, 0))],
        out_specs=[pl.BlockSpec((None, tq, A_Q), lambda b, i, fl: (b, i, 0)),
                   pl.BlockSpec((None, tq, A_Q), lambda b, i, fl: (b, i, 0))],
        scratch_shapes=[pltpu.VMEM((A_KV // 2, 2 * A_REP * tq, 1), F32),
                        pltpu.VMEM((A_KV // 2, 2 * A_REP * tq, 1), F32),
                        pltpu.VMEM((A_KV // 2, 2 * A_REP * tq, LANES), F32)],
    )
    return pl.pallas_call(
        functools.partial(_nsa_sw_kernel, tq=tq, tk=tk, ns=ns, nq=nq, nkt=nkt),
        grid_spec=grid_spec,
        out_shape=[jax.ShapeDtypeStruct((n_b, s_len, A_Q), F32),
                   jax.ShapeDtypeStruct((n_b, s_len, A_Q), F32)],
        compiler_params=_cparams(("arbitrary", "arbitrary")),
        name="nsa_sw",
    )(flags, q_exp, kvt, sel, etile)


def _mem_kernel(qm_ref, mkv_ref, o_ref, *, tt):
    lane = lax.broadcasted_iota(jnp.int32, (tt, LANES), 1)
    cols = []
    for c in range(MEM_HEADS // 2):
        qc = qm_ref[:, c * LANES:(c + 1) * LANES]
        mkt = mkv_ref[c * LANES:(c + 1) * LANES, :].astype(BF16)
        mvt = mkv_ref[MEM_W + c * LANES:MEM_W + (c + 1) * LANES, :].astype(BF16)
        halves = []
        for k in (0, 1):
            qh = jnp.where((lane < HEAD_DIM) == (k == 0), qc, 0.0).astype(BF16)
            s = _dot(qh, mkt) * QSCALE
            m = jnp.max(s, axis=-1, keepdims=True)
            e = jnp.exp(s - m)
            den = jnp.sum(e, axis=-1, keepdims=True)
            halves.append(_dot_nt(e.astype(BF16), mvt) / den)
        cols.append(_pick_half(lane, halves[0], halves[1]))
    o_ref[...] = jnp.concatenate(cols, axis=1)


def _mem_attend(rest, qm_blk, mkvt4, layer, n, t, tt):
    mem = mkvt4.shape[3]
    steps = t // tt
    return pl.pallas_call(
        functools.partial(_mem_kernel, tt=tt),
        grid=(n, steps),
        in_specs=[pl.BlockSpec((tt, MEM_W), lambda b, i: (b * steps + i, qm_blk)),
                  pl.BlockSpec((None, None, 2 * MEM_W, mem), lambda b, i: (b, layer, 0, 0))],
        out_specs=pl.BlockSpec((tt, MEM_W), lambda b, i: (b * steps + i, 0)),
        out_shape=jax.ShapeDtypeStruct((n * t, MEM_W), F32),
        compiler_params=_cparams(("arbitrary", "arbitrary")),
        name="mem_attend",
    )(rest, mkvt4)


def _post(x_ref, y, gp_ref, out_ref):
    ms = jnp.mean(y * y, axis=-1, keepdims=True)
    out_ref[...] = x_ref[...] + (y * lax.rsqrt(ms + EPS)) * gp_ref[...]


def _merge_a_kernel(x_ref, oc_ref, os_ref, ow_ref, z_ref, zm_ref, gate_ref, om_ref,
                    w_ref, gp_ref, eg_ref, out_ref):
    gs = 1.0 / (1.0 + jnp.exp(-gate_ref[...]))
    eg = eg_ref[...]
    gexp = sum(_dot(t, eg) for t in _split3(gs))
    o = jnp.zeros(oc_ref.shape, F32)
    for b, ob_ref in enumerate((oc_ref, os_ref, ow_ref)):
        o = o + (ob_ref[...] * _silu(z_ref[:, b * A_Q:(b + 1) * A_Q])) * gexp[:, b * A_Q:(b + 1) * A_Q]
    om = om_ref[...] * _silu(zm_ref[...])
    y = _dot(o.astype(BF16), w_ref[0:A_Q, :]) + _dot(om.astype(BF16), w_ref[A_Q:A_Q + MEM_W, :])
    _post(x_ref, y, gp_ref, out_ref)


def _merge_a(x2d, o_cmp, o_slc, o_win, rest, om, w_bf, g_post, egate, tm=TM):
    m, d = x2d.shape
    tm = min(tm, m)
    row = lambda i: (i, 0)
    return pl.pallas_call(
        _merge_a_kernel,
        grid=(m // tm,),
        in_specs=[pl.BlockSpec((tm, d), row),
                  pl.BlockSpec((tm, A_Q), row), pl.BlockSpec((tm, A_Q), row), pl.BlockSpec((tm, A_Q), row),
                  pl.BlockSpec((tm, 3 * A_Q), row),
                  pl.BlockSpec((tm, MEM_W), lambda i: (i, 10)),
                  pl.BlockSpec((tm, LANES), lambda i: (i, 22)),
                  pl.BlockSpec((tm, MEM_W), row),
                  pl.BlockSpec((A_Q + MEM_W, d), lambda i: (0, 0)),
                  pl.BlockSpec((1, d), lambda i: (0, 0)),
                  pl.BlockSpec((LANES, 3 * A_Q), lambda i: (0, 0))],
        out_specs=pl.BlockSpec((tm, d), row),
        out_shape=jax.ShapeDtypeStruct((m, d), F32),
        compiler_params=_cparams(("arbitrary",)),
        name="merge_a",
    )(x2d, o_cmp, o_slc, o_win, rest, rest, rest, om, w_bf, g_post.reshape(1, d), egate)


def _merge_b_kernel(x_ref, o0_ref, o1_ref, o2_ref, l0_ref, l1_ref, l2_ref, z_ref, zm_ref, om_ref,
                    w_ref, gp_ref, *rest, n_u):
    u_refs = rest[:n_u]
    out_ref = rest[n_u]

    def nat(ref, ui):
        v = ref[...]
        if v.ndim == 2:
            return v
        v = v.reshape(v.shape[0] * v.shape[1], v.shape[2])
        u = u_refs[ui][...]
        return sum(_dot(u, t) for t in _split3(v))

    o0, o1, o2 = nat(o0_ref, 0), nat(o1_ref, 0), nat(o2_ref, 1)
    l0, l1, l2 = nat(l0_ref, 0), nat(l1_ref, 0), nat(l2_ref, 1)
    mx = jnp.maximum(jnp.maximum(l0, l1), l2)
    e0, e1, e2 = jnp.exp(l0 - mx), jnp.exp(l1 - mx), jnp.exp(l2 - mx)
    den = e0 + e1 + e2
    o = (e0 / den) * o0 + (e1 / den) * o1 + (e2 / den) * o2
    o = o * _silu(z_ref[...])
    om = om_ref[...] * _silu(zm_ref[...])
    y = _dot(o.astype(BF16), w_ref[0:B_O, :]) + _dot(om.astype(BF16), w_ref[B_O:B_O + MEM_W, :])
    _post(x_ref, y, gp_ref, out_ref)


def _merge_b(x2d, outs, lses, rest, om, w_bf, g_post, unperms, tm=TM):
    m, d = x2d.shape
    tm = min(tm, m)
    row = lambda i: (i, 0)

    def spec(a):
        if a.ndim == 2:
            return pl.BlockSpec((tm, B_O), row)
        assert tm == CM_TILE
        return pl.BlockSpec((None,) + a.shape[1:], lambda i: (i, 0, 0, 0))

    return pl.pallas_call(
        functools.partial(_merge_b_kernel, n_u=len(unperms)),
        grid=(m // tm,),
        in_specs=[pl.BlockSpec((tm, d), row)] + [spec(a) for a in outs] + [spec(a) for a in lses] +
                 [pl.BlockSpec((tm, B_O), row),
                  pl.BlockSpec((tm, MEM_W), lambda i: (i, 3)),
                  pl.BlockSpec((tm, MEM_W), row),
                  pl.BlockSpec((B_O + MEM_W, d), lambda i: (0, 0)),
                  pl.BlockSpec((1, d), lambda i: (0, 0))] +
                 [pl.BlockSpec(u.shape, lambda i: (0, 0)) for u in unperms],
        out_specs=pl.BlockSpec((tm, d), row),
        out_shape=jax.ShapeDtypeStruct((m, d), F32),
        compiler_params=_cparams(("arbitrary",)),
        name="merge_b",
    )(x2d, *outs, *lses, rest, rest, om, w_bf, g_post.reshape(1, d), *unperms)


def _dil_kernel(q_ref, kc_ref, kp_ref, o_ref, lse_ref, *, tn, wr):
    i = pl.program_id(2)
    nk = wr + tn

    def rows(ref):
        v = ref[...]
        return v if v.ndim == 2 else v.reshape(v.shape[0] * v.shape[1], v.shape[2])

    q, kc, kp = rows(q_ref), rows(kc_ref), rows(kp_ref)
    t = lax.broadcasted_iota(jnp.int32, (tn, 1), 0)
    k = lax.broadcasted_iota(jnp.int32, (1, nk), 1)
    dist = t - k + wr
    valid = (dist >= 0) & (dist <= wr) & ((k >= wr) | (i > 0))
    distf = dist.astype(F32)
    lane = lax.broadcasted_iota(jnp.int32, (tn, LANES), 1)
    o_cols, l_cols = [], []
    hpp = 2 * B_REP
    for p in range(B_KV // 2):
        kk = jnp.concatenate([kp[:, p * LANES:(p + 1) * LANES], kc[:, p * LANES:(p + 1) * LANES]], axis=0)
        vv = jnp.concatenate([kp[:, 256 + p * LANES:256 + (p + 1) * LANES],
                              kc[:, 256 + p * LANES:256 + (p + 1) * LANES]], axis=0)
        qh = jnp.concatenate([q[:, (hpp * p + hh) * LANES:(hpp * p + hh + 1) * LANES]
                              for hh in range(hpp)], axis=0)
        s = _dot_nt(qh, kk)
        parts = []
        for hh in range(hpp):
            sj = s[hh * tn:(hh + 1) * tn, :] - SLOPES_B[hpp * p + hh] * distf
            parts.append(jnp.where(valid, sj, NEG))
        sm = jnp.concatenate(parts, axis=0)
        m = jnp.max(sm, axis=-1, keepdims=True)
        e = jnp.exp(sm - m)
        den = jnp.sum(e, axis=-1, keepdims=True)
        og = _dot(e.astype(BF16), vv) / den
        lse = m + jnp.log(den)
        for k in range(2):
            halves, lhalves = [], []
            for j in range(B_REP):
                r0 = (k * B_REP + j) * tn
                piece = og[r0:r0 + tn, :]
                if k != j:
                    piece = pltpu.roll(piece, HEAD_DIM, 1)
                halves.append(piece)
                lhalves.append(jnp.broadcast_to(lse[r0:r0 + tn], (tn, LANES)))
            o_cols.append(_pick_half(lane, halves[0], halves[1]))
            l_cols.append(_pick_half(lane, lhalves[0], lhalves[1]))
    o = jnp.concatenate(o_cols, axis=1)
    l = jnp.concatenate(l_cols, axis=1)
    o_ref[...] = o.reshape(o_ref.shape)
    lse_ref[...] = l.reshape(lse_ref.shape)


def _dilated_prompt(q, kv, gi, n_b, s_len):
    win, dil = B_GROUPS[gi]
    wr = win // dil
    n = s_len // dil
    tn = min(CM_TILE, n)
    assert n % tn == 0 and tn % wr == 0
    ratio = tn // wr
    qw = B_HPG * LANES
    if dil == 1:
        in_specs = [pl.BlockSpec((None, tn, qw), lambda b, c, i: (b, i, 0)),
                    pl.BlockSpec((None, tn, 512), lambda b, c, i: (b, i, 0)),
                    pl.BlockSpec((None, wr, 512), lambda b, c, i: (b, jnp.maximum(i * ratio - 1, 0), 0))]
        out_spec = pl.BlockSpec((None, tn, B_O), lambda b, c, i: (b, i, 0))
        out_shape = jax.ShapeDtypeStruct((n_b, s_len, B_O), F32)
    else:
        rpt = CM_TILE // dil
        nt, npv = tn // rpt, wr // rpt
        in_specs = [pl.BlockSpec((None, nt, None, rpt, qw), lambda b, c, i: (b, i, c, 0, 0)),
                    pl.BlockSpec((None, nt, None, rpt, 512), lambda b, c, i: (b, i, c, 0, 0)),
                    pl.BlockSpec((None, npv, None, rpt, 512),
                                 lambda b, c, i: (b, jnp.maximum(i * ratio - 1, 0), c, 0, 0))]
        out_spec = pl.BlockSpec((None, nt, None, rpt, B_O), lambda b, c, i: (b, i, c, 0, 0))
        out_shape = jax.ShapeDtypeStruct((n_b, s_len // CM_TILE, dil, rpt, B_O), F32)
    return pl.pallas_call(
        functools.partial(_dil_kernel, tn=tn, wr=wr),
        grid=(n_b, dil, n // tn),
        in_specs=in_specs,
        out_specs=[out_spec, out_spec],
        out_shape=[out_shape, out_shape],
        compiler_params=_cparams(("arbitrary", "arbitrary", "arbitrary")),
        name="dilated_prompt",
    )(q, kv, kv)


def _dil_sample_kernel(q_ref, kn_ref, cache_ref, o_ref, lse_ref, *, t_new, wb):
    rows = B_KV * B_REP * t_new
    rid = lax.broadcasted_iota(jnp.int32, (rows, 1), 0)
    pos = wb + (rid & (t_new - 1))
    lane = lax.broadcasted_iota(jnp.int32, (t_new, LANES), 1)
    zero = jnp.zeros((t_new, LANES), BF16)
    new = jnp.concatenate([kn_ref[...], jnp.zeros((LANES - t_new, 512), F32)], axis=0).astype(BF16)
    nidx = wb + lax.broadcasted_iota(jnp.int32, (1, LANES), 1)
    dn = pos - nidx
    for gi, (win, dil) in enumerate(B_GROUPS):
        lo = wb - min(win, wb)
        nk = wb - lo
        blocks = []
        for h in range(B_KV):
            for j in range(B_REP):
                c = gi * B_HPG + h * B_REP + j
                slot = q_ref[:, c * LANES:(c + 1) * LANES]
                blocks.append(jnp.concatenate([slot, zero] if h < 2 else [zero, slot], axis=1))
        qbd = jnp.concatenate(blocks, axis=0)
        kt = cache_ref[0:256, lo:wb].astype(BF16)
        vt = cache_ref[256:512, lo:wb].astype(BF16)
        s_c = _dot(qbd, kt)
        s_n = _dot_nt(qbd, new[:, 0:256])
        kidx = lo + lax.broadcasted_iota(jnp.int32, (1, nk), 1)
        dc = pos - kidx
        vc = (dc <= win) & ((dc & (dil - 1)) == 0)
        vn = (dn >= 0) & ((dn & (dil - 1)) == 0)
        dcf, dnf = dc.astype(F32), dn.astype(F32)
        pc, pn = [], []
        for hj in range(B_HPG):
            r = slice(hj * t_new, (hj + 1) * t_new)
            sl = SLOPES_B[hj] / dil
            pc.append(jnp.where(vc[r], s_c[r] - sl * dcf[r], NEG))
            pn.append(jnp.where(vn[r], s_n[r] - sl * dnf[r], NEG))
        sc, sn = jnp.concatenate(pc, axis=0), jnp.concatenate(pn, axis=0)
        m = jnp.maximum(jnp.max(sc, axis=-1, keepdims=True), jnp.max(sn, axis=-1, keepdims=True))
        ec, en = jnp.exp(sc - m), jnp.exp(sn - m)
        den = jnp.sum(ec, axis=-1, keepdims=True) + jnp.sum(en, axis=-1, keepdims=True)
        ow = (_dot_nt(ec.astype(BF16), vt) + _dot(en.astype(BF16), new[:, 256:512])) / den
        lse = m + jnp.log(den)
        for h in range(B_KV):
            halves, lhalves = [], []
            for j in range(B_REP):
                r0 = (h * B_REP + j) * t_new
                piece = ow[r0:r0 + t_new, (h // 2) * LANES:(h // 2 + 1) * LANES]
                if (h % 2) != j:
                    piece = pltpu.roll(piece, HEAD_DIM, 1)
                halves.append(piece)
                lhalves.append(jnp.broadcast_to(lse[r0:r0 + t_new], (t_new, LANES)))
            c0 = gi * B_O + h * LANES
            o_ref[:, c0:c0 + LANES] = _pick_half(lane, halves[0], halves[1])
            lse_ref[:, c0:c0 + LANES] = _pick_half(lane, lhalves[0], lhalves[1])


def _dilated_sample(q_exp, kv_new, cache_t, n, t_new):
    wb = cache_t.shape[2]
    assert t_new & (t_new - 1) == 0 and t_new <= LANES and all(w <= wb for w, _ in B_GROUPS)
    o, lse = pl.pallas_call(
        functools.partial(_dil_sample_kernel, t_new=t_new, wb=wb),
        grid=(n,),
        in_specs=[pl.BlockSpec((t_new, N_B_GROUPS * B_HPG * LANES), lambda s: (s, 0)),
                  pl.BlockSpec((t_new, 512), lambda s: (s, 0)),
                  pl.BlockSpec((None, 512, wb), lambda s: (s, 0, 0))],
        out_specs=[pl.BlockSpec((t_new, N_B_GROUPS * B_O), lambda s: (s, 0)),
                   pl.BlockSpec((t_new, N_B_GROUPS * B_O), lambda s: (s, 0))],
        out_shape=[jax.ShapeDtypeStruct((n * t_new, N_B_GROUPS * B_O), F32),
                   jax.ShapeDtypeStruct((n * t_new, N_B_GROUPS * B_O), F32)],
        compiler_params=_cparams(("arbitrary",)),
        name="dilated_sample",
    )(q_exp, kv_new, cache_t)
    return ([o[:, g * B_O:(g + 1) * B_O] for g in range(N_B_GROUPS)],
            [lse[:, g * B_O:(g + 1) * B_O] for g in range(N_B_GROUPS)])


def _nsa_sample_kernel(pt_ref, q_ref, *refs, n_pages, t_new, past, wbuf, ns, nslot):
    ck_refs = refs[0:n_pages]
    sp_refs = refs[n_pages:2 * n_pages]
    kvn_ref, win_ref, es_ref = refs[2 * n_pages:2 * n_pages + 3]
    oc_ref, os_ref, ow_ref = refs[2 * n_pages + 3:2 * n_pages + 6]
    ck_s, kall = refs[2 * n_pages + 6:]
    del pt_ref
    page = sp_refs[0].shape[1]
    hs = nslot // 2
    per_page = page // CMP_LEN
    ck_s[...] = jnp.zeros(ck_s.shape, F32)
    for p in range(n_pages):
        for n in range(per_page):
            c = p * per_page + n
            slot = (c % 2) * hs + c // 2
            ck_s[slot:slot + 1, :] = ck_refs[p][n:n + 1, :]
    for p in range(n_pages):
        kall[:, p * page:(p + 1) * page] = sp_refs[p][...].astype(BF16)
    new = jnp.concatenate([kvn_ref[...], jnp.zeros((LANES - t_new, 3 * A_KVW), F32)], axis=0).astype(BF16)

    rows = A_HEADS * t_new
    zero = jnp.zeros((t_new, LANES), BF16)
    blocks = []
    for g in range(A_KV):
        for r in range(A_REP):
            h = g * A_REP + r
            slot = q_ref[:, h * LANES:(h + 1) * LANES]
            blocks.append(jnp.concatenate([slot, zero] if g < 2 else [zero, slot], axis=1))
    qbd = jnp.concatenate(blocks, axis=0)
    rid = lax.broadcasted_iota(jnp.int32, (rows, 1), 0)
    pos = past + (rid & (t_new - 1))
    posf = pos.astype(F32)
    npos = past + lax.broadcasted_iota(jnp.int32, (1, LANES), 1)
    nd = pos - npos
    ndf = nd.astype(F32)

    def hs_(h):
        return slice(h * t_new, (h + 1) * t_new)

    def head_rows(fn):
        return jnp.concatenate([fn(h) for h in range(A_HEADS)], axis=0)

    def finish(ow):
        pieces = []
        for g in range(A_KV):
            for r in range(A_REP):
                r0 = (g * A_REP + r) * t_new
                pieces.append((ow[r0:r0 + t_new, (g // 2) * LANES:(g // 2 + 1) * LANES], g % 2))
        return _assemble(pieces, t_new)

    n_c = n_pages * per_page
    slot_i = lax.broadcasted_iota(jnp.int32, (1, nslot), 1)
    sl_lo = slot_i & (hs - 1)
    cidx = 2 * sl_lo + jnp.where(slot_i >= hs, 1, 0)
    svalid = (sl_lo < n_c // 2) & (cidx * CMP_LEN + (CMP_LEN - 1) <= pos)
    dist = posf - (cidx.astype(F32) * CMP_LEN + 0.5 * (CMP_LEN - 1))
    s = _dot_nt(qbd, ck_s[:, 0:256].astype(BF16))
    sm = head_rows(lambda h: jnp.where(svalid[hs_(h)], s[hs_(h)] - SLOPES_A[h] * dist[hs_(h)], -jnp.inf))
    m = jnp.max(sm, axis=-1, keepdims=True)
    m = jnp.where(m == -jnp.inf, 0.0, m)
    e = jnp.exp(sm - m)
    den = jnp.sum(e, axis=-1, keepdims=True)
    pc = e / jnp.where(den > 0, den, 1.0)
    oc_ref[...] = finish(_dot(pc.astype(BF16), ck_s[:, 256:512].astype(BF16)))
    g_rows = A_KV * t_new
    ps = jnp.concatenate(
        [sum(pc[hs_(g * A_REP + r)] for r in range(A_REP)) for g in range(A_KV)], axis=0)
    imp = ps + pltpu.roll(ps, hs, 1)
    blk = lax.broadcasted_iota(jnp.int32, (1, nslot), 1)
    gid = lax.broadcasted_iota(jnp.int32, (g_rows, 1), 0)
    cur = jnp.right_shift(past + (gid & (t_new - 1)), 6)
    imp = jnp.where(blk < n_c // 2, imp, 0.0)
    imp = jnp.where((blk == 0) | (blk == cur) | (blk == cur - 1), A_REP + 1.0, imp)
    imp = jnp.where(blk <= cur, imp, -1.0)
    imp = jnp.where(blk < ns, imp, -3.0)
    cnt = jnp.zeros((g_rows, nslot), F32)
    for j in range(ns):
        col = imp[:, j:j + 1]
        ahead = (col > imp) | ((col == imp) & (blk > j))
        cnt = cnt + jnp.where(ahead, 1.0, 0.0)
    sel = jnp.where((cnt < min(N_SEL, ns)) & (blk < ns), 1.0, 0.0).astype(BF16)
    selk = _dot(sel, es_ref[...])
    kpos = lax.broadcasted_iota(jnp.int32, (1, past), 1)
    kposf = kpos.astype(F32)
    s_c = _dot(qbd, kall[0:256, :])
    s_n = _dot_nt(qbd, new[:, 512:768])

    def sel_c(h):
        g = h // A_REP
        return jnp.where(selk[g * t_new:(g + 1) * t_new, 0:past] > 0.5,
                         s_c[hs_(h)] - SLOPES_A[h] * (posf[hs_(h)] - kposf), NEG)

    def sel_n(h):
        g = h // A_REP
        ok = (selk[g * t_new:(g + 1) * t_new, past:past + LANES] > 0.5) & (nd[hs_(h)] >= 0)
        return jnp.where(ok, s_n[hs_(h)] - SLOPES_A[h] * ndf[hs_(h)], NEG)

    sc, sn = head_rows(sel_c), head_rows(sel_n)
    m = jnp.maximum(jnp.max(sc, axis=-1, keepdims=True), jnp.max(sn, axis=-1, keepdims=True))
    ec, en = jnp.exp(sc - m), jnp.exp(sn - m)
    den = jnp.sum(ec, axis=-1, keepdims=True) + jnp.sum(en, axis=-1, keepdims=True)
    os_ref[...] = finish((_dot_nt(ec.astype(BF16), kall[256:512, :]) + _dot(en.astype(BF16), new[:, 768:1024])) / den)
    wpos = (past - wbuf) + lax.broadcasted_iota(jnp.int32, (1, wbuf), 1)
    wd = pos - wpos
    wvalid = (wd <= WIN_A) & (wpos >= 0)
    wdf = wd.astype(F32)
    s_c = _dot(qbd, win_ref[0:256, :].astype(BF16))
    s_n = _dot_nt(qbd, new[:, 1024:1280])
    sc = head_rows(lambda h: jnp.where(wvalid[hs_(h)], s_c[hs_(h)] - SLOPES_A[h] * wdf[hs_(h)], NEG))
    sn = head_rows(lambda h: jnp.where(nd[hs_(h)] >= 0, s_n[hs_(h)] - SLOPES_A[h] * ndf[hs_(h)], NEG))
    m = jnp.maximum(jnp.max(sc, axis=-1, keepdims=True), jnp.max(sn, axis=-1, keepdims=True))
    ec, en = jnp.exp(sc - m), jnp.exp(sn - m)
    den = jnp.sum(ec, axis=-1, keepdims=True) + jnp.sum(en, axis=-1, keepdims=True)
    ow_ref[...] = finish((_dot_nt(ec.astype(BF16), win_ref[256:512, :].astype(BF16)) +
                          _dot(en.astype(BF16), new[:, 1280:1536])) / den)


def _nsa_sample(q_exp, ckv_pool, slc_t, kv_new, win_t, page_table, layer, n, t_new):
    n_pages = page_table.shape[1]
    page = slc_t.shape[3]
    past = n_pages * page
    wbuf = win_t.shape[3]
    per_page = page // CMP_LEN
    tk = past + t_new
    ns = -(-tk // SLC_LEN)
    assert past % CMP_LEN == 0 and t_new < CMP_LEN and t_new & (t_new - 1) == 0 and past >= wbuf
    assert wbuf >= WIN_A and t_new <= SLC_LEN
    hs = HEAD_DIM
    while hs < max(n_pages * per_page // 2, ns):
        hs *= 2
    nslot = 2 * hs
    es = (np.arange(nslot)[:, None] == (np.arange(past + LANES)[None, :] // SLC_LEN)).astype(np.float32)
    es[:, past + t_new:] = 0.0
    es = jnp.asarray(es, BF16)
    ck_specs = [pl.BlockSpec((None, per_page, 512), functools.partial(lambda s, pt, p: (pt[s, p], 0, 0), p=p))
                for p in range(n_pages)]
    sp_specs = [pl.BlockSpec((None, None, 512, page), functools.partial(lambda s, pt, p: (pt[s, p], layer, 0, 0), p=p))
                for p in range(n_pages)]
    grid_spec = pltpu.PrefetchScalarGridSpec(
        num_scalar_prefetch=1,
        grid=(n,),
        in_specs=[pl.BlockSpec((t_new, A_HEADS * LANES), lambda s, pt: (s, 0))] + ck_specs + sp_specs +
                 [pl.BlockSpec((t_new, 3 * A_KVW), lambda s, pt: (s, 0)),
                  pl.BlockSpec((None, None, 512, wbuf), lambda s, pt: (s, layer, 0, 0)),
                  pl.BlockSpec(es.shape, lambda s, pt: (0, 0))],
        out_specs=[pl.BlockSpec((t_new, A_Q), lambda s, pt: (s, 0))] * 3,
        scratch_shapes=[pltpu.VMEM((nslot, 512), F32),
                        pltpu.VMEM((512, past), BF16)],
    )
    return pl.pallas_call(
        functools.partial(_nsa_sample_kernel, n_pages=n_pages, t_new=t_new, past=past, wbuf=wbuf,
                          ns=ns, nslot=nslot),
        grid_spec=grid_spec,
        out_shape=[jax.ShapeDtypeStruct((n * t_new, A_Q), F32)] * 3,
        compiler_params=_cparams(("arbitrary",)),
        name="nsa_sample",
    )(page_table, q_exp, *([ckv_pool] * n_pages), *([slc_t] * n_pages), kv_new, win_t, es)


def _take_cols(w, src):
    src = np.asarray(src)
    cols = jnp.take(w, jnp.asarray(np.maximum(src, 0)), axis=1)
    return jnp.where(jnp.asarray(src >= 0)[None, :], cols, 0.0).astype(BF16)


A_Q0, A_KC0, A_KS0, A_KW0 = 0, A_Q, A_Q + A_KVW, A_Q + 2 * A_KVW
A_GATE0 = A_Q + 3 * A_KVW
A_Z0 = A_GATE0 + 3 * A_HEADS
A_QM0 = A_Z0 + 3 * A_Q
A_ZM0 = A_QM0 + MEM_W
A_QW = A_HEADS * LANES
A_RESTW = 3 * A_Q + 2 * MEM_W + LANES


def _a_q_cols():
    src = []
    for h in range(A_HEADS):
        g = h // A_REP
        slot = [-1] * LANES
        for d in range(HEAD_DIM):
            slot[(g % 2) * HEAD_DIM + d] = A_Q0 + h * HEAD_DIM + d
        src += slot
    return src


def _a_rest_cols():
    return (list(range(A_Z0, A_Z0 + 3 * A_Q)) + list(range(A_QM0, A_QM0 + MEM_W)) +
            list(range(A_ZM0, A_ZM0 + MEM_W)) + list(range(A_GATE0, A_GATE0 + 3 * A_HEADS)) +
            [-1] * (LANES - 3 * A_HEADS))


def _b_q_cols():
    src = []
    for gi in range(N_B_GROUPS):
        for h in range(B_KV):
            for j in range(B_REP):
                slot = [-1] * LANES
                for d in range(HEAD_DIM):
                    slot[(h % 2) * HEAD_DIM + d] = gi * B_O + (h * B_REP + j) * HEAD_DIM + d
                src += slot
    return src


def _blockdiag2(w):
    z = jnp.zeros_like(w)
    top = jnp.concatenate([w, z], axis=-1)
    bot = jnp.concatenate([z, w], axis=-1)
    return jnp.concatenate([top, bot], axis=-2)


def _class_perm(dil):
    p = np.zeros((CM_TILE, CM_TILE), np.float32)
    s = np.arange(CM_TILE)
    p[(s % dil) * (CM_TILE // dil) + s // dil, s] = 1.0
    return p


def kernel(x_prompt, x_sample, mem_prompt, cache_cmp_kv, cache_slc_kv, cache_win_kv, cache_dil_kv, cache_mem_kv,
           page_table, g_pre, g_post, g_mem, w_mem_kv, w_in_a, w_out_a, cmp_pos, cmp_w1, cmp_w2,
           g_kv_b, w_kv_b, w_in_b, w_out_b):
    n_b, s_len, d = x_prompt.shape
    n_s, t_s, _ = x_sample.shape
    depth = g_pre.shape[0]
    n_a = w_in_a.shape[0]
    n_pool, page = cache_cmp_kv.shape[:2]
    mem_len = mem_prompt.shape[1]
    wbuf_a = cache_win_kv.shape[1]
    wbuf_b = cache_dil_kv.shape[1]
    ns_p = s_len // SLC_LEN
    m_p = n_b * s_len
    assert s_len % TM == 0 and mem_len % TM == 0 or mem_len == TM
    spb = s_len // TM
    nkt = s_len // TK

    xp = x_prompt.reshape(m_p, d)
    xs = x_sample.reshape(n_s * t_s, d)
    mem2d = mem_prompt.reshape(n_b * mem_len, d)

    egate = np.zeros((LANES, 3 * A_Q), np.float32)
    for b in range(3):
        for h in range(A_HEADS):
            egate[b * A_HEADS + h, b * A_Q + h * HEAD_DIM:b * A_Q + (h + 1) * HEAD_DIM] = 1.0
    egate = jnp.asarray(egate, BF16)
    etile = (np.arange(ns_p)[None, :, None] ==
             (np.arange(nkt)[:, None, None] * TK + np.arange(TK)[None, None, :]) // SLC_LEN)
    etile = jnp.asarray(etile.astype(np.float32), BF16)
    anymat = np.zeros((ns_p, LANES), np.float32)
    anymat[np.arange(ns_p), np.arange(ns_p) // (TK // SLC_LEN)] = 1.0
    anymat = jnp.asarray(anymat, BF16)
    perms = [jnp.asarray(_class_perm(dil), BF16) for _, dil in B_GROUPS[1:]]
    unperms = [jnp.asarray(_class_perm(dil).T, BF16) for _, dil in B_GROUPS[1:]]

    cmp_t = jnp.transpose(cache_cmp_kv, (0, 2, 3, 4, 5, 1)).reshape(n_pool, n_a, A_KVW, page)
    slc_t = jnp.transpose(cache_slc_kv, (0, 2, 3, 4, 5, 1)).reshape(n_pool, n_a, A_KVW, page)
    win_t = jnp.transpose(cache_win_kv, (0, 2, 3, 4, 5, 1)).reshape(n_s, n_a, A_KVW, wbuf_a)
    dil_t = jnp.transpose(cache_dil_kv, (0, 2, 3, 4, 1)).reshape(n_s, 2 * B_KV * HEAD_DIM, wbuf_b)
    mem_t = jnp.transpose(cache_mem_kv, (0, 2, 3, 4, 5, 1)).reshape(n_s, depth, 2 * MEM_W, mem_len)

    a_q_cols, a_rest_cols, b_q_cols = _a_q_cols(), _a_rest_cols(), _b_q_cols()
    row_spec = lambda w: pl.BlockSpec((TM, w), lambda i: (i, 0))

    kct_l, kst_l, kwt_l, kv_s, mkvt_l = [], [], [], [], []
    kvbt_p = kvb_s = None
    kvb_cm = None
    for l in range(depth):
        mkvt = _proj(mem2d, g_mem[l], [w_mem_kv[l].T.astype(BF16)], [],
                     [(True, 0, a, b, 0, (_st_t(0, a),)) for a, b in _chunks(0, 2 * MEM_W)],
                     [jax.ShapeDtypeStruct((n_b, 2 * MEM_W, mem_len), F32)],
                     [_t_spec(2 * MEM_W, mem_len // min(TM, mem_len), min(TM, mem_len))])[0]
        mkvt_l.append(mkvt)
        mkvt4 = mkvt.reshape(n_b, 1, 2 * MEM_W, mem_len)
        if l < n_a:
            w = w_in_a[l]
            w_n = jnp.concatenate([w[:, A_KC0:A_KC0 + A_KVW].astype(BF16), _take_cols(w, a_q_cols),
                                   _take_cols(w, a_rest_cols)], axis=1)
            w_t = w[:, A_KC0:A_KC0 + 3 * A_KVW].T.astype(BF16)
            w_out = w_out_a[l].astype(BF16)
            pos2 = jnp.concatenate([cmp_pos[l], cmp_pos[l]], axis=-1).transpose(1, 0, 2)
            w1bd = _blockdiag2(cmp_w1[l]).transpose(1, 0, 2, 3).astype(BF16)
            w2bd = _blockdiag2(cmp_w2[l]).astype(BF16)
            w2bdt = jnp.swapaxes(w2bd, 1, 2)
            plan = []
            for j in range(3):
                stores = [_st_t(j, 0)]
                if j > 0:
                    stores.append(_st_ttile(3, (j - 1) * A_KVW))
                plan.append((True, 1, j * A_KVW, (j + 1) * A_KVW, 0, tuple(stores)))
            for j in range(4):
                plan.append((False, 0, j * LANES, (j + 1) * LANES, 0, (_st_rows(4 + j, 0),)))
            for a, b in _chunks(A_KVW, A_KVW + A_QW):
                plan.append((False, 0, a, b, 0, (_st_rows(8, a - A_KVW, QSCALE),)))
            for a, b in _chunks(A_KVW + A_QW, A_KVW + A_QW + A_RESTW):
                plan.append((False, 0, a, b, 0, (_st_rows(9, a - A_KVW - A_QW),)))
            qm0 = A_KVW + A_QW + 3 * A_Q
            plan.append((False, 0, qm0, qm0 + MEM_W, 0, (_st_rows(10, 0),)))
            shapes = ([jax.ShapeDtypeStruct((n_b, A_KVW, s_len), F32)] * 3 +
                      [jax.ShapeDtypeStruct((n_b, s_len // LANES, 2 * A_KVW, LANES), BF16)] +
                      [jax.ShapeDtypeStruct((m_p, LANES), F32)] * 4 +
                      [jax.ShapeDtypeStruct((m_p, A_QW), BF16), jax.ShapeDtypeStruct((m_p, A_RESTW), F32),
                       jax.ShapeDtypeStruct((m_p, MEM_W), F32)])
            specs = ([_t_spec(A_KVW, spb, TM)] * 3 +
                     [pl.BlockSpec((None, TM // LANES, 2 * A_KVW, LANES), lambda i: (i // spb, i % spb, 0, 0))] +
                     [row_spec(LANES)] * 4 + [row_spec(A_QW), row_spec(A_RESTW), row_spec(MEM_W)])
            outs = _proj(xp, g_pre[l], [w_n, w_t], [], plan, shapes, specs)
            kct, kst, kwt, kvt = outs[0:4]
            slabs, q, rest, qm = outs[4:8], outs[8], outs[9], outs[10]
            kct_l.append(kct)
            kst_l.append(kst)
            kwt_l.append(kwt)
            ckt, cv = _compress_prompt(slabs, pos2, w1bd, w2bd, w2bdt, n_b, s_len)
            q3 = q.reshape(n_b, s_len, A_QW)
            o_cmp, sel, flags = _nsa_cmp(q3, ckt, cv, anymat, n_b, s_len, TQ)
            flags = (flags[:, :, 0:A_KV, 0:nkt] > 0.5).astype(jnp.int32).reshape(-1)
            o_slc, o_win = _nsa_sw(flags, q3, kvt, sel, etile, n_b, s_len, TQ, TK)
            om = _mem_attend(qm, 0, mkvt4, 0, n_b, s_len, min(512, s_len))
            xp = _merge_a(xp, o_cmp.reshape(-1, A_Q), o_slc.reshape(-1, A_Q), o_win.reshape(-1, A_Q),
                          rest, om, w_out, g_post[l], egate)
            w_rows = jnp.concatenate([w[:, A_KC0:A_KC0 + 3 * A_KVW].astype(BF16), w_n[:, A_KVW:]], axis=1)
            kv, q, rest = _proj_rows(xs, g_pre[l], w_rows,
                                     [(0, 3 * A_KVW, F32, 1.0), (3 * A_KVW, A_QW, BF16, QSCALE),
                                      (3 * A_KVW + A_QW, A_RESTW, F32, 1.0)])
            kv_s.append(kv)
            posr = jnp.tile(pos2.transpose(1, 0, 2), (1, page // CMP_LEN, 1))
            ckv_pool = _compress_pool(cmp_t, l, posr, w1bd, w2bd, 32)
            ckv_pool = ckv_pool.reshape(n_pool, page // CMP_LEN, 512)
            o_cmp, o_slc, o_win = _nsa_sample(q, ckv_pool, slc_t, kv, win_t, page_table, l, n_s, t_s)
            om = _mem_attend(rest, 9, mem_t, l, n_s, t_s, t_s)
            xs = _merge_a(xs, o_cmp, o_slc, o_win, rest, om, w_out, g_post[l], egate)
            if l == n_a - 1:
                w_kv = w_kv_b.astype(BF16)
                plan = [(True, 1, 0, 512, 0, (_st_t(0, 0),)),
                        (False, 0, 0, 512, 0, (_st_rows(1, 0),)),
                        (False, 0, 0, 512, 1, (_st_cm(2, 0, B_GROUPS[1][1]),)),
                        (False, 0, 0, 512, 2, (_st_cm(3, 0, B_GROUPS[2][1]),))]
                shapes = [jax.ShapeDtypeStruct((n_b, 512, s_len), F32), jax.ShapeDtypeStruct((m_p, 512), BF16)]
                specs = [_t_spec(512, spb, TM), row_spec(512)]
                for _, dil in B_GROUPS[1:]:
                    shapes.append(jax.ShapeDtypeStruct((m_p // CM_TILE, dil, CM_TILE // dil, 512), BF16))
                    specs.append(_cm_spec(dil, 512))
                kvbt_p, kv0, kv1, kv2 = _proj(xp, g_kv_b, [w_kv, w_kv_b.T.astype(BF16)], perms, plan, shapes, specs)
                kvb_cm = [kv0.reshape(n_b, s_len, 512)] + [
                    a.reshape((n_b, s_len // CM_TILE) + a.shape[1:]) for a in (kv1, kv2)]
                kvb_s = _proj_rows(xs, g_kv_b, w_kv, [(0, 512, F32, 1.0)])[0]
        else:
            lb = l - n_a
            w = w_in_b[lb]
            w_n = jnp.concatenate([_take_cols(w, b_q_cols), w[:, B_Q:].astype(BF16)], axis=1)
            w_out = w_out_b[lb].astype(BF16)
            qw = B_HPG * LANES
            restw = B_O + 2 * MEM_W
            plan, shapes, specs = [], [], []
            for gi, (_, dil) in enumerate(B_GROUPS):
                for a, b in _chunks(gi * qw, (gi + 1) * qw):
                    st = _st_rows(gi, a - gi * qw, QSCALE) if dil == 1 else _st_cm(gi, a - gi * qw, dil, QSCALE)
                    plan.append((False, 0, a, b, gi, (st,)))
                if dil == 1:
                    shapes.append(jax.ShapeDtypeStruct((m_p, qw), BF16))
                    specs.append(row_spec(qw))
                else:
                    shapes.append(jax.ShapeDtypeStruct((m_p // CM_TILE, dil, CM_TILE // dil, qw), BF16))
                    specs.append(_cm_spec(dil, qw))
            for a, b in _chunks(3 * qw, 3 * qw + restw):
                plan.append((False, 0, a, b, 0, (_st_rows(3, a - 3 * qw),)))
            shapes.append(jax.ShapeDtypeStruct((m_p, restw), F32))
            specs.append(row_spec(restw))
            qm0 = 3 * qw + B_O
            plan.append((False, 0, qm0, qm0 + MEM_W, 0, (_st_rows(4, 0),)))
            shapes.append(jax.ShapeDtypeStruct((m_p, MEM_W), F32))
            specs.append(row_spec(MEM_W))
            q0, q1, q2, rest, qm = _proj(xp, g_pre[l], [w_n], perms, plan, shapes, specs)
            qs = [q0.reshape(n_b, s_len, qw)] + [a.reshape((n_b, s_len // CM_TILE) + a.shape[1:]) for a in (q1, q2)]
            outs, lses = [], []
            for gi in range(N_B_GROUPS):
                o, lse = _dilated_prompt(qs[gi], kvb_cm[gi], gi, n_b, s_len)
                if gi == 0:
                    o, lse = o.reshape(m_p, B_O), lse.reshape(m_p, B_O)
                else:
                    o, lse = (a.reshape((m_p // CM_TILE,) + a.shape[2:]) for a in (o, lse))
                outs.append(o)
                lses.append(lse)
            om = _mem_attend(qm, 0, mkvt4, 0, n_b, s_len, min(512, s_len))
            xp = _merge_b(xp, outs, lses, rest, om, w_out, g_post[l], unperms)
            q, rest = _proj_rows(xs, g_pre[l], w_n, [(0, 3 * qw, BF16, QSCALE), (3 * qw, restw, F32, 1.0)])
            outs, lses = _dilated_sample(q, kvb_s, dil_t, n_s, t_s)
            om = _mem_attend(rest, 2, mem_t, l, n_s, t_s, t_s)
            xs = _merge_b(xs, outs, lses, rest, om, w_out, g_post[l], [])

    def from_t(arrs, n, t):
        a = jnp.stack(arrs, 1).reshape(n, len(arrs), 2, A_KV, HEAD_DIM, t)
        return jnp.transpose(a, (0, 5, 1, 2, 3, 4))

    def kv_stack(kvs, n, t, j):
        return jnp.stack([k[:, j * A_KVW:(j + 1) * A_KVW].reshape(n, t, 2, A_KV, HEAD_DIM) for k in kvs], 2)

    new_cmp_p = from_t(kct_l, n_b, s_len)
    new_slc_p = from_t(kst_l, n_b, s_len)
    wa = min(WIN_A, s_len)
    new_win_p = from_t([k[:, :, s_len - wa:] for k in kwt_l], n_b, wa)
    new_cmp_s = kv_stack(kv_s, n_s, t_s, 0)
    new_slc_s = kv_stack(kv_s, n_s, t_s, 1)
    win_s = kv_stack(kv_s, n_s, t_s, 2)
    new_win_s = jnp.concatenate([cache_win_kv, win_s], 1)[:, -wbuf_a:]
    wb = min(B_GROUPS[-1][0], s_len)
    new_dil_p = jnp.transpose(kvbt_p[:, :, s_len - wb:].reshape(n_b, 2, B_KV, HEAD_DIM, wb), (0, 4, 1, 2, 3))
    new_dil_s = jnp.concatenate([cache_dil_kv, kvb_s.reshape(n_s, t_s, 2, B_KV, HEAD_DIM)], 1)[:, -wbuf_b:]
    new_mem_p = jnp.transpose(jnp.stack(mkvt_l, 1).reshape(n_b, depth, 2, MEM_HEADS, HEAD_DIM, mem_len),
                              (0, 5, 1, 2, 3, 4))
    return (xp.reshape(n_b, s_len, d), xs.reshape(n_s, t_s, d), new_cmp_p, new_cmp_s, new_slc_p, new_slc_s,
            new_win_p, new_win_s, new_dil_p, new_dil_s, new_mem_p)
```

```python
import functools
import numpy as np
import jax
import jax.numpy as jnp
from jax import lax
from jax.experimental import pallas as pl
from jax.experimental.pallas import tpu as pltpu

F32 = jnp.float32
BF16 = jnp.bfloat16

HEAD_DIM = 64
LANES = 128
A_HEADS = 12
A_KV = 4
A_REP = A_HEADS // A_KV
CMP_LEN = 32
SLC_LEN = 64
N_SEL = 16
WIN_A = 512
B_GROUPS = ((128, 1), (512, 4), (2048, 16))
N_B_GROUPS = 3
B_KV = 4
B_REP = 2
B_HPG = B_KV * B_REP
MEM_HEADS = 4
EPS = 1e-6
A_Q = A_HEADS * HEAD_DIM
A_KVW = 2 * A_KV * HEAD_DIM
MEM_W = MEM_HEADS * HEAD_DIM
B_Q = N_B_GROUPS * B_HPG * HEAD_DIM
B_O = B_HPG * HEAD_DIM
QSCALE = HEAD_DIM ** -0.5
NEG = -1e30
VMEM_LIMIT = 56 * 1024 * 1024
TM = 256
TQ = 128
TK = 512
CM_TILE = 256


def _alibi(n):
    return [float(2.0 ** (-8.0 * i / n)) for i in range(1, n + 1)]


SLOPES_A = _alibi(A_HEADS)
SLOPES_B = _alibi(B_HPG)


def _cparams(sem):
    return pltpu.CompilerParams(dimension_semantics=sem, vmem_limit_bytes=VMEM_LIMIT)


def _silu(x):
    return x * (1.0 / (1.0 + jnp.exp(-x)))


def _dot_nt(a, b):
    return lax.dot_general(a, b, (((1,), (1,)), ((), ())), preferred_element_type=F32)


def _dot(a, b):
    return jnp.dot(a, b, preferred_element_type=F32)


def _split3(x):
    hi = x.astype(BF16)
    r1 = x - hi.astype(F32)
    mid = r1.astype(BF16)
    lo = (r1 - mid.astype(F32)).astype(BF16)
    return hi, mid, lo


def _pick_half(lane, a, b):
    return jnp.where(lane < HEAD_DIM, a, b)


def _assemble(pieces, rows):
    lane = lax.broadcasted_iota(jnp.int32, (rows, LANES), 1)
    cols = []
    for c in range(len(pieces) // 2):
        halves = []
        for k in (0, 1):
            arr, useful = pieces[2 * c + k]
            if useful != k:
                arr = pltpu.roll(arr, HEAD_DIM, 1)
            halves.append(arr)
        cols.append(_pick_half(lane, halves[0], halves[1]))
    return jnp.concatenate(cols, axis=1)


def _proj_kernel(x_ref, g_ref, *refs, n_w, n_p, plan):
    w_refs = refs[:n_w]
    p_refs = refs[n_w:n_w + n_p]
    out_refs = refs[n_w + n_p:]
    x = x_ref[...]
    ms = jnp.mean(x * x, axis=-1, keepdims=True)
    xn = ((x * lax.rsqrt(ms + EPS)) * g_ref[...]).astype(BF16)
    xs = [xn] + [_dot(p_ref[...], xn).astype(BF16) for p_ref in p_refs]
    for (transposed, wi, c0, c1, xi, stores) in plan:
        if transposed:
            y = _dot_nt(w_refs[wi][c0:c1, :], xs[xi])
        else:
            y = _dot(xs[xi], w_refs[wi][:, c0:c1])
        for store in stores:
            store(out_refs, y)


def _st_rows(oi, o0, scale=1.0):
    def store(out_refs, y):
        v = y if scale == 1.0 else y * scale
        out_refs[oi][:, o0:o0 + y.shape[1]] = v.astype(out_refs[oi].dtype)
    return store


def _st_t(oi, o0):
    def store(out_refs, y):
        out_refs[oi][o0:o0 + y.shape[0], :] = y.astype(out_refs[oi].dtype)
    return store


def _st_ttile(oi, o0):
    def store(out_refs, y):
        for u in range(y.shape[1] // LANES):
            out_refs[oi][u, o0:o0 + y.shape[0], :] = y[:, u * LANES:(u + 1) * LANES].astype(out_refs[oi].dtype)
    return store


def _st_cm(oi, o0, dil, scale=1.0):
    def store(out_refs, y):
        v = y if scale == 1.0 else y * scale
        v = v.astype(out_refs[oi].dtype)
        out_refs[oi][:, :, o0:o0 + y.shape[1]] = v.reshape(dil, y.shape[0] // dil, y.shape[1])
    return store


def _chunks(c0, c1, step=512):
    return [(a, min(a + step, c1)) for a in range(c0, c1, step)]


def _proj(x2d, g, weights, perms, plan, out_shapes, out_specs, tm=TM):
    m, d = x2d.shape
    tm = min(tm, m)
    const2 = lambda i: (0, 0)
    return pl.pallas_call(
        functools.partial(_proj_kernel, n_w=len(weights), n_p=len(perms), plan=tuple(plan)),
        grid=(m // tm,),
        in_specs=[pl.BlockSpec((tm, d), lambda i: (i, 0)), pl.BlockSpec((1, d), const2)] +
                 [pl.BlockSpec(w.shape, const2) for w in weights] +
                 [pl.BlockSpec(p.shape, const2) for p in perms],
        out_specs=out_specs,
        out_shape=out_shapes,
        compiler_params=_cparams(("arbitrary",)),
        name="proj",
    )(x2d, g.reshape(1, d), *weights, *perms)


def _proj_rows(x2d, g, w_bf, outs):
    m = x2d.shape[0]
    tm = min(TM, m)
    plan = []
    for oi, (c0, width, _, scale) in enumerate(outs):
        for a, b in _chunks(c0, c0 + width):
            plan.append((False, 0, a, b, 0, (_st_rows(oi, a - c0, scale),)))
    return _proj(x2d, g, [w_bf], [], plan,
                 [jax.ShapeDtypeStruct((m, o[1]), o[2]) for o in outs],
                 [pl.BlockSpec((tm, o[1]), lambda i: (i, 0)) for o in outs])


def _t_spec(width, spb, tm):
    return pl.BlockSpec((None, width, tm), lambda i: (i // spb, 0, i % spb))


def _cm_spec(dil, width):
    return pl.BlockSpec((None, dil, CM_TILE // dil, width), lambda i: (i, 0, 0, 0))


def _compress_pool_kernel(x_ref, posr_ref, w1_ref, w2_ref, o_ref, s0, s1, s2, s3, *, pages, pitch):
    scr = (s0, s1, s2, s3)
    page = x_ref.shape[2]
    per_page = page // CMP_LEN
    rows = per_page * pages

    def relayout(pg, carry):
        for idx in range(4):
            t = x_ref[pg, idx * LANES:(idx + 1) * LANES, :].T + posr_ref[idx // 2]
            for k in range(page // 8):
                n, l0 = (8 * k) // CMP_LEN, (8 * k) % CMP_LEN
                scr[idx][pl.ds(l0 * pitch + pg * per_page + n, 8, stride=pitch), :] = t[8 * k:8 * k + 8, :]
        return carry

    lax.fori_loop(0, pages, relayout, 0, unroll=4)
    for idx in range(4):
        kv = idx // 2
        acc = jnp.zeros((rows, LANES), F32)
        for l in range(CMP_LEN):
            xl = scr[idx][l * pitch:l * pitch + rows, :].astype(BF16)
            acc = acc + _dot(xl, w1_ref[l, kv])
        h = _silu(acc).astype(BF16)
        o_ref[:, idx * LANES:(idx + 1) * LANES] = _dot(h, w2_ref[kv])


def _compress_pool(cmp_t, layer, posr, w1bd, w2bd, pages):
    n_pool, _, _, page = cmp_t.shape
    per_page = page // CMP_LEN
    pages = max(p for p in range(1, min(pages, n_pool) + 1) if n_pool % p == 0 and (per_page * p) % 16 == 0)
    assert page % CMP_LEN == 0 and page == LANES
    rows = per_page * pages
    pitch = rows + 8
    assert rows % 16 == 0
    return pl.pallas_call(
        functools.partial(_compress_pool_kernel, pages=pages, pitch=pitch),
        grid=(n_pool // pages,),
        in_specs=[pl.BlockSpec((pages, None, 512, page), lambda i: (i, layer, 0, 0)),
                  pl.BlockSpec(posr.shape, lambda i: (0, 0, 0)),
                  pl.BlockSpec(w1bd.shape, lambda i: (0, 0, 0, 0)),
                  pl.BlockSpec(w2bd.shape, lambda i: (0, 0, 0))],
        out_specs=pl.BlockSpec((rows, 512), lambda i: (i, 0)),
        out_shape=jax.ShapeDtypeStruct((n_pool * per_page, 512), F32),
        scratch_shapes=[pltpu.VMEM((CMP_LEN * pitch, LANES), F32)] * 4,
        compiler_params=_cparams(("arbitrary",)),
        name="compress_pool",
    )(cmp_t, posr, w1bd, w2bd)


def _compress_prompt_kernel(s0, s1, s2, s3, pos_ref, w1_ref, w2_ref, w2t_ref, ckt_ref, cv_ref, *, nc):
    half = nc // 2
    slabs = (s0, s1, s2, s3)
    accs = [jnp.zeros((nc, LANES), F32) for _ in range(4)]
    for l in range(CMP_LEN):
        for idx in range(4):
            kv = idx // 2
            xe = slabs[idx][pl.ds(l, half, stride=2 * CMP_LEN), :]
            xo = slabs[idx][pl.ds(CMP_LEN + l, half, stride=2 * CMP_LEN), :]
            xb = (jnp.concatenate([xe, xo], axis=0) + pos_ref[l, kv:kv + 1, :]).astype(BF16)
            accs[idx] = accs[idx] + _dot(xb, w1_ref[l, kv])
    for idx in range(4):
        kv, p = idx // 2, idx % 2
        h = _silu(accs[idx]).astype(BF16)
        if kv == 0:
            ckt_ref[p * LANES:(p + 1) * LANES, :] = _dot_nt(w2t_ref[0], h)
        else:
            cv_ref[:, p * LANES:(p + 1) * LANES] = _dot(h, w2_ref[1])


def _compress_prompt(slabs, pos2, w1bd, w2bd, w2bdt, n_b, s_len):
    nc = s_len // CMP_LEN
    slab_spec = pl.BlockSpec((s_len, LANES), lambda b: (b, 0))
    return pl.pallas_call(
        functools.partial(_compress_prompt_kernel, nc=nc),
        grid=(n_b,),
        in_specs=[slab_spec] * 4 +
                 [pl.BlockSpec(pos2.shape, lambda b: (0, 0, 0)),
                  pl.BlockSpec(w1bd.shape, lambda b: (0, 0, 0, 0)),
                  pl.BlockSpec(w2bd.shape, lambda b: (0, 0, 0)),
                  pl.BlockSpec(w2bdt.shape, lambda b: (0, 0, 0))],
        out_specs=[pl.BlockSpec((None, 256, nc), lambda b: (b, 0, 0)),
                   pl.BlockSpec((None, nc, 256), lambda b: (b, 0, 0))],
        out_shape=[jax.ShapeDtypeStruct((n_b, 256, nc), F32),
                   jax.ShapeDtypeStruct((n_b, nc, 256), F32)],
        compiler_params=_cparams(("arbitrary",)),
        name="compress_prompt",
    )(*slabs, pos2, w1bd, w2bd, w2bdt)


def _topk_mask_t(imp_t, k):
    nb, cols = imp_t.shape
    row = lax.broadcasted_iota(jnp.int32, (nb, cols), 0)
    work = imp_t
    sel = jnp.zeros((nb, cols), F32)
    for _ in range(k):
        m = jnp.max(work, axis=0, keepdims=True)
        idx = jnp.min(jnp.where(work == m, row, nb), axis=0, keepdims=True)
        hit = row == idx
        sel = jnp.where(hit, 1.0, sel)
        work = jnp.where(hit, -3.0, work)
    return sel


def _group_q(q_ref, g):
    return jnp.concatenate(
        [q_ref[:, (g * A_REP + r) * LANES:(g * A_REP + r + 1) * LANES] for r in range(A_REP)], axis=0)


def _nsa_cmp_kernel(q_ref, ckt_ref, cv_ref, any_ref, o_ref, sel_ref, flag_ref, *, tq, nc, ns):
    i = pl.program_id(1)
    q0 = i * tq
    half = nc // 2
    pos = q0 + lax.broadcasted_iota(jnp.int32, (tq, 1), 0)
    posf = pos.astype(F32)
    tok = lax.broadcasted_iota(jnp.int32, (1, nc), 1)
    cidx = jnp.where(tok < half, 2 * tok, 2 * (tok - half) + 1)
    c_end = cidx * CMP_LEN + (CMP_LEN - 1)
    c_mid = cidx.astype(F32) * CMP_LEN + 0.5 * (CMP_LEN - 1)
    cmask = c_end <= pos
    dist = posf - c_mid
    blk = lax.broadcasted_iota(jnp.int32, (1, ns), 1)
    cur = jnp.right_shift(pos, 6)
    forced = (blk == 0) | (blk == cur) | (blk == cur - 1)
    allowed = blk <= cur
    pieces = []
    anys = []
    for p in range(A_KV // 2):
        ckt = ckt_ref[p * LANES:(p + 1) * LANES, :].astype(BF16)
        cv = cv_ref[:, p * LANES:(p + 1) * LANES].astype(BF16)
        qq = jnp.concatenate([_group_q(q_ref, 2 * p), _group_q(q_ref, 2 * p + 1)], axis=0)
        s = _dot(qq, ckt)
        parts = []
        for hh in range(2 * A_REP):
            sr = s[hh * tq:(hh + 1) * tq, :] - SLOPES_A[2 * p * A_REP + hh] * dist
            parts.append(jnp.where(cmask, sr, -jnp.inf))
        sm = jnp.concatenate(parts, axis=0)
        m = jnp.max(sm, axis=-1, keepdims=True)
        m = jnp.where(m == -jnp.inf, 0.0, m)
        e = jnp.exp(sm - m)
        den = jnp.sum(e, axis=-1, keepdims=True)
        pc = e / jnp.where(den > 0, den, 1.0)
        og = _dot(pc.astype(BF16), cv)
        imps = []
        for k in range(2):
            for r in range(A_REP):
                pieces.append((og[(k * A_REP + r) * tq:(k * A_REP + r + 1) * tq, :], k))
            ps = sum(pc[(k * A_REP + r) * tq:(k * A_REP + r + 1) * tq] for r in range(A_REP))
            imp = ps[:, 0:half] + ps[:, half:nc]
            imp = jnp.where(allowed, imp, -1.0)
            imps.append(jnp.where(forced, -3.0, imp).T)
        sel_t = _topk_mask_t(jnp.concatenate(imps, axis=1), max(min(N_SEL, ns) - 3, 0))
        for k in range(2):
            g = 2 * p + k
            sel = jnp.where(forced, 1.0, sel_t[:, k * tq:(k + 1) * tq].T)
            sel_ref[:, g * ns:(g + 1) * ns] = sel.astype(BF16)
            anys.append(jnp.max(sel, axis=0, keepdims=True))
    o_ref[...] = _assemble(pieces, tq)
    rows = jnp.concatenate(anys + [jnp.zeros((8 - A_KV, ns), F32)], axis=0).astype(BF16)
    flag_ref[...] = _dot(rows, any_ref[...])


def _nsa_cmp(q_exp, ckt, cv, anymat, n_b, s_len, tq):
    nc = ckt.shape[2]
    ns = s_len // SLC_LEN
    assert nc == 2 * ns
    return pl.pallas_call(
        functools.partial(_nsa_cmp_kernel, tq=tq, nc=nc, ns=ns),
        grid=(n_b, s_len // tq),
        in_specs=[pl.BlockSpec((None, tq, A_HEADS * LANES), lambda b, i: (b, i, 0)),
                  pl.BlockSpec((None, 256, nc), lambda b, i: (b, 0, 0)),
                  pl.BlockSpec((None, nc, 256), lambda b, i: (b, 0, 0)),
                  pl.BlockSpec(anymat.shape, lambda b, i: (0, 0))],
        out_specs=[pl.BlockSpec((None, tq, A_Q), lambda b, i: (b, i, 0)),
                   pl.BlockSpec((None, tq, A_KV * ns), lambda b, i: (b, i, 0)),
                   pl.BlockSpec((None, None, 8, LANES), lambda b, i: (b, i, 0, 0))],
        out_shape=[jax.ShapeDtypeStruct((n_b, s_len, A_Q), F32),
                   jax.ShapeDtypeStruct((n_b, s_len, A_KV * ns), BF16),
                   jax.ShapeDtypeStruct((n_b, s_len // tq, 8, LANES), F32)],
        compiler_params=_cparams(("arbitrary", "arbitrary")),
        name="nsa_cmp",
    )(q_exp, ckt, cv, anymat)


def _nsa_sw_kernel(fl_ref, q_ref, kv_ref, sel_ref, e_ref, oslc_ref, owin_ref,
                   m_ref, l_ref, acc_ref, *, tq, tk, ns, nq, nkt):
    b = pl.program_id(0)
    i = pl.program_id(1)
    q0 = i * tq
    per = tk // LANES
    trow = lax.broadcasted_iota(jnp.int32, (tq, 1), 0)
    n_pair = A_KV // 2
    hpp = 2 * A_REP
    qp = [jnp.concatenate([_group_q(q_ref, 2 * p), _group_q(q_ref, 2 * p + 1)], axis=0) for p in range(n_pair)]

    m_ref[...] = jnp.full(m_ref.shape, NEG, F32)
    l_ref[...] = jnp.zeros(l_ref.shape, F32)
    acc_ref[...] = jnp.zeros(acc_ref.shape, F32)

    def tiles(j, r0):
        return jnp.concatenate([kv_ref[j * per + u, r0:r0 + LANES, :] for u in range(per)], axis=1)

    def body(j, carry):
        k0 = j * tk
        krel = (k0 - q0) + lax.broadcasted_iota(jnp.int32, (1, tk), 1)
        causal = krel <= trow
        krelf = krel.astype(F32)
        for p in range(n_pair):
            f0 = fl_ref[((b * nq + i) * A_KV + 2 * p) * nkt + j]
            f1 = fl_ref[((b * nq + i) * A_KV + 2 * p + 1) * nkt + j]

            @pl.when((f0 | f1) != 0)
            def _(p=p):
                s = _dot(qp[p], tiles(j, p * LANES))
                selk = _dot(jnp.concatenate([sel_ref[:, (2 * p + k) * ns:(2 * p + k + 1) * ns] for k in range(2)],
                                            axis=0), e_ref[j])
                valid = [(selk[k * tq:(k + 1) * tq, :] > 0.5) & causal for k in range(2)]
                parts = []
                for hh in range(hpp):
                    sr = s[hh * tq:(hh + 1) * tq, :] + SLOPES_A[2 * p * A_REP + hh] * krelf
                    parts.append(jnp.where(valid[hh // A_REP], sr, NEG))
                sm = jnp.concatenate(parts, axis=0)
                m_old = m_ref[p]
                m_new = jnp.maximum(m_old, jnp.max(sm, axis=-1, keepdims=True))
                pe = jnp.exp(sm - m_new)
                alpha = jnp.exp(m_old - m_new)
                l_ref[p] = alpha * l_ref[p] + jnp.sum(pe, axis=-1, keepdims=True)
                acc_ref[p] = alpha * acc_ref[p] + _dot_nt(pe.astype(BF16), tiles(j, 256 + p * LANES))
                m_ref[p] = m_new
        return carry

    lax.fori_loop(0, (q0 + tq + tk - 1) // tk, body, 0)
    pieces = []
    for p in range(n_pair):
        og = acc_ref[p] / l_ref[p]
        for hh in range(hpp):
            pieces.append((og[hh * tq:(hh + 1) * tq, :], hh // A_REP))
    oslc_ref[...] = _assemble(pieces, tq)

    nwt = WIN_A // LANES + tq // LANES
    t0 = jnp.maximum(i * (tq // LANES) - WIN_A // LANES, 0)
    nkw = nwt * LANES
    wrel = (t0 * LANES - q0) + lax.broadcasted_iota(jnp.int32, (1, nkw), 1)
    dist = trow - wrel
    wvalid = (dist >= 0) & (dist <= WIN_A)
    wrelf = wrel.astype(F32)
    pieces = []
    for p in range(n_pair):
        kk = jnp.concatenate([kv_ref[t0 + u, 512 + p * LANES:512 + (p + 1) * LANES, :] for u in range(nwt)], axis=1)
        vv = jnp.concatenate([kv_ref[t0 + u, 768 + p * LANES:768 + (p + 1) * LANES, :] for u in range(nwt)], axis=1)
        s = _dot(qp[p], kk)
        parts = []
        for hh in range(hpp):
            sr = s[hh * tq:(hh + 1) * tq, :] + SLOPES_A[2 * p * A_REP + hh] * wrelf
            parts.append(jnp.where(wvalid, sr, NEG))
        sm = jnp.concatenate(parts, axis=0)
        m = jnp.max(sm, axis=-1, keepdims=True)
        pe = jnp.exp(sm - m)
        den = jnp.sum(pe, axis=-1, keepdims=True)
        og = _dot_nt(pe.astype(BF16), vv) / den
        for hh in range(hpp):
            pieces.append((og[hh * tq:(hh + 1) * tq, :], hh // A_REP))
    owin_ref[...] = _assemble(pieces, tq)


def _nsa_sw(flags, q_exp, kvt, sel, etile, n_b, s_len, tq, tk):
    ns = s_len // SLC_LEN
    nq = s_len // tq
    nkt = s_len // tk
    assert s_len >= WIN_A + tq and tk % tq == 0 and tq % LANES == 0
    grid_spec = pltpu.PrefetchScalarGridSpec(
        num_scalar_prefetch=1,
        grid=(n_b, nq),
        in_specs=[pl.BlockSpec((None, tq, A_HEADS * LANES), lambda b, i, fl: (b, i, 0)),
                  pl.BlockSpec((None, s_len // LANES, 1024, LANES), lambda b, i, fl: (b, 0, 0, 0)),
                  pl.BlockSpec((None, tq, A_KV * ns), lambda b, i, fl: (b, i, 0)),
                  pl.BlockSpec(etile.shape, lambda b, i, fl: (0, 0, 0))],
        out_specs=[pl.BlockSpec((None, tq, A_Q), lambda b, i, fl: (b, i, 0)),
                   pl.BlockSpec((None, tq, A_Q), lambda b, i, fl: (b, i, 0))],
        scratch_shapes=[pltpu.VMEM((A_KV // 2, 2 * A_REP * tq, 1), F32),
                        pltpu.VMEM((A_KV // 2, 2 * A_REP * tq, 1), F32),
                        pltpu.VMEM((A_KV // 2, 2 * A_REP * tq, LANES), F32)],
    )
    return pl.pallas_call(
        functools.partial(_nsa_sw_kernel, tq=tq, tk=tk, ns=ns, nq=nq, nkt=nkt),
        grid_spec=grid_spec,
        out_shape=[jax.ShapeDtypeStruct((n_b, s_len, A_Q), F32),
                   jax.ShapeDtypeStruct((n_b, s_len, A_Q), F32)],
        compiler_params=_cparams(("arbitrary", "arbitrary")),
        name="nsa_sw",
    )(flags, q_exp, kvt, sel, etile)


def _mem_kernel(qm_ref, mkv_ref, o_ref, *, tt, sb):
    lane = lax.broadcasted_iota(jnp.int32, (tt, LANES), 1)
    for u in range(sb):
        cols = []
        for c in range(MEM_HEADS // 2):
            qc = qm_ref[u * tt:(u + 1) * tt, c * LANES:(c + 1) * LANES]
            mkt = mkv_ref[u, c * LANES:(c + 1) * LANES, :].astype(BF16)
            mvt = mkv_ref[u, MEM_W + c * LANES:MEM_W + (c + 1) * LANES, :].astype(BF16)
            halves = []
            for k in (0, 1):
                qh = jnp.where((lane < HEAD_DIM) == (k == 0), qc, 0.0).astype(BF16)
                s = _dot(qh, mkt) * QSCALE
                m = jnp.max(s, axis=-1, keepdims=True)
                e = jnp.exp(s - m)
                den = jnp.sum(e, axis=-1, keepdims=True)
                halves.append(_dot_nt(e.astype(BF16), mvt) / den)
            cols.append(_pick_half(lane, halves[0], halves[1]))
        o_ref[u * tt:(u + 1) * tt, :] = jnp.concatenate(cols, axis=1)


def _mem_attend(rest, qm_blk, mkvt4, layer, n, t, tt, sb=1):
    mem = mkvt4.shape[3]
    steps = t // tt
    assert n % sb == 0 and (sb == 1 or steps == 1)
    return pl.pallas_call(
        functools.partial(_mem_kernel, tt=tt, sb=sb),
        grid=(n // sb, steps),
        in_specs=[pl.BlockSpec((sb * tt, MEM_W), lambda b, i: (b * steps + i, qm_blk)),
                  pl.BlockSpec((sb, None, 2 * MEM_W, mem), lambda b, i: (b, layer, 0, 0))],
        out_specs=pl.BlockSpec((sb * tt, MEM_W), lambda b, i: (b * steps + i, 0)),
        out_shape=jax.ShapeDtypeStruct((n * t, MEM_W), F32),
        compiler_params=_cparams(("arbitrary", "arbitrary")),
        name="mem_attend",
    )(rest, mkvt4)


def _post(x_ref, y, gp_ref, out_ref):
    ms = jnp.mean(y * y, axis=-1, keepdims=True)
    out_ref[...] = x_ref[...] + (y * lax.rsqrt(ms + EPS)) * gp_ref[...]


def _merge_a_kernel(x_ref, oc_ref, os_ref, ow_ref, z_ref, zm_ref, gate_ref, om_ref,
                    w_ref, gp_ref, eg_ref, out_ref):
    gs = 1.0 / (1.0 + jnp.exp(-gate_ref[...]))
    eg = eg_ref[...]
    gexp = sum(_dot(t, eg) for t in _split3(gs))
    o = jnp.zeros(oc_ref.shape, F32)
    for b, ob_ref in enumerate((oc_ref, os_ref, ow_ref)):
        o = o + (ob_ref[...] * _silu(z_ref[:, b * A_Q:(b + 1) * A_Q])) * gexp[:, b * A_Q:(b + 1) * A_Q]
    om = om_ref[...] * _silu(zm_ref[...])
    y = _dot(o.astype(BF16), w_ref[0:A_Q, :]) + _dot(om.astype(BF16), w_ref[A_Q:A_Q + MEM_W, :])
    _post(x_ref, y, gp_ref, out_ref)


def _merge_a(x2d, o_cmp, o_slc, o_win, rest, om, w_bf, g_post, egate, tm=TM):
    m, d = x2d.shape
    tm = min(tm, m)
    row = lambda i: (i, 0)
    return pl.pallas_call(
        _merge_a_kernel,
        grid=(m // tm,),
        in_specs=[pl.BlockSpec((tm, d), row),
                  pl.BlockSpec((tm, A_Q), row), pl.BlockSpec((tm, A_Q), row), pl.BlockSpec((tm, A_Q), row),
                  pl.BlockSpec((tm, 3 * A_Q), row),
                  pl.BlockSpec((tm, MEM_W), lambda i: (i, 10)),
                  pl.BlockSpec((tm, LANES), lambda i: (i, 22)),
                  pl.BlockSpec((tm, MEM_W), row),
                  pl.BlockSpec((A_Q + MEM_W, d), lambda i: (0, 0)),
                  pl.BlockSpec((1, d), lambda i: (0, 0)),
                  pl.BlockSpec((LANES, 3 * A_Q), lambda i: (0, 0))],
        out_specs=pl.BlockSpec((tm, d), row),
        out_shape=jax.ShapeDtypeStruct((m, d), F32),
        compiler_params=_cparams(("arbitrary",)),
        name="merge_a",
    )(x2d, o_cmp, o_slc, o_win, rest, rest, rest, om, w_bf, g_post.reshape(1, d), egate)


def _merge_b_kernel(x_ref, o0_ref, o1_ref, o2_ref, l0_ref, l1_ref, l2_ref, z_ref, zm_ref, om_ref,
                    w_ref, gp_ref, *rest, n_u):
    u_refs = rest[:n_u]
    out_ref = rest[n_u]

    def nat(ref, ui):
        v = ref[...]
        if v.ndim == 2:
            return v
        v = v.reshape(v.shape[0] * v.shape[1], v.shape[2])
        u = u_refs[ui][...]
        return sum(_dot(u, t) for t in _split3(v))

    o0, o1, o2 = nat(o0_ref, 0), nat(o1_ref, 0), nat(o2_ref, 1)
    l0, l1, l2 = nat(l0_ref, 0), nat(l1_ref, 0), nat(l2_ref, 1)
    mx = jnp.maximum(jnp.maximum(l0, l1), l2)
    e0, e1, e2 = jnp.exp(l0 - mx), jnp.exp(l1 - mx), jnp.exp(l2 - mx)
    den = e0 + e1 + e2
    o = (e0 / den) * o0 + (e1 / den) * o1 + (e2 / den) * o2
    o = o * _silu(z_ref[...])
    om = om_ref[...] * _silu(zm_ref[...])
    y = _dot(o.astype(BF16), w_ref[0:B_O, :]) + _dot(om.astype(BF16), w_ref[B_O:B_O + MEM_W, :])
    _post(x_ref, y, gp_ref, out_ref)


def _merge_b(x2d, outs, lses, rest, om, w_bf, g_post, unperms, tm=TM):
    m, d = x2d.shape
    tm = min(tm, m)
    row = lambda i: (i, 0)

    def spec(a):
        if a.ndim == 2:
            return pl.BlockSpec((tm, B_O), row)
        assert tm == CM_TILE
        return pl.BlockSpec((None,) + a.shape[1:], lambda i: (i, 0, 0, 0))

    return pl.pallas_call(
        functools.partial(_merge_b_kernel, n_u=len(unperms)),
        grid=(m // tm,),
        in_specs=[pl.BlockSpec((tm, d), row)] + [spec(a) for a in outs] + [spec(a) for a in lses] +
                 [pl.BlockSpec((tm, B_O), row),
                  pl.BlockSpec((tm, MEM_W), lambda i: (i, 3)),
                  pl.BlockSpec((tm, MEM_W), row),
                  pl.BlockSpec((B_O + MEM_W, d), lambda i: (0, 0)),
                  pl.BlockSpec((1, d), lambda i: (0, 0))] +
                 [pl.BlockSpec(u.shape, lambda i: (0, 0)) for u in unperms],
        out_specs=pl.BlockSpec((tm, d), row),
        out_shape=jax.ShapeDtypeStruct((m, d), F32),
        compiler_params=_cparams(("arbitrary",)),
        name="merge_b",
    )(x2d, *outs, *lses, rest, rest, om, w_bf, g_post.reshape(1, d), *unperms)


def _dil_kernel(q_ref, kc_ref, kp_ref, o_ref, lse_ref, *, tn, wr):
    i = pl.program_id(2)
    nk = wr + tn

    def rows(ref):
        v = ref[...]
        return v if v.ndim == 2 else v.reshape(v.shape[0] * v.shape[1], v.shape[2])

    q, kc, kp = rows(q_ref), rows(kc_ref), rows(kp_ref)
    t = lax.broadcasted_iota(jnp.int32, (tn, 1), 0)
    k = lax.broadcasted_iota(jnp.int32, (1, nk), 1)
    dist = t - k + wr
    valid = (dist >= 0) & (dist <= wr) & ((k >= wr) | (i > 0))
    distf = dist.astype(F32)
    lane = lax.broadcasted_iota(jnp.int32, (tn, LANES), 1)
    o_cols, l_cols = [], []
    hpp = 2 * B_REP
    for p in range(B_KV // 2):
        kk = jnp.concatenate([kp[:, p * LANES:(p + 1) * LANES], kc[:, p * LANES:(p + 1) * LANES]], axis=0)
        vv = jnp.concatenate([kp[:, 256 + p * LANES:256 + (p + 1) * LANES],
                              kc[:, 256 + p * LANES:256 + (p + 1) * LANES]], axis=0)
        qh = jnp.concatenate([q[:, (hpp * p + hh) * LANES:(hpp * p + hh + 1) * LANES]
                              for hh in range(hpp)], axis=0)
        s = _dot_nt(qh, kk)
        parts = []
        for hh in range(hpp):
            sj = s[hh * tn:(hh + 1) * tn, :] - SLOPES_B[hpp * p + hh] * distf
            parts.append(jnp.where(valid, sj, NEG))
        sm = jnp.concatenate(parts, axis=0)
        m = jnp.max(sm, axis=-1, keepdims=True)
        e = jnp.exp(sm - m)
        den = jnp.sum(e, axis=-1, keepdims=True)
        og = _dot(e.astype(BF16), vv) / den
        lse = m + jnp.log(den)
        for k in range(2):
            halves, lhalves = [], []
            for j in range(B_REP):
                r0 = (k * B_REP + j) * tn
                piece = og[r0:r0 + tn, :]
                if k != j:
                    piece = pltpu.roll(piece, HEAD_DIM, 1)
                halves.append(piece)
                lhalves.append(jnp.broadcast_to(lse[r0:r0 + tn], (tn, LANES)))
            o_cols.append(_pick_half(lane, halves[0], halves[1]))
            l_cols.append(_pick_half(lane, lhalves[0], lhalves[1]))
    o = jnp.concatenate(o_cols, axis=1)
    l = jnp.concatenate(l_cols, axis=1)
    o_ref[...] = o.reshape(o_ref.shape)
    lse_ref[...] = l.reshape(lse_ref.shape)


def _dilated_prompt(q, kv, gi, n_b, s_len):
    win, dil = B_GROUPS[gi]
    wr = win // dil
    n = s_len // dil
    tn = min(CM_TILE, n)
    assert n % tn == 0 and tn % wr == 0
    ratio = tn // wr
    qw = B_HPG * LANES
    if dil == 1:
        in_specs = [pl.BlockSpec((None, tn, qw), lambda b, c, i: (b, i, 0)),
                    pl.BlockSpec((None, tn, 512), lambda b, c, i: (b, i, 0)),
                    pl.BlockSpec((None, wr, 512), lambda b, c, i: (b, jnp.maximum(i * ratio - 1, 0), 0))]
        out_spec = pl.BlockSpec((None, tn, B_O), lambda b, c, i: (b, i, 0))
        out_shape = jax.ShapeDtypeStruct((n_b, s_len, B_O), F32)
    else:
        rpt = CM_TILE // dil
        nt, npv = tn // rpt, wr // rpt
        in_specs = [pl.BlockSpec((None, nt, None, rpt, qw), lambda b, c, i: (b, i, c, 0, 0)),
                    pl.BlockSpec((None, nt, None, rpt, 512), lambda b, c, i: (b, i, c, 0, 0)),
                    pl.BlockSpec((None, npv, None, rpt, 512),
                                 lambda b, c, i: (b, jnp.maximum(i * ratio - 1, 0), c, 0, 0))]
        out_spec = pl.BlockSpec((None, nt, None, rpt, B_O), lambda b, c, i: (b, i, c, 0, 0))
        out_shape = jax.ShapeDtypeStruct((n_b, s_len // CM_TILE, dil, rpt, B_O), F32)
    return pl.pallas_call(
        functools.partial(_dil_kernel, tn=tn, wr=wr),
        grid=(n_b, dil, n // tn),
        in_specs=in_specs,
        out_specs=[out_spec, out_spec],
        out_shape=[out_shape, out_shape],
        compiler_params=_cparams(("arbitrary", "arbitrary", "arbitrary")),
        name="dilated_prompt",
    )(q, kv, kv)


def _dil_sample_kernel(q_ref, kn_ref, cache_ref, o_ref, lse_ref, *, t_new, wb):
    rows = B_KV * B_REP * t_new
    rid = lax.broadcasted_iota(jnp.int32, (rows, 1), 0)
    pos = wb + (rid & (t_new - 1))
    lane = lax.broadcasted_iota(jnp.int32, (t_new, LANES), 1)
    zero = jnp.zeros((t_new, LANES), BF16)
    new = jnp.concatenate([kn_ref[...], jnp.zeros((LANES - t_new, 512), F32)], axis=0).astype(BF16)
    nidx = wb + lax.broadcasted_iota(jnp.int32, (1, LANES), 1)
    dn = pos - nidx
    for gi, (win, dil) in enumerate(B_GROUPS):
        lo = wb - min(win, wb)
        nk = wb - lo
        blocks = []
        for h in range(B_KV):
            for j in range(B_REP):
                c = gi * B_HPG + h * B_REP + j
                slot = q_ref[:, c * LANES:(c + 1) * LANES]
                blocks.append(jnp.concatenate([slot, zero] if h < 2 else [zero, slot], axis=1))
        qbd = jnp.concatenate(blocks, axis=0)
        kt = cache_ref[0:256, lo:wb].astype(BF16)
        vt = cache_ref[256:512, lo:wb].astype(BF16)
        s_c = _dot(qbd, kt)
        s_n = _dot_nt(qbd, new[:, 0:256])
        kidx = lo + lax.broadcasted_iota(jnp.int32, (1, nk), 1)
        dc = pos - kidx
        vc = (dc <= win) & ((dc & (dil - 1)) == 0)
        vn = (dn >= 0) & ((dn & (dil - 1)) == 0)
        dcf, dnf = dc.astype(F32), dn.astype(F32)
        pc, pn = [], []
        for hj in range(B_HPG):
            r = slice(hj * t_new, (hj + 1) * t_new)
            sl = SLOPES_B[hj] / dil
            pc.append(jnp.where(vc[r], s_c[r] - sl * dcf[r], NEG))
            pn.append(jnp.where(vn[r], s_n[r] - sl * dnf[r], NEG))
        sc, sn = jnp.concatenate(pc, axis=0), jnp.concatenate(pn, axis=0)
        m = jnp.maximum(jnp.max(sc, axis=-1, keepdims=True), jnp.max(sn, axis=-1, keepdims=True))
        ec, en = jnp.exp(sc - m), jnp.exp(sn - m)
        den = jnp.sum(ec, axis=-1, keepdims=True) + jnp.sum(en, axis=-1, keepdims=True)
        ow = (_dot_nt(ec.astype(BF16), vt) + _dot(en.astype(BF16), new[:, 256:512])) / den
        lse = m + jnp.log(den)
        for h in range(B_KV):
            halves, lhalves = [], []
            for j in range(B_REP):
                r0 = (h * B_REP + j) * t_new
                piece = ow[r0:r0 + t_new, (h // 2) * LANES:(h // 2 + 1) * LANES]
                if (h % 2) != j:
                    piece = pltpu.roll(piece, HEAD_DIM, 1)
                halves.append(piece)
                lhalves.append(jnp.broadcast_to(lse[r0:r0 + t_new], (t_new, LANES)))
            c0 = gi * B_O + h * LANES
            o_ref[:, c0:c0 + LANES] = _pick_half(lane, halves[0], halves[1])
            lse_ref[:, c0:c0 + LANES] = _pick_half(lane, lhalves[0], lhalves[1])


def _dilated_sample(q_exp, kv_new, cache_t, n, t_new):
    wb = cache_t.shape[2]
    assert t_new & (t_new - 1) == 0 and t_new <= LANES and all(w <= wb for w, _ in B_GROUPS)
    o, lse = pl.pallas_call(
        functools.partial(_dil_sample_kernel, t_new=t_new, wb=wb),
        grid=(n,),
        in_specs=[pl.BlockSpec((t_new, N_B_GROUPS * B_HPG * LANES), lambda s: (s, 0)),
                  pl.BlockSpec((t_new, 512), lambda s: (s, 0)),
                  pl.BlockSpec((None, 512, wb), lambda s: (s, 0, 0))],
        out_specs=[pl.BlockSpec((t_new, N_B_GROUPS * B_O), lambda s: (s, 0)),
                   pl.BlockSpec((t_new, N_B_GROUPS * B_O), lambda s: (s, 0))],
        out_shape=[jax.ShapeDtypeStruct((n * t_new, N_B_GROUPS * B_O), F32),
                   jax.ShapeDtypeStruct((n * t_new, N_B_GROUPS * B_O), F32)],
        compiler_params=_cparams(("arbitrary",)),
        name="dilated_sample",
    )(q_exp, kv_new, cache_t)
    return ([o[:, g * B_O:(g + 1) * B_O] for g in range(N_B_GROUPS)],
            [lse[:, g * B_O:(g + 1) * B_O] for g in range(N_B_GROUPS)])


def _nsa_sample_kernel(pt_ref, q_ref, *refs, n_pages, t_new, past, wbuf, ns, nslot):
    ck_refs = refs[0:n_pages]
    sp_refs = refs[n_pages:2 * n_pages]
    kvn_ref, win_ref, es_ref = refs[2 * n_pages:2 * n_pages + 3]
    oc_ref, os_ref, ow_ref = refs[2 * n_pages + 3:2 * n_pages + 6]
    ck_s, kall = refs[2 * n_pages + 6:]
    del pt_ref
    page = sp_refs[0].shape[1]
    hs = nslot // 2
    per_page = page // CMP_LEN
    ck_s[...] = jnp.zeros(ck_s.shape, F32)
    for p in range(n_pages):
        for n in range(per_page):
            c = p * per_page + n
            slot = (c % 2) * hs + c // 2
            ck_s[slot:slot + 1, :] = ck_refs[p][n:n + 1, :]
    for p in range(n_pages):
        kall[:, p * page:(p + 1) * page] = sp_refs[p][...].astype(BF16)
    new = jnp.concatenate([kvn_ref[...], jnp.zeros((LANES - t_new, 3 * A_KVW), F32)], axis=0).astype(BF16)

    rows = A_HEADS * t_new
    zero = jnp.zeros((t_new, LANES), BF16)
    blocks = []
    for g in range(A_KV):
        for r in range(A_REP):
            h = g * A_REP + r
            slot = q_ref[:, h * LANES:(h + 1) * LANES]
            blocks.append(jnp.concatenate([slot, zero] if g < 2 else [zero, slot], axis=1))
    qbd = jnp.concatenate(blocks, axis=0)
    rid = lax.broadcasted_iota(jnp.int32, (rows, 1), 0)
    pos = past + (rid & (t_new - 1))
    posf = pos.astype(F32)
    npos = past + lax.broadcasted_iota(jnp.int32, (1, LANES), 1)
    nd = pos - npos
    ndf = nd.astype(F32)

    def hs_(h):
        return slice(h * t_new, (h + 1) * t_new)

    def head_rows(fn):
        return jnp.concatenate([fn(h) for h in range(A_HEADS)], axis=0)

    def finish(ow):
        pieces = []
        for g in range(A_KV):
            for r in range(A_REP):
                r0 = (g * A_REP + r) * t_new
                pieces.append((ow[r0:r0 + t_new, (g // 2) * LANES:(g // 2 + 1) * LANES], g % 2))
        return _assemble(pieces, t_new)

    n_c = n_pages * per_page
    slot_i = lax.broadcasted_iota(jnp.int32, (1, nslot), 1)
    sl_lo = slot_i & (hs - 1)
    cidx = 2 * sl_lo + jnp.where(slot_i >= hs, 1, 0)
    svalid = (sl_lo < n_c // 2) & (cidx * CMP_LEN + (CMP_LEN - 1) <= pos)
    dist = posf - (cidx.astype(F32) * CMP_LEN + 0.5 * (CMP_LEN - 1))
    s = _dot_nt(qbd, ck_s[:, 0:256].astype(BF16))
    sm = head_rows(lambda h: jnp.where(svalid[hs_(h)], s[hs_(h)] - SLOPES_A[h] * dist[hs_(h)], -jnp.inf))
    m = jnp.max(sm, axis=-1, keepdims=True)
    m = jnp.where(m == -jnp.inf, 0.0, m)
    e = jnp.exp(sm - m)
    den = jnp.sum(e, axis=-1, keepdims=True)
    pc = e / jnp.where(den > 0, den, 1.0)
    oc_ref[...] = finish(_dot(pc.astype(BF16), ck_s[:, 256:512].astype(BF16)))
    g_rows = A_KV * t_new
    ps = jnp.concatenate(
        [sum(pc[hs_(g * A_REP + r)] for r in range(A_REP)) for g in range(A_KV)], axis=0)
    imp = ps + pltpu.roll(ps, hs, 1)
    blk = lax.broadcasted_iota(jnp.int32, (1, nslot), 1)
    gid = lax.broadcasted_iota(jnp.int32, (g_rows, 1), 0)
    cur = jnp.right_shift(past + (gid & (t_new - 1)), 6)
    imp = jnp.where(blk < n_c // 2, imp, 0.0)
    imp = jnp.where((blk == 0) | (blk == cur) | (blk == cur - 1), A_REP + 1.0, imp)
    imp = jnp.where(blk <= cur, imp, -1.0)
    imp = jnp.where(blk < ns, imp, -3.0)
    cnt = jnp.zeros((g_rows, nslot), F32)
    for j in range(ns):
        col = imp[:, j:j + 1]
        ahead = (col > imp) | ((col == imp) & (blk > j))
        cnt = cnt + jnp.where(ahead, 1.0, 0.0)
    sel = jnp.where((cnt < min(N_SEL, ns)) & (blk < ns), 1.0, 0.0).astype(BF16)
    selk = _dot(sel, es_ref[...])
    kpos = lax.broadcasted_iota(jnp.int32, (1, past), 1)
    kposf = kpos.astype(F32)
    s_c = _dot(qbd, kall[0:256, :])
    s_n = _dot_nt(qbd, new[:, 512:768])

    def sel_c(h):
        g = h // A_REP
        return jnp.where(selk[g * t_new:(g + 1) * t_new, 0:past] > 0.5,
                         s_c[hs_(h)] - SLOPES_A[h] * (posf[hs_(h)] - kposf), NEG)

    def sel_n(h):
        g = h // A_REP
        ok = (selk[g * t_new:(g + 1) * t_new, past:past + LANES] > 0.5) & (nd[hs_(h)] >= 0)
        return jnp.where(ok, s_n[hs_(h)] - SLOPES_A[h] * ndf[hs_(h)], NEG)

    sc, sn = head_rows(sel_c), head_rows(sel_n)
    m = jnp.maximum(jnp.max(sc, axis=-1, keepdims=True), jnp.max(sn, axis=-1, keepdims=True))
    ec, en = jnp.exp(sc - m), jnp.exp(sn - m)
    den = jnp.sum(ec, axis=-1, keepdims=True) + jnp.sum(en, axis=-1, keepdims=True)
    os_ref[...] = finish((_dot_nt(ec.astype(BF16), kall[256:512, :]) + _dot(en.astype(BF16), new[:, 768:1024])) / den)
    wpos = (past - wbuf) + lax.broadcasted_iota(jnp.int32, (1, wbuf), 1)
    wd = pos - wpos
    wvalid = (wd <= WIN_A) & (wpos >= 0)
    wdf = wd.astype(F32)
    s_c = _dot(qbd, win_ref[0:256, :].astype(BF16))
    s_n = _dot_nt(qbd, new[:, 1024:1280])
    sc = head_rows(lambda h: jnp.where(wvalid[hs_(h)], s_c[hs_(h)] - SLOPES_A[h] * wdf[hs_(h)], NEG))
    sn = head_rows(lambda h: jnp.where(nd[hs_(h)] >= 0, s_n[hs_(h)] - SLOPES_A[h] * ndf[hs_(h)], NEG))
    m = jnp.maximum(jnp.max(sc, axis=-1, keepdims=True), jnp.max(sn, axis=-1, keepdims=True))
    ec, en = jnp.exp(sc - m), jnp.exp(sn - m)
    den = jnp.sum(ec, axis=-1, keepdims=True) + jnp.sum(en, axis=-1, keepdims=True)
    ow_ref[...] = finish((_dot_nt(ec.astype(BF16), win_ref[256:512, :].astype(BF16)) +
                          _dot(en.astype(BF16), new[:, 1280:1536])) / den)


def _nsa_sample(q_exp, ckv_pool, slc_t, kv_new, win_t, page_table, layer, n, t_new):
    n_pages = page_table.shape[1]
    page = slc_t.shape[3]
    past = n_pages * page
    wbuf = win_t.shape[3]
    per_page = page // CMP_LEN
    tk = past + t_new
    ns = -(-tk // SLC_LEN)
    assert past % CMP_LEN == 0 and t_new < CMP_LEN and t_new & (t_new - 1) == 0 and past >= wbuf
    assert wbuf >= WIN_A and t_new <= SLC_LEN
    hs = HEAD_DIM
    while hs < max(n_pages * per_page // 2, ns):
        hs *= 2
    nslot = 2 * hs
    es = (np.arange(nslot)[:, None] == (np.arange(past + LANES)[None, :] // SLC_LEN)).astype(np.float32)
    es[:, past + t_new:] = 0.0
    es = jnp.asarray(es, BF16)
    ck_specs = [pl.BlockSpec((None, per_page, 512), functools.partial(lambda s, pt, p: (pt[s, p], 0, 0), p=p))
                for p in range(n_pages)]
    sp_specs = [pl.BlockSpec((None, None, 512, page), functools.partial(lambda s, pt, p: (pt[s, p], layer, 0, 0), p=p))
                for p in range(n_pages)]
    grid_spec = pltpu.PrefetchScalarGridSpec(
        num_scalar_prefetch=1,
        grid=(n,),
        in_specs=[pl.BlockSpec((t_new, A_HEADS * LANES), lambda s, pt: (s, 0))] + ck_specs + sp_specs +
                 [pl.BlockSpec((t_new, 3 * A_KVW), lambda s, pt: (s, 0)),
                  pl.BlockSpec((None, None, 512, wbuf), lambda s, pt: (s, layer, 0, 0)),
                  pl.BlockSpec(es.shape, lambda s, pt: (0, 0))],
        out_specs=[pl.BlockSpec((t_new, A_Q), lambda s, pt: (s, 0))] * 3,
        scratch_shapes=[pltpu.VMEM((nslot, 512), F32),
                        pltpu.VMEM((512, past), BF16)],
    )
    return pl.pallas_call(
        functools.partial(_nsa_sample_kernel, n_pages=n_pages, t_new=t_new, past=past, wbuf=wbuf,
                          ns=ns, nslot=nslot),
        grid_spec=grid_spec,
        out_shape=[jax.ShapeDtypeStruct((n * t_new, A_Q), F32)] * 3,
        compiler_params=_cparams(("arbitrary",)),
        name="nsa_sample",
    )(page_table, q_exp, *([ckv_pool] * n_pages), *([slc_t] * n_pages), kv_new, win_t, es)


def _take_cols(w, src):
    src = np.asarray(src)
    cols = jnp.take(w, jnp.asarray(np.maximum(src, 0)), axis=1)
    return jnp.where(jnp.asarray(src >= 0)[None, :], cols, 0.0).astype(BF16)


A_Q0, A_KC0, A_KS0, A_KW0 = 0, A_Q, A_Q + A_KVW, A_Q + 2 * A_KVW
A_GATE0 = A_Q + 3 * A_KVW
A_Z0 = A_GATE0 + 3 * A_HEADS
A_QM0 = A_Z0 + 3 * A_Q
A_ZM0 = A_QM0 + MEM_W
A_QW = A_HEADS * LANES
A_RESTW = 3 * A_Q + 2 * MEM_W + LANES


def _a_q_cols():
    src = []
    for h in range(A_HEADS):
        g = h // A_REP
        slot = [-1] * LANES
        for d in range(HEAD_DIM):
            slot[(g % 2) * HEAD_DIM + d] = A_Q0 + h * HEAD_DIM + d
        src += slot
    return src


def _a_rest_cols():
    return (list(range(A_Z0, A_Z0 + 3 * A_Q)) + list(range(A_QM0, A_QM0 + MEM_W)) +
            list(range(A_ZM0, A_ZM0 + MEM_W)) + list(range(A_GATE0, A_GATE0 + 3 * A_HEADS)) +
            [-1] * (LANES - 3 * A_HEADS))


def _b_q_cols():
    src = []
    for gi in range(N_B_GROUPS):
        for h in range(B_KV):
            for j in range(B_REP):
                slot = [-1] * LANES
                for d in range(HEAD_DIM):
                    slot[(h % 2) * HEAD_DIM + d] = gi * B_O + (h * B_REP + j) * HEAD_DIM + d
                src += slot
    return src


def _blockdiag2(w):
    z = jnp.zeros_like(w)
    top = jnp.concatenate([w, z], axis=-1)
    bot = jnp.concatenate([z, w], axis=-1)
    return jnp.concatenate([top, bot], axis=-2)


def _class_perm(dil):
    p = np.zeros((CM_TILE, CM_TILE), np.float32)
    s = np.arange(CM_TILE)
    p[(s % dil) * (CM_TILE // dil) + s // dil, s] = 1.0
    return p


def kernel(x_prompt, x_sample, mem_prompt, cache_cmp_kv, cache_slc_kv, cache_win_kv, cache_dil_kv, cache_mem_kv,
           page_table, g_pre, g_post, g_mem, w_mem_kv, w_in_a, w_out_a, cmp_pos, cmp_w1, cmp_w2,
           g_kv_b, w_kv_b, w_in_b, w_out_b):
    n_b, s_len, d = x_prompt.shape
    n_s, t_s, _ = x_sample.shape
    depth = g_pre.shape[0]
    n_a = w_in_a.shape[0]
    n_pool, page = cache_cmp_kv.shape[:2]
    mem_len = mem_prompt.shape[1]
    wbuf_a = cache_win_kv.shape[1]
    wbuf_b = cache_dil_kv.shape[1]
    ns_p = s_len // SLC_LEN
    m_p = n_b * s_len
    assert s_len % TM == 0 and mem_len % TM == 0 or mem_len == TM
    spb = s_len // TM
    nkt = s_len // TK
    sb_s = max(c for c in (8, 4, 2, 1) if n_s % c == 0)

    xp = x_prompt.reshape(m_p, d)
    xs = x_sample.reshape(n_s * t_s, d)
    mem2d = mem_prompt.reshape(n_b * mem_len, d)

    egate = np.zeros((LANES, 3 * A_Q), np.float32)
    for b in range(3):
        for h in range(A_HEADS):
            egate[b * A_HEADS + h, b * A_Q + h * HEAD_DIM:b * A_Q + (h + 1) * HEAD_DIM] = 1.0
    egate = jnp.asarray(egate, BF16)
    etile = (np.arange(ns_p)[None, :, None] ==
             (np.arange(nkt)[:, None, None] * TK + np.arange(TK)[None, None, :]) // SLC_LEN)
    etile = jnp.asarray(etile.astype(np.float32), BF16)
    anymat = np.zeros((ns_p, LANES), np.float32)
    anymat[np.arange(ns_p), np.arange(ns_p) // (TK // SLC_LEN)] = 1.0
    anymat = jnp.asarray(anymat, BF16)
    perms = [jnp.asarray(_class_perm(dil), BF16) for _, dil in B_GROUPS[1:]]
    unperms = [jnp.asarray(_class_perm(dil).T, BF16) for _, dil in B_GROUPS[1:]]

    cmp_t = jnp.transpose(cache_cmp_kv, (0, 2, 3, 4, 5, 1)).reshape(n_pool, n_a, A_KVW, page)
    slc_t = jnp.transpose(cache_slc_kv, (0, 2, 3, 4, 5, 1)).reshape(n_pool, n_a, A_KVW, page)
    win_t = jnp.transpose(cache_win_kv, (0, 2, 3, 4, 5, 1)).reshape(n_s, n_a, A_KVW, wbuf_a)
    dil_t = jnp.transpose(cache_dil_kv, (0, 2, 3, 4, 1)).reshape(n_s, 2 * B_KV * HEAD_DIM, wbuf_b)
    mem_t = jnp.transpose(cache_mem_kv, (0, 2, 3, 4, 5, 1)).reshape(n_s, depth, 2 * MEM_W, mem_len)

    a_q_cols, a_rest_cols, b_q_cols = _a_q_cols(), _a_rest_cols(), _b_q_cols()
    row_spec = lambda w: pl.BlockSpec((TM, w), lambda i: (i, 0))

    kct_l, kst_l, kwt_l, kv_s, mkvt_l = [], [], [], [], []
    kvbt_p = kvb_s = None
    kvb_cm = None
    for l in range(depth):
        mkvt = _proj(mem2d, g_mem[l], [w_mem_kv[l].T.astype(BF16)], [],
                     [(True, 0, a, b, 0, (_st_t(0, a),)) for a, b in _chunks(0, 2 * MEM_W)],
                     [jax.ShapeDtypeStruct((n_b, 2 * MEM_W, mem_len), F32)],
                     [_t_spec(2 * MEM_W, mem_len // min(TM, mem_len), min(TM, mem_len))])[0]
        mkvt_l.append(mkvt)
        mkvt4 = mkvt.reshape(n_b, 1, 2 * MEM_W, mem_len)
        if l < n_a:
            w = w_in_a[l]
            w_n = jnp.concatenate([w[:, A_KC0:A_KC0 + A_KVW].astype(BF16), _take_cols(w, a_q_cols),
                                   _take_cols(w, a_rest_cols)], axis=1)
            w_t = w[:, A_KC0:A_KC0 + 3 * A_KVW].T.astype(BF16)
            w_out = w_out_a[l].astype(BF16)
            pos2 = jnp.concatenate([cmp_pos[l], cmp_pos[l]], axis=-1).transpose(1, 0, 2)
            w1bd = _blockdiag2(cmp_w1[l]).transpose(1, 0, 2, 3).astype(BF16)
            w2bd = _blockdiag2(cmp_w2[l]).astype(BF16)
            w2bdt = jnp.swapaxes(w2bd, 1, 2)
            plan = []
            for j in range(3):
                stores = [_st_t(j, 0)]
                if j > 0:
                    stores.append(_st_ttile(3, (j - 1) * A_KVW))
                plan.append((True, 1, j * A_KVW, (j + 1) * A_KVW, 0, tuple(stores)))
            for j in range(4):
                plan.append((False, 0, j * LANES, (j + 1) * LANES, 0, (_st_rows(4 + j, 0),)))
            for a, b in _chunks(A_KVW, A_KVW + A_QW):
                plan.append((False, 0, a, b, 0, (_st_rows(8, a - A_KVW, QSCALE),)))
            for a, b in _chunks(A_KVW + A_QW, A_KVW + A_QW + A_RESTW):
                plan.append((False, 0, a, b, 0, (_st_rows(9, a - A_KVW - A_QW),)))
            qm0 = A_KVW + A_QW + 3 * A_Q
            plan.append((False, 0, qm0, qm0 + MEM_W, 0, (_st_rows(10, 0),)))
            shapes = ([jax.ShapeDtypeStruct((n_b, A_KVW, s_len), F32)] * 3 +
                      [jax.ShapeDtypeStruct((n_b, s_len // LANES, 2 * A_KVW, LANES), BF16)] +
                      [jax.ShapeDtypeStruct((m_p, LANES), F32)] * 4 +
                      [jax.ShapeDtypeStruct((m_p, A_QW), BF16), jax.ShapeDtypeStruct((m_p, A_RESTW), F32),
                       jax.ShapeDtypeStruct((m_p, MEM_W), F32)])
            specs = ([_t_spec(A_KVW, spb, TM)] * 3 +
                     [pl.BlockSpec((None, TM // LANES, 2 * A_KVW, LANES), lambda i: (i // spb, i % spb, 0, 0))] +
                     [row_spec(LANES)] * 4 + [row_spec(A_QW), row_spec(A_RESTW), row_spec(MEM_W)])
            outs = _proj(xp, g_pre[l], [w_n, w_t], [], plan, shapes, specs)
            kct, kst, kwt, kvt = outs[0:4]
            slabs, q, rest, qm = outs[4:8], outs[8], outs[9], outs[10]
            kct_l.append(kct)
            kst_l.append(kst)
            kwt_l.append(kwt)
            ckt, cv = _compress_prompt(slabs, pos2, w1bd, w2bd, w2bdt, n_b, s_len)
            q3 = q.reshape(n_b, s_len, A_QW)
            o_cmp, sel, flags = _nsa_cmp(q3, ckt, cv, anymat, n_b, s_len, TQ)
            flags = (flags[:, :, 0:A_KV, 0:nkt] > 0.5).astype(jnp.int32).reshape(-1)
            o_slc, o_win = _nsa_sw(flags, q3, kvt, sel, etile, n_b, s_len, TQ, TK)
            om = _mem_attend(qm, 0, mkvt4, 0, n_b, s_len, min(512, s_len))
            xp = _merge_a(xp, o_cmp.reshape(-1, A_Q), o_slc.reshape(-1, A_Q), o_win.reshape(-1, A_Q),
                          rest, om, w_out, g_post[l], egate)
            w_rows = jnp.concatenate([w[:, A_KC0:A_KC0 + 3 * A_KVW].astype(BF16), w_n[:, A_KVW:]], axis=1)
            kv, q, rest = _proj_rows(xs, g_pre[l], w_rows,
                                     [(0, 3 * A_KVW, F32, 1.0), (3 * A_KVW, A_QW, BF16, QSCALE),
                                      (3 * A_KVW + A_QW, A_RESTW, F32, 1.0)])
            kv_s.append(kv)
            posr = jnp.tile(pos2.transpose(1, 0, 2), (1, page // CMP_LEN, 1))
            ckv_pool = _compress_pool(cmp_t, l, posr, w1bd, w2bd, 32)
            ckv_pool = ckv_pool.reshape(n_pool, page // CMP_LEN, 512)
            o_cmp, o_slc, o_win = _nsa_sample(q, ckv_pool, slc_t, kv, win_t, page_table, l, n_s, t_s)
            om = _mem_attend(rest, 9, mem_t, l, n_s, t_s, t_s, sb_s)
            xs = _merge_a(xs, o_cmp, o_slc, o_win, rest, om, w_out, g_post[l], egate)
            if l == n_a - 1:
                w_kv = w_kv_b.astype(BF16)
                plan = [(True, 1, 0, 512, 0, (_st_t(0, 0),)),
                        (False, 0, 0, 512, 0, (_st_rows(1, 0),)),
                        (False, 0, 0, 512, 1, (_st_cm(2, 0, B_GROUPS[1][1]),)),
                        (False, 0, 0, 512, 2, (_st_cm(3, 0, B_GROUPS[2][1]),))]
                shapes = [jax.ShapeDtypeStruct((n_b, 512, s_len), F32), jax.ShapeDtypeStruct((m_p, 512), BF16)]
                specs = [_t_spec(512, spb, TM), row_spec(512)]
                for _, dil in B_GROUPS[1:]:
                    shapes.append(jax.ShapeDtypeStruct((m_p // CM_TILE, dil, CM_TILE // dil, 512), BF16))
                    specs.append(_cm_spec(dil, 512))
                kvbt_p, kv0, kv1, kv2 = _proj(xp, g_kv_b, [w_kv, w_kv_b.T.astype(BF16)], perms, plan, shapes, specs)
                kvb_cm = [kv0.reshape(n_b, s_len, 512)] + [
                    a.reshape((n_b, s_len // CM_TILE) + a.shape[1:]) for a in (kv1, kv2)]
                kvb_s = _proj_rows(xs, g_kv_b, w_kv, [(0, 512, F32, 1.0)])[0]
        else:
            lb = l - n_a
            w = w_in_b[lb]
            w_n = jnp.concatenate([_take_cols(w, b_q_cols), w[:, B_Q:].astype(BF16)], axis=1)
            w_out = w_out_b[lb].astype(BF16)
            qw = B_HPG * LANES
            restw = B_O + 2 * MEM_W
            plan, shapes, specs = [], [], []
            for gi, (_, dil) in enumerate(B_GROUPS):
                for a, b in _chunks(gi * qw, (gi + 1) * qw):
                    st = _st_rows(gi, a - gi * qw, QSCALE) if dil == 1 else _st_cm(gi, a - gi * qw, dil, QSCALE)
                    plan.append((False, 0, a, b, gi, (st,)))
                if dil == 1:
                    shapes.append(jax.ShapeDtypeStruct((m_p, qw), BF16))
                    specs.append(row_spec(qw))
                else:
                    shapes.append(jax.ShapeDtypeStruct((m_p // CM_TILE, dil, CM_TILE // dil, qw), BF16))
                    specs.append(_cm_spec(dil, qw))
            for a, b in _chunks(3 * qw, 3 * qw + restw):
                plan.append((False, 0, a, b, 0, (_st_rows(3, a - 3 * qw),)))
            shapes.append(jax.ShapeDtypeStruct((m_p, restw), F32))
            specs.append(row_spec(restw))
            qm0 = 3 * qw + B_O
            plan.append((False, 0, qm0, qm0 + MEM_W, 0, (_st_rows(4, 0),)))
            shapes.append(jax.ShapeDtypeStruct((m_p, MEM_W), F32))
            specs.append(row_spec(MEM_W))
            q0, q1, q2, rest, qm = _proj(xp, g_pre[l], [w_n], perms, plan, shapes, specs)
            qs = [q0.reshape(n_b, s_len, qw)] + [a.reshape((n_b, s_len // CM_TILE) + a.shape[1:]) for a in (q1, q2)]
            outs, lses = [], []
            for gi in range(N_B_GROUPS):
                o, lse = _dilated_prompt(qs[gi], kvb_cm[gi], gi, n_b, s_len)
                if gi == 0:
                    o, lse = o.reshape(m_p, B_O), lse.reshape(m_p, B_O)
                else:
                    o, lse = (a.reshape((m_p // CM_TILE,) + a.shape[2:]) for a in (o, lse))
                outs.append(o)
                lses.append(lse)
            om = _mem_attend(qm, 0, mkvt4, 0, n_b, s_len, min(512, s_len))
            xp = _merge_b(xp, outs, lses, rest, om, w_out, g_post[l], unperms)
            q, rest = _proj_rows(xs, g_pre[l], w_n, [(0, 3 * qw, BF16, QSCALE), (3 * qw, restw, F32, 1.0)])
            outs, lses = _dilated_sample(q, kvb_s, dil_t, n_s, t_s)
            om = _mem_attend(rest, 2, mem_t, l, n_s, t_s, t_s, sb_s)
            xs = _merge_b(xs, outs, lses, rest, om, w_out, g_post[l], [])

    def from_t(arrs, n, t):
        a = jnp.stack(arrs, 1).reshape(n, len(arrs), 2, A_KV, HEAD_DIM, t)
        return jnp.transpose(a, (0, 5, 1, 2, 3, 4))

    def kv_stack(kvs, n, t, j):
        return jnp.stack([k[:, j * A_KVW:(j + 1) * A_KVW].reshape(n, t, 2, A_KV, HEAD_DIM) for k in kvs], 2)

    new_cmp_p = from_t(kct_l, n_b, s_len)
    new_slc_p = from_t(kst_l, n_b, s_len)
    wa = min(WIN_A, s_len)
    new_win_p = from_t([k[:, :, s_len - wa:] for k in kwt_l], n_b, wa)
    new_cmp_s = kv_stack(kv_s, n_s, t_s, 0)
    new_slc_s = kv_stack(kv_s, n_s, t_s, 1)
    win_s = kv_stack(kv_s, n_s, t_s, 2)
    new_win_s = jnp.concatenate([cache_win_kv, win_s], 1)[:, -wbuf_a:]
    wb = min(B_GROUPS[-1][0], s_len)
    new_dil_p = jnp.transpose(kvbt_p[:, :, s_len - wb:].reshape(n_b, 2, B_KV, HEAD_DIM, wb), (0, 4, 1, 2, 3))
    new_dil_s = jnp.concatenate([cache_dil_kv, kvb_s.reshape(n_s, t_s, 2, B_KV, HEAD_DIM)], 1)[:, -wbuf_b:]
    new_mem_p = jnp.transpose(jnp.stack(mkvt_l, 1).reshape(n_b, depth, 2, MEM_HEADS, HEAD_DIM, mem_len),
                              (0, 5, 1, 2, 3, 4))
    return (xp.reshape(n_b, s_len, d), xs.reshape(n_s, t_s, d), new_cmp_p, new_cmp_s, new_slc_p, new_slc_s,
            new_win_p, new_win_s, new_dil_p, new_dil_s, new_mem_p)
```

```python
import functools
import numpy as np
import jax
import jax.numpy as jnp
from jax import lax
from jax.experimental import pallas as pl
from jax.experimental.pallas import tpu as pltpu

F32 = jnp.float32
BF16 = jnp.bfloat16

HEAD_DIM = 64
LANES = 128
A_HEADS = 12
A_KV = 4
A_REP = A_HEADS // A_KV
CMP_LEN = 32
SLC_LEN = 64
N_SEL = 16
WIN_A = 512
B_GROUPS = ((128, 1), (512, 4), (2048, 16))
N_B_GROUPS = 3
B_KV = 4
B_REP = 2
B_HPG = B_KV * B_REP
MEM_HEADS = 4
EPS = 1e-6
A_Q = A_HEADS * HEAD_DIM
A_KVW = 2 * A_KV * HEAD_DIM
MEM_W = MEM_HEADS * HEAD_DIM
B_Q = N_B_GROUPS * B_HPG * HEAD_DIM
B_O = B_HPG * HEAD_DIM
QSCALE = HEAD_DIM ** -0.5
NEG = -1e30
VMEM_LIMIT = 56 * 1024 * 1024
TM = 256
TQ = 128
TK = 512
CM_TILE = 256


def _alibi(n):
    return [float(2.0 ** (-8.0 * i / n)) for i in range(1, n + 1)]


SLOPES_A = _alibi(A_HEADS)
SLOPES_B = _alibi(B_HPG)


def _cparams(sem):
    return pltpu.CompilerParams(dimension_semantics=sem, vmem_limit_bytes=VMEM_LIMIT)


def _silu(x):
    return x * (1.0 / (1.0 + jnp.exp(-x)))


def _dot_nt(a, b):
    return lax.dot_general(a, b, (((1,), (1,)), ((), ())), preferred_element_type=F32)


def _dot(a, b):
    return jnp.dot(a, b, preferred_element_type=F32)


def _split3(x):
    hi = x.astype(BF16)
    r1 = x - hi.astype(F32)
    mid = r1.astype(BF16)
    lo = (r1 - mid.astype(F32)).astype(BF16)
    return hi, mid, lo


def _pick_half(lane, a, b):
    return jnp.where(lane < HEAD_DIM, a, b)


def _assemble(pieces, rows):
    lane = lax.broadcasted_iota(jnp.int32, (rows, LANES), 1)
    cols = []
    for c in range(len(pieces) // 2):
        halves = []
        for k in (0, 1):
            arr, useful = pieces[2 * c + k]
            if useful != k:
                arr = pltpu.roll(arr, HEAD_DIM, 1)
            halves.append(arr)
        cols.append(_pick_half(lane, halves[0], halves[1]))
    return jnp.concatenate(cols, axis=1)


def _proj_kernel(x_ref, g_ref, *refs, n_w, n_p, plan):
    w_refs = refs[:n_w]
    p_refs = refs[n_w:n_w + n_p]
    out_refs = refs[n_w + n_p:]
    x = x_ref[...]
    ms = jnp.mean(x * x, axis=-1, keepdims=True)
    xn = ((x * lax.rsqrt(ms + EPS)) * g_ref[...]).astype(BF16)
    xs = [xn] + [_dot(p_ref[...], xn).astype(BF16) for p_ref in p_refs]
    for (transposed, wi, c0, c1, xi, stores) in plan:
        if transposed:
            y = _dot_nt(w_refs[wi][c0:c1, :], xs[xi])
        else:
            y = _dot(xs[xi], w_refs[wi][:, c0:c1])
        for store in stores:
            store(out_refs, y)


def _st_rows(oi, o0, scale=1.0):
    def store(out_refs, y):
        v = y if scale == 1.0 else y * scale
        out_refs[oi][:, o0:o0 + y.shape[1]] = v.astype(out_refs[oi].dtype)
    return store


def _st_t(oi, o0):
    def store(out_refs, y):
        out_refs[oi][o0:o0 + y.shape[0], :] = y.astype(out_refs[oi].dtype)
    return store


def _st_ttile(oi, o0):
    def store(out_refs, y):
        for u in range(y.shape[1] // LANES):
            out_refs[oi][u, o0:o0 + y.shape[0], :] = y[:, u * LANES:(u + 1) * LANES].astype(out_refs[oi].dtype)
    return store


def _st_cm(oi, o0, dil, scale=1.0):
    def store(out_refs, y):
        v = y if scale == 1.0 else y * scale
        v = v.astype(out_refs[oi].dtype)
        out_refs[oi][:, :, o0:o0 + y.shape[1]] = v.reshape(dil, y.shape[0] // dil, y.shape[1])
    return store


def _chunks(c0, c1, step=512):
    return [(a, min(a + step, c1)) for a in range(c0, c1, step)]


def _proj(x2d, g, weights, perms, plan, out_shapes, out_specs, tm=TM):
    m, d = x2d.shape
    tm = min(tm, m)
    const2 = lambda i: (0, 0)
    return pl.pallas_call(
        functools.partial(_proj_kernel, n_w=len(weights), n_p=len(perms), plan=tuple(plan)),
        grid=(m // tm,),
        in_specs=[pl.BlockSpec((tm, d), lambda i: (i, 0)), pl.BlockSpec((1, d), const2)] +
                 [pl.BlockSpec(w.shape, const2) for w in weights] +
                 [pl.BlockSpec(p.shape, const2) for p in perms],
        out_specs=out_specs,
        out_shape=out_shapes,
        compiler_params=_cparams(("arbitrary",)),
        name="proj",
    )(x2d, g.reshape(1, d), *weights, *perms)


def _proj_rows(x2d, g, w_bf, outs):
    m = x2d.shape[0]
    tm = min(TM, m)
    plan = []
    for oi, (c0, width, _, scale) in enumerate(outs):
        for a, b in _chunks(c0, c0 + width):
            plan.append((False, 0, a, b, 0, (_st_rows(oi, a - c0, scale),)))
    return _proj(x2d, g, [w_bf], [], plan,
                 [jax.ShapeDtypeStruct((m, o[1]), o[2]) for o in outs],
                 [pl.BlockSpec((tm, o[1]), lambda i: (i, 0)) for o in outs])


def _t_spec(width, spb, tm):
    return pl.BlockSpec((None, width, tm), lambda i: (i // spb, 0, i % spb))


def _cm_spec(dil, width):
    return pl.BlockSpec((None, dil, CM_TILE // dil, width), lambda i: (i, 0, 0, 0))


def _compress_pool_kernel(x_ref, posr_ref, w1_ref, w2_ref, o_ref, s0, s1, s2, s3, *, pages, pitch):
    scr = (s0, s1, s2, s3)
    page = x_ref.shape[2]
    per_page = page // CMP_LEN
    rows = per_page * pages

    def relayout(pg, carry):
        for idx in range(4):
            t = x_ref[pg, idx * LANES:(idx + 1) * LANES, :].T + posr_ref[idx // 2]
            for k in range(page // 8):
                n, l0 = (8 * k) // CMP_LEN, (8 * k) % CMP_LEN
                scr[idx][pl.ds(l0 * pitch + pg * per_page + n, 8, stride=pitch), :] = t[8 * k:8 * k + 8, :]
        return carry

    lax.fori_loop(0, pages, relayout, 0, unroll=4)
    for idx in range(4):
        kv = idx // 2
        acc = jnp.zeros((rows, LANES), F32)
        for l in range(CMP_LEN):
            xl = scr[idx][l * pitch:l * pitch + rows, :].astype(BF16)
            acc = acc + _dot(xl, w1_ref[l, kv])
        h = _silu(acc).astype(BF16)
        o_ref[:, idx * LANES:(idx + 1) * LANES] = _dot(h, w2_ref[kv])


def _compress_pool(cmp_t, layer, posr, w1bd, w2bd, pages):
    n_pool, _, _, page = cmp_t.shape
    per_page = page // CMP_LEN
    pages = max(p for p in range(1, min(pages, n_pool) + 1) if n_pool % p == 0 and (per_page * p) % 16 == 0)
    assert page % CMP_LEN == 0 and page == LANES
    rows = per_page * pages
    pitch = rows + 8
    assert rows % 16 == 0
    return pl.pallas_call(
        functools.partial(_compress_pool_kernel, pages=pages, pitch=pitch),
        grid=(n_pool // pages,),
        in_specs=[pl.BlockSpec((pages, None, 512, page), lambda i: (i, layer, 0, 0)),
                  pl.BlockSpec(posr.shape, lambda i: (0, 0, 0)),
                  pl.BlockSpec(w1bd.shape, lambda i: (0, 0, 0, 0)),
                  pl.BlockSpec(w2bd.shape, lambda i: (0, 0, 0))],
        out_specs=pl.BlockSpec((rows, 512), lambda i: (i, 0)),
        out_shape=jax.ShapeDtypeStruct((n_pool * per_page, 512), F32),
        scratch_shapes=[pltpu.VMEM((CMP_LEN * pitch, LANES), F32)] * 4,
        compiler_params=_cparams(("arbitrary",)),
        name="compress_pool",
    )(cmp_t, posr, w1bd, w2bd)


def _compress_prompt_kernel(s0, s1, s2, s3, pos_ref, w1_ref, w2_ref, w2t_ref, ckt_ref, cv_ref, *, nc):
    half = nc // 2
    slabs = (s0, s1, s2, s3)
    accs = [jnp.zeros((nc, LANES), F32) for _ in range(4)]
    for l in range(CMP_LEN):
        for idx in range(4):
            kv = idx // 2
            xe = slabs[idx][pl.ds(l, half, stride=2 * CMP_LEN), :]
            xo = slabs[idx][pl.ds(CMP_LEN + l, half, stride=2 * CMP_LEN), :]
            xb = (jnp.concatenate([xe, xo], axis=0) + pos_ref[l, kv:kv + 1, :]).astype(BF16)
            accs[idx] = accs[idx] + _dot(xb, w1_ref[l, kv])
    for idx in range(4):
        kv, p = idx // 2, idx % 2
        h = _silu(accs[idx]).astype(BF16)
        if kv == 0:
            ckt_ref[p * LANES:(p + 1) * LANES, :] = _dot_nt(w2t_ref[0], h)
        else:
            cv_ref[:, p * LANES:(p + 1) * LANES] = _dot(h, w2_ref[1])


def _compress_prompt(slabs, pos2, w1bd, w2bd, w2bdt, n_b, s_len):
    nc = s_len // CMP_LEN
    slab_spec = pl.BlockSpec((s_len, LANES), lambda b: (b, 0))
    return pl.pallas_call(
        functools.partial(_compress_prompt_kernel, nc=nc),
        grid=(n_b,),
        in_specs=[slab_spec] * 4 +
                 [pl.BlockSpec(pos2.shape, lambda b: (0, 0, 0)),
                  pl.BlockSpec(w1bd.shape, lambda b: (0, 0, 0, 0)),
                  pl.BlockSpec(w2bd.shape, lambda b: (0, 0, 0)),
                  pl.BlockSpec(w2bdt.shape, lambda b: (0, 0, 0))],
        out_specs=[pl.BlockSpec((None, 256, nc), lambda b: (b, 0, 0)),
                   pl.BlockSpec((None, nc, 256), lambda b: (b, 0, 0))],
        out_shape=[jax.ShapeDtypeStruct((n_b, 256, nc), F32),
                   jax.ShapeDtypeStruct((n_b, nc, 256), F32)],
        compiler_params=_cparams(("arbitrary",)),
        name="compress_prompt",
    )(*slabs, pos2, w1bd, w2bd, w2bdt)


def _topk_mask_t(imp_t, k):
    nb, cols = imp_t.shape
    row = lax.broadcasted_iota(jnp.int32, (nb, cols), 0)
    work = imp_t
    sel = jnp.zeros((nb, cols), F32)
    for _ in range(k):
        m = jnp.max(work, axis=0, keepdims=True)
        idx = jnp.min(jnp.where(work == m, row, nb), axis=0, keepdims=True)
        hit = row == idx
        sel = jnp.where(hit, 1.0, sel)
        work = jnp.where(hit, -3.0, work)
    return sel


def _group_q(q_ref, g):
    return jnp.concatenate(
        [q_ref[:, (g * A_REP + r) * LANES:(g * A_REP + r + 1) * LANES] for r in range(A_REP)], axis=0)


def _nsa_cmp_kernel(q_ref, ckt_ref, cv_ref, any_ref, o_ref, sel_ref, flag_ref, *, tq, nc, ns):
    i = pl.program_id(1)
    q0 = i * tq
    half = nc // 2
    pos = q0 + lax.broadcasted_iota(jnp.int32, (tq, 1), 0)
    posf = pos.astype(F32)
    tok = lax.broadcasted_iota(jnp.int32, (1, nc), 1)
    cidx = jnp.where(tok < half, 2 * tok, 2 * (tok - half) + 1)
    c_end = cidx * CMP_LEN + (CMP_LEN - 1)
    c_mid = cidx.astype(F32) * CMP_LEN + 0.5 * (CMP_LEN - 1)
    cmask = c_end <= pos
    dist = posf - c_mid
    blk = lax.broadcasted_iota(jnp.int32, (1, ns), 1)
    cur = jnp.right_shift(pos, 6)
    forced = (blk == 0) | (blk == cur) | (blk == cur - 1)
    allowed = blk <= cur
    pieces = []
    anys = []
    for p in range(A_KV // 2):
        ckt = ckt_ref[p * LANES:(p + 1) * LANES, :].astype(BF16)
        cv = cv_ref[:, p * LANES:(p + 1) * LANES].astype(BF16)
        qq = jnp.concatenate([_group_q(q_ref, 2 * p), _group_q(q_ref, 2 * p + 1)], axis=0)
        s = _dot(qq, ckt)
        parts = []
        for hh in range(2 * A_REP):
            sr = s[hh * tq:(hh + 1) * tq, :] - SLOPES_A[2 * p * A_REP + hh] * dist
            parts.append(jnp.where(cmask, sr, -jnp.inf))
        sm = jnp.concatenate(parts, axis=0)
        m = jnp.max(sm, axis=-1, keepdims=True)
        m = jnp.where(m == -jnp.inf, 0.0, m)
        e = jnp.exp(sm - m)
        den = jnp.sum(e, axis=-1, keepdims=True)
        pc = e / jnp.where(den > 0, den, 1.0)
        og = _dot(pc.astype(BF16), cv)
        imps = []
        for k in range(2):
            for r in range(A_REP):
                pieces.append((og[(k * A_REP + r) * tq:(k * A_REP + r + 1) * tq, :], k))
            ps = sum(pc[(k * A_REP + r) * tq:(k * A_REP + r + 1) * tq] for r in range(A_REP))
            imp = ps[:, 0:half] + ps[:, half:nc]
            imp = jnp.where(allowed, imp, -1.0)
            imps.append(jnp.where(forced, -3.0, imp).T)
        sel_t = _topk_mask_t(jnp.concatenate(imps, axis=1), max(min(N_SEL, ns) - 3, 0))
        for k in range(2):
            g = 2 * p + k
            sel = jnp.where(forced, 1.0, sel_t[:, k * tq:(k + 1) * tq].T)
            sel_ref[:, g * ns:(g + 1) * ns] = sel.astype(BF16)
            anys.append(jnp.max(sel, axis=0, keepdims=True))
    o_ref[...] = _assemble(pieces, tq)
    rows = jnp.concatenate(anys + [jnp.zeros((8 - A_KV, ns), F32)], axis=0).astype(BF16)
    flag_ref[...] = _dot(rows, any_ref[...])


def _nsa_cmp(q_exp, ckt, cv, anymat, n_b, s_len, tq):
    nc = ckt.shape[2]
    ns = s_len // SLC_LEN
    assert nc == 2 * ns
    return pl.pallas_call(
        functools.partial(_nsa_cmp_kernel, tq=tq, nc=nc, ns=ns),
        grid=(n_b, s_len // tq),
        in_specs=[pl.BlockSpec((None, tq, A_HEADS * LANES), lambda b, i: (b, i, 0)),
                  pl.BlockSpec((None, 256, nc), lambda b, i: (b, 0, 0)),
                  pl.BlockSpec((None, nc, 256), lambda b, i: (b, 0, 0)),
                  pl.BlockSpec(anymat.shape, lambda b, i: (0, 0))],
        out_specs=[pl.BlockSpec((None, tq, A_Q), lambda b, i: (b, i, 0)),
                   pl.BlockSpec((None, tq, A_KV * ns), lambda b, i: (b, i, 0)),
                   pl.BlockSpec((None, None, 8, LANES), lambda b, i: (b, i, 0, 0))],
        out_shape=[jax.ShapeDtypeStruct((n_b, s_len, A_Q), F32),
                   jax.ShapeDtypeStruct((n_b, s_len, A_KV * ns), BF16),
                   jax.ShapeDtypeStruct((n_b, s_len // tq, 8, LANES), F32)],
        compiler_params=_cparams(("arbitrary", "arbitrary")),
        name="nsa_cmp",
    )(q_exp, ckt, cv, anymat)


def _nsa_sw_kernel(fl_ref, q_ref, kv_ref, sel_ref, e_ref, oslc_ref, owin_ref,
                   m_ref, l_ref, acc_ref, *, tq, tk, ns, nq, nkt):
    b = pl.program_id(0)
    i = pl.program_id(1)
    q0 = i * tq
    per = tk // LANES
    trow = lax.broadcasted_iota(jnp.int32, (tq, 1), 0)
    n_pair = A_KV // 2
    hpp = 2 * A_REP
    qp = [jnp.concatenate([_group_q(q_ref, 2 * p), _group_q(q_ref, 2 * p + 1)], axis=0) for p in range(n_pair)]

    m_ref[...] = jnp.full(m_ref.shape, NEG, F32)
    l_ref[...] = jnp.zeros(l_ref.shape, F32)
    acc_ref[...] = jnp.zeros(acc_ref.shape, F32)

    def tiles(j, r0):
        return jnp.concatenate([kv_ref[j * per + u, r0:r0 + LANES, :] for u in range(per)], axis=1)

    def body(j, carry):
        k0 = j * tk
        krel = (k0 - q0) + lax.broadcasted_iota(jnp.int32, (1, tk), 1)
        causal = krel <= trow
        krelf = krel.astype(F32)
        for p in range(n_pair):
            f0 = fl_ref[((b * nq + i) * A_KV + 2 * p) * nkt + j]
            f1 = fl_ref[((b * nq + i) * A_KV + 2 * p + 1) * nkt + j]

            @pl.when((f0 | f1) != 0)
            def _(p=p):
                s = _dot(qp[p], tiles(j, p * LANES))
                selk = _dot(jnp.concatenate([sel_ref[:, (2 * p + k) * ns:(2 * p + k + 1) * ns] for k in range(2)],
                                            axis=0), e_ref[j])
                valid = [(selk[k * tq:(k + 1) * tq, :] > 0.5) & causal for k in range(2)]
                parts = []
                for hh in range(hpp):
                    sr = s[hh * tq:(hh + 1) * tq, :] + SLOPES_A[2 * p * A_REP + hh] * krelf
                    parts.append(jnp.where(valid[hh // A_REP], sr, NEG))
                sm = jnp.concatenate(parts, axis=0)
                m_old = m_ref[p]
                m_new = jnp.maximum(m_old, jnp.max(sm, axis=-1, keepdims=True))
                pe = jnp.exp(sm - m_new)
                alpha = jnp.exp(m_old - m_new)
                l_ref[p] = alpha * l_ref[p] + jnp.sum(pe, axis=-1, keepdims=True)
                acc_ref[p] = alpha * acc_ref[p] + _dot_nt(pe.astype(BF16), tiles(j, 256 + p * LANES))
                m_ref[p] = m_new
        return carry

    lax.fori_loop(0, (q0 + tq + tk - 1) // tk, body, 0)
    pieces = []
    for p in range(n_pair):
        og = acc_ref[p] / l_ref[p]
        for hh in range(hpp):
            pieces.append((og[hh * tq:(hh + 1) * tq, :], hh // A_REP))
    oslc_ref[...] = _assemble(pieces, tq)

    nwt = WIN_A // LANES + tq // LANES
    t0 = jnp.maximum(i * (tq // LANES) - WIN_A // LANES, 0)
    nkw = nwt * LANES
    wrel = (t0 * LANES - q0) + lax.broadcasted_iota(jnp.int32, (1, nkw), 1)
    dist = trow - wrel
    wvalid = (dist >= 0) & (dist <= WIN_A)
    wrelf = wrel.astype(F32)
    pieces = []
    for p in range(n_pair):
        kk = jnp.concatenate([kv_ref[t0 + u, 512 + p * LANES:512 + (p + 1) * LANES, :] for u in range(nwt)], axis=1)
        vv = jnp.concatenate([kv_ref[t0 + u, 768 + p * LANES:768 + (p + 1) * LANES, :] for u in range(nwt)], axis=1)
        s = _dot(qp[p], kk)
        parts = []
        for hh in range(hpp):
            sr = s[hh * tq:(hh + 1) * tq, :] + SLOPES_A[2 * p * A_REP + hh] * wrelf
            parts.append(jnp.where(wvalid, sr, NEG))
        sm = jnp.concatenate(parts, axis=0)
        m = jnp.max(sm, axis=-1, keepdims=True)
        pe = jnp.exp(sm - m)
        den = jnp.sum(pe, axis=-1, keepdims=True)
        og = _dot_nt(pe.astype(BF16), vv) / den
        for hh in range(hpp):
            pieces.append((og[hh * tq:(hh + 1) * tq, :], hh // A_REP))
    owin_ref[...] = _assemble(pieces, tq)


def _nsa_sw(flags, q_exp, kvt, sel, etile, n_b, s_len, tq, tk):
    ns = s_len // SLC_LEN
    nq = s_len // tq
    nkt = s_len // tk
    assert s_len >= WIN_A + tq and tk % tq == 0 and tq % LANES == 0
    grid_spec = pltpu.PrefetchScalarGridSpec(
        num_scalar_prefetch=1,
        grid=(n_b, nq),
        in_specs=[pl.BlockSpec((None, tq, A_HEADS * LANES), lambda b, i, fl: (b, i, 0)),
                  pl.BlockSpec((None, s_len // LANES, 1024, LANES), lambda b, i, fl: (b, 0, 0, 0)),
                  pl.BlockSpec((None, tq, A_KV * ns), lambda b, i, fl: (b, i, 0)),
                  pl.BlockSpec(etile.shape, lambda b, i, fl: (0, 0, 0))],
        out_specs=[pl.BlockSpec((None, tq, A_Q), lambda b, i, fl: (b, i, 0)),
                   pl.BlockSpec((None, tq, A_Q), lambda b, i, fl: (b, i, 0))],
        scratch_shapes=[pltpu.VMEM((A_KV // 2, 2 * A_REP * tq, 1), F32),
                        pltpu.VMEM((A_KV // 2, 2 * A_REP * tq, 1), F32),
                        pltpu.VMEM((A_KV // 2, 2 * A_REP * tq, LANES), F32)],
    )
    return pl.pallas_call(
        functools.partial(_nsa_sw_kernel, tq=tq, tk=tk, ns=ns, nq=nq, nkt=nkt),
        grid_spec=grid_spec,
        out_shape=[jax.ShapeDtypeStruct((n_b, s_len, A_Q), F32),
                   jax.ShapeDtypeStruct((n_b, s_len, A_Q), F32)],
        compiler_params=_cparams(("arbitrary", "arbitrary")),
        name="nsa_sw",
    )(flags, q_exp, kvt, sel, etile)


def _mem_kernel(qm_ref, mkv_ref, o_ref, *, tt, sb):
    lane = lax.broadcasted_iota(jnp.int32, (tt, LANES), 1)
    for u in range(sb):
        cols = []
        for c in range(MEM_HEADS // 2):
            qc = qm_ref[u * tt:(u + 1) * tt, c * LANES:(c + 1) * LANES]
            mkt = mkv_ref[u, c * LANES:(c + 1) * LANES, :].astype(BF16)
            mvt = mkv_ref[u, MEM_W + c * LANES:MEM_W + (c + 1) * LANES, :].astype(BF16)
            halves = []
            for k in (0, 1):
                qh = jnp.where((lane < HEAD_DIM) == (k == 0), qc, 0.0).astype(BF16)
                s = _dot(qh, mkt) * QSCALE
                m = jnp.max(s, axis=-1, keepdims=True)
                e = jnp.exp(s - m)
                den = jnp.sum(e, axis=-1, keepdims=True)
                halves.append(_dot_nt(e.astype(BF16), mvt) / den)
            cols.append(_pick_half(lane, halves[0], halves[1]))
        o_ref[u * tt:(u + 1) * tt, :] = jnp.concatenate(cols, axis=1)


def _mem_attend(rest, qm_blk, mkvt4, layer, n, t, tt, sb=1):
    mem = mkvt4.shape[3]
    steps = t // tt
    assert n % sb == 0 and (sb == 1 or steps == 1)
    return pl.pallas_call(
        functools.partial(_mem_kernel, tt=tt, sb=sb),
        grid=(n // sb, steps),
        in_specs=[pl.BlockSpec((sb * tt, MEM_W), lambda b, i: (b * steps + i, qm_blk)),
                  pl.BlockSpec((sb, None, 2 * MEM_W, mem), lambda b, i: (b, layer, 0, 0))],
        out_specs=pl.BlockSpec((sb * tt, MEM_W), lambda b, i: (b * steps + i, 0)),
        out_shape=jax.ShapeDtypeStruct((n * t, MEM_W), F32),
        compiler_params=_cparams(("arbitrary", "arbitrary")),
        name="mem_attend",
    )(rest, mkvt4)


def _post(x_ref, y, gp_ref, out_ref):
    ms = jnp.mean(y * y, axis=-1, keepdims=True)
    out_ref[...] = x_ref[...] + (y * lax.rsqrt(ms + EPS)) * gp_ref[...]


def _merge_a_kernel(x_ref, oc_ref, os_ref, ow_ref, z_ref, zm_ref, gate_ref, om_ref,
                    w_ref, gp_ref, eg_ref, out_ref):
    gs = 1.0 / (1.0 + jnp.exp(-gate_ref[...]))
    eg = eg_ref[...]
    gexp = sum(_dot(t, eg) for t in _split3(gs))
    o = jnp.zeros(oc_ref.shape, F32)
    for b, ob_ref in enumerate((oc_ref, os_ref, ow_ref)):
        o = o + (ob_ref[...] * _silu(z_ref[:, b * A_Q:(b + 1) * A_Q])) * gexp[:, b * A_Q:(b + 1) * A_Q]
    om = om_ref[...] * _silu(zm_ref[...])
    y = _dot(o.astype(BF16), w_ref[0:A_Q, :]) + _dot(om.astype(BF16), w_ref[A_Q:A_Q + MEM_W, :])
    _post(x_ref, y, gp_ref, out_ref)


def _merge_a(x2d, o_cmp, o_slc, o_win, rest, om, w_bf, g_post, egate, tm=TM):
    m, d = x2d.shape
    tm = min(tm, m)
    row = lambda i: (i, 0)
    return pl.pallas_call(
        _merge_a_kernel,
        grid=(m // tm,),
        in_specs=[pl.BlockSpec((tm, d), row),
                  pl.BlockSpec((tm, A_Q), row), pl.BlockSpec((tm, A_Q), row), pl.BlockSpec((tm, A_Q), row),
                  pl.BlockSpec((tm, 3 * A_Q), row),
                  pl.BlockSpec((tm, MEM_W), lambda i: (i, 10)),
                  pl.BlockSpec((tm, LANES), lambda i: (i, 22)),
                  pl.BlockSpec((tm, MEM_W), row),
                  pl.BlockSpec((A_Q + MEM_W, d), lambda i: (0, 0)),
                  pl.BlockSpec((1, d), lambda i: (0, 0)),
                  pl.BlockSpec((LANES, 3 * A_Q), lambda i: (0, 0))],
        out_specs=pl.BlockSpec((tm, d), row),
        out_shape=jax.ShapeDtypeStruct((m, d), F32),
        compiler_params=_cparams(("arbitrary",)),
        name="merge_a",
    )(x2d, o_cmp, o_slc, o_win, rest, rest, rest, om, w_bf, g_post.reshape(1, d), egate)


def _merge_b_kernel(x_ref, o0_ref, o1_ref, o2_ref, l0_ref, l1_ref, l2_ref, z_ref, zm_ref, om_ref,
                    w_ref, gp_ref, *rest, n_u):
    u_refs = rest[:n_u]
    out_ref = rest[n_u]

    def nat(ref, ui):
        v = ref[...]
        if v.ndim == 2:
            return v
        v = v.reshape(v.shape[0] * v.shape[1], v.shape[2])
        u = u_refs[ui][...]
        return sum(_dot(u, t) for t in _split3(v))

    o0, o1, o2 = nat(o0_ref, 0), nat(o1_ref, 0), nat(o2_ref, 1)
    l0, l1, l2 = nat(l0_ref, 0), nat(l1_ref, 0), nat(l2_ref, 1)
    mx = jnp.maximum(jnp.maximum(l0, l1), l2)
    e0, e1, e2 = jnp.exp(l0 - mx), jnp.exp(l1 - mx), jnp.exp(l2 - mx)
    den = e0 + e1 + e2
    o = (e0 / den) * o0 + (e1 / den) * o1 + (e2 / den) * o2
    o = o * _silu(z_ref[...])
    om = om_ref[...] * _silu(zm_ref[...])
    y = _dot(o.astype(BF16), w_ref[0:B_O, :]) + _dot(om.astype(BF16), w_ref[B_O:B_O + MEM_W, :])
    _post(x_ref, y, gp_ref, out_ref)


def _merge_b(x2d, outs, lses, rest, om, w_bf, g_post, unperms, tm=TM):
    m, d = x2d.shape
    tm = min(tm, m)
    row = lambda i: (i, 0)

    def spec(a):
        if a.ndim == 2:
            return pl.BlockSpec((tm, B_O), row)
        assert tm == CM_TILE
        return pl.BlockSpec((None,) + a.shape[1:], lambda i: (i, 0, 0, 0))

    return pl.pallas_call(
        functools.partial(_merge_b_kernel, n_u=len(unperms)),
        grid=(m // tm,),
        in_specs=[pl.BlockSpec((tm, d), row)] + [spec(a) for a in outs] + [spec(a) for a in lses] +
                 [pl.BlockSpec((tm, B_O), row),
                  pl.BlockSpec((tm, MEM_W), lambda i: (i, 3)),
                  pl.BlockSpec((tm, MEM_W), row),
                  pl.BlockSpec((B_O + MEM_W, d), lambda i: (0, 0)),
                  pl.BlockSpec((1, d), lambda i: (0, 0))] +
                 [pl.BlockSpec(u.shape, lambda i: (0, 0)) for u in unperms],
        out_specs=pl.BlockSpec((tm, d), row),
        out_shape=jax.ShapeDtypeStruct((m, d), F32),
        compiler_params=_cparams(("arbitrary",)),
        name="merge_b",
    )(x2d, *outs, *lses, rest, rest, om, w_bf, g_post.reshape(1, d), *unperms)


def _dil_kernel(q_ref, kc_ref, kp_ref, o_ref, lse_ref, *, tn, wr):
    i = pl.program_id(2)
    nk = wr + tn

    def rows(ref):
        v = ref[...]
        return v if v.ndim == 2 else v.reshape(v.shape[0] * v.shape[1], v.shape[2])

    q, kc, kp = rows(q_ref), rows(kc_ref), rows(kp_ref)
    t = lax.broadcasted_iota(jnp.int32, (tn, 1), 0)
    k = lax.broadcasted_iota(jnp.int32, (1, nk), 1)
    dist = t - k + wr
    valid = (dist >= 0) & (dist <= wr) & ((k >= wr) | (i > 0))
    distf = dist.astype(F32)
    lane = lax.broadcasted_iota(jnp.int32, (tn, LANES), 1)
    o_cols, l_cols = [], []
    hpp = 2 * B_REP
    for p in range(B_KV // 2):
        kk = jnp.concatenate([kp[:, p * LANES:(p + 1) * LANES], kc[:, p * LANES:(p + 1) * LANES]], axis=0)
        vv = jnp.concatenate([kp[:, 256 + p * LANES:256 + (p + 1) * LANES],
                              kc[:, 256 + p * LANES:256 + (p + 1) * LANES]], axis=0)
        qh = jnp.concatenate([q[:, (hpp * p + hh) * LANES:(hpp * p + hh + 1) * LANES]
                              for hh in range(hpp)], axis=0)
        s = _dot_nt(qh, kk)
        parts = []
        for hh in range(hpp):
            sj = s[hh * tn:(hh + 1) * tn, :] - SLOPES_B[hpp * p + hh] * distf
            parts.append(jnp.where(valid, sj, NEG))
        sm = jnp.concatenate(parts, axis=0)
        m = jnp.max(sm, axis=-1, keepdims=True)
        e = jnp.exp(sm - m)
        den = jnp.sum(e, axis=-1, keepdims=True)
        og = _dot(e.astype(BF16), vv) / den
        lse = m + jnp.log(den)
        for k in range(2):
            halves, lhalves = [], []
            for j in range(B_REP):
                r0 = (k * B_REP + j) * tn
                piece = og[r0:r0 + tn, :]
                if k != j:
                    piece = pltpu.roll(piece, HEAD_DIM, 1)
                halves.append(piece)
                lhalves.append(jnp.broadcast_to(lse[r0:r0 + tn], (tn, LANES)))
            o_cols.append(_pick_half(lane, halves[0], halves[1]))
            l_cols.append(_pick_half(lane, lhalves[0], lhalves[1]))
    o = jnp.concatenate(o_cols, axis=1)
    l = jnp.concatenate(l_cols, axis=1)
    o_ref[...] = o.reshape(o_ref.shape)
    lse_ref[...] = l.reshape(lse_ref.shape)


def _dilated_prompt(q, kv, gi, n_b, s_len):
    win, dil = B_GROUPS[gi]
    wr = win // dil
    n = s_len // dil
    tn = min(CM_TILE, n)
    assert n % tn == 0 and tn % wr == 0
    ratio = tn // wr
    qw = B_HPG * LANES
    if dil == 1:
        in_specs = [pl.BlockSpec((None, tn, qw), lambda b, c, i: (b, i, 0)),
                    pl.BlockSpec((None, tn, 512), lambda b, c, i: (b, i, 0)),
                    pl.BlockSpec((None, wr, 512), lambda b, c, i: (b, jnp.maximum(i * ratio - 1, 0), 0))]
        out_spec = pl.BlockSpec((None, tn, B_O), lambda b, c, i: (b, i, 0))
        out_shape = jax.ShapeDtypeStruct((n_b, s_len, B_O), F32)
    else:
        rpt = CM_TILE // dil
        nt, npv = tn // rpt, wr // rpt
        in_specs = [pl.BlockSpec((None, nt, None, rpt, qw), lambda b, c, i: (b, i, c, 0, 0)),
                    pl.BlockSpec((None, nt, None, rpt, 512), lambda b, c, i: (b, i, c, 0, 0)),
                    pl.BlockSpec((None, npv, None, rpt, 512),
                                 lambda b, c, i: (b, jnp.maximum(i * ratio - 1, 0), c, 0, 0))]
        out_spec = pl.BlockSpec((None, nt, None, rpt, B_O), lambda b, c, i: (b, i, c, 0, 0))
        out_shape = jax.ShapeDtypeStruct((n_b, s_len // CM_TILE, dil, rpt, B_O), F32)
    return pl.pallas_call(
        functools.partial(_dil_kernel, tn=tn, wr=wr),
        grid=(n_b, dil, n // tn),
        in_specs=in_specs,
        out_specs=[out_spec, out_spec],
        out_shape=[out_shape, out_shape],
        compiler_params=_cparams(("arbitrary", "arbitrary", "arbitrary")),
        name="dilated_prompt",
    )(q, kv, kv)


def _dil_sample_kernel(q_ref, kn_ref, cache_ref, o_ref, lse_ref, *, t_new, wb, sb):
    for u in range(sb):
        r = pl.ds(u * t_new, t_new)
        _dil_sample_one(q_ref.at[r, :], kn_ref.at[r, :], cache_ref.at[u], o_ref.at[r, :], lse_ref.at[r, :],
                        t_new=t_new, wb=wb)


def _dil_sample_one(q_ref, kn_ref, cache_ref, o_ref, lse_ref, *, t_new, wb):
    rows = B_KV * B_REP * t_new
    rid = lax.broadcasted_iota(jnp.int32, (rows, 1), 0)
    pos = wb + (rid & (t_new - 1))
    lane = lax.broadcasted_iota(jnp.int32, (t_new, LANES), 1)
    zero = jnp.zeros((t_new, LANES), BF16)
    new = jnp.concatenate([kn_ref[...], jnp.zeros((LANES - t_new, 512), F32)], axis=0).astype(BF16)
    nidx = wb + lax.broadcasted_iota(jnp.int32, (1, LANES), 1)
    dn = pos - nidx
    for gi, (win, dil) in enumerate(B_GROUPS):
        lo = wb - min(win, wb)
        nk = wb - lo
        blocks = []
        for h in range(B_KV):
            for j in range(B_REP):
                c = gi * B_HPG + h * B_REP + j
                slot = q_ref[:, c * LANES:(c + 1) * LANES]
                blocks.append(jnp.concatenate([slot, zero] if h < 2 else [zero, slot], axis=1))
        qbd = jnp.concatenate(blocks, axis=0)
        kt = cache_ref[0:256, lo:wb].astype(BF16)
        vt = cache_ref[256:512, lo:wb].astype(BF16)
        s_c = _dot(qbd, kt)
        s_n = _dot_nt(qbd, new[:, 0:256])
        kidx = lo + lax.broadcasted_iota(jnp.int32, (1, nk), 1)
        dc = pos - kidx
        vc = (dc <= win) & ((dc & (dil - 1)) == 0)
        vn = (dn >= 0) & ((dn & (dil - 1)) == 0)
        dcf, dnf = dc.astype(F32), dn.astype(F32)
        pc, pn = [], []
        for hj in range(B_HPG):
            r = slice(hj * t_new, (hj + 1) * t_new)
            sl = SLOPES_B[hj] / dil
            pc.append(jnp.where(vc[r], s_c[r] - sl * dcf[r], NEG))
            pn.append(jnp.where(vn[r], s_n[r] - sl * dnf[r], NEG))
        sc, sn = jnp.concatenate(pc, axis=0), jnp.concatenate(pn, axis=0)
        m = jnp.maximum(jnp.max(sc, axis=-1, keepdims=True), jnp.max(sn, axis=-1, keepdims=True))
        ec, en = jnp.exp(sc - m), jnp.exp(sn - m)
        den = jnp.sum(ec, axis=-1, keepdims=True) + jnp.sum(en, axis=-1, keepdims=True)
        ow = (_dot_nt(ec.astype(BF16), vt) + _dot(en.astype(BF16), new[:, 256:512])) / den
        lse = m + jnp.log(den)
        for h in range(B_KV):
            halves, lhalves = [], []
            for j in range(B_REP):
                r0 = (h * B_REP + j) * t_new
                piece = ow[r0:r0 + t_new, (h // 2) * LANES:(h // 2 + 1) * LANES]
                if (h % 2) != j:
                    piece = pltpu.roll(piece, HEAD_DIM, 1)
                halves.append(piece)
                lhalves.append(jnp.broadcast_to(lse[r0:r0 + t_new], (t_new, LANES)))
            c0 = gi * B_O + h * LANES
            o_ref[:, c0:c0 + LANES] = _pick_half(lane, halves[0], halves[1])
            lse_ref[:, c0:c0 + LANES] = _pick_half(lane, lhalves[0], lhalves[1])


def _dilated_sample(q_exp, kv_new, cache_t, n, t_new):
    wb = cache_t.shape[2]
    assert t_new & (t_new - 1) == 0 and t_new <= LANES and all(w <= wb for w, _ in B_GROUPS)
    sb = 2 if n % 2 == 0 else 1
    o, lse = pl.pallas_call(
        functools.partial(_dil_sample_kernel, t_new=t_new, wb=wb, sb=sb),
        grid=(n // sb,),
        in_specs=[pl.BlockSpec((sb * t_new, N_B_GROUPS * B_HPG * LANES), lambda s: (s, 0)),
                  pl.BlockSpec((sb * t_new, 512), lambda s: (s, 0)),
                  pl.BlockSpec((sb, 512, wb), lambda s: (s, 0, 0))],
        out_specs=[pl.BlockSpec((sb * t_new, N_B_GROUPS * B_O), lambda s: (s, 0)),
                   pl.BlockSpec((sb * t_new, N_B_GROUPS * B_O), lambda s: (s, 0))],
        out_shape=[jax.ShapeDtypeStruct((n * t_new, N_B_GROUPS * B_O), F32),
                   jax.ShapeDtypeStruct((n * t_new, N_B_GROUPS * B_O), F32)],
        compiler_params=_cparams(("arbitrary",)),
        name="dilated_sample",
    )(q_exp, kv_new, cache_t)
    return ([o[:, g * B_O:(g + 1) * B_O] for g in range(N_B_GROUPS)],
            [lse[:, g * B_O:(g + 1) * B_O] for g in range(N_B_GROUPS)])


def _nsa_sample_kernel(pt_ref, q_ref, *refs, n_pages, t_new, past, wbuf, ns, nslot):
    ck_refs = refs[0:n_pages]
    sp_refs = refs[n_pages:2 * n_pages]
    kvn_ref, win_ref, es_ref = refs[2 * n_pages:2 * n_pages + 3]
    oc_ref, os_ref, ow_ref = refs[2 * n_pages + 3:2 * n_pages + 6]
    ck_s, kall = refs[2 * n_pages + 6:]
    del pt_ref
    page = sp_refs[0].shape[1]
    hs = nslot // 2
    per_page = page // CMP_LEN
    ck_s[...] = jnp.zeros(ck_s.shape, F32)
    for p in range(n_pages):
        for n in range(per_page):
            c = p * per_page + n
            slot = (c % 2) * hs + c // 2
            ck_s[slot:slot + 1, :] = ck_refs[p][n:n + 1, :]
    for p in range(n_pages):
        kall[:, p * page:(p + 1) * page] = sp_refs[p][...].astype(BF16)
    new = jnp.concatenate([kvn_ref[...], jnp.zeros((LANES - t_new, 3 * A_KVW), F32)], axis=0).astype(BF16)

    rows = A_HEADS * t_new
    zero = jnp.zeros((t_new, LANES), BF16)
    blocks = []
    for g in range(A_KV):
        for r in range(A_REP):
            h = g * A_REP + r
            slot = q_ref[:, h * LANES:(h + 1) * LANES]
            blocks.append(jnp.concatenate([slot, zero] if g < 2 else [zero, slot], axis=1))
    qbd = jnp.concatenate(blocks, axis=0)
    rid = lax.broadcasted_iota(jnp.int32, (rows, 1), 0)
    pos = past + (rid & (t_new - 1))
    posf = pos.astype(F32)
    npos = past + lax.broadcasted_iota(jnp.int32, (1, LANES), 1)
    nd = pos - npos
    ndf = nd.astype(F32)

    def hs_(h):
        return slice(h * t_new, (h + 1) * t_new)

    def head_rows(fn):
        return jnp.concatenate([fn(h) for h in range(A_HEADS)], axis=0)

    def finish(ow):
        pieces = []
        for g in range(A_KV):
            for r in range(A_REP):
                r0 = (g * A_REP + r) * t_new
                pieces.append((ow[r0:r0 + t_new, (g // 2) * LANES:(g // 2 + 1) * LANES], g % 2))
        return _assemble(pieces, t_new)

    n_c = n_pages * per_page
    slot_i = lax.broadcasted_iota(jnp.int32, (1, nslot), 1)
    sl_lo = slot_i & (hs - 1)
    cidx = 2 * sl_lo + jnp.where(slot_i >= hs, 1, 0)
    svalid = (sl_lo < n_c // 2) & (cidx * CMP_LEN + (CMP_LEN - 1) <= pos)
    dist = posf - (cidx.astype(F32) * CMP_LEN + 0.5 * (CMP_LEN - 1))
    s = _dot_nt(qbd, ck_s[:, 0:256].astype(BF16))
    sm = head_rows(lambda h: jnp.where(svalid[hs_(h)], s[hs_(h)] - SLOPES_A[h] * dist[hs_(h)], -jnp.inf))
    m = jnp.max(sm, axis=-1, keepdims=True)
    m = jnp.where(m == -jnp.inf, 0.0, m)
    e = jnp.exp(sm - m)
    den = jnp.sum(e, axis=-1, keepdims=True)
    pc = e / jnp.where(den > 0, den, 1.0)
    oc_ref[...] = finish(_dot(pc.astype(BF16), ck_s[:, 256:512].astype(BF16)))
    g_rows = A_KV * t_new
    ps = jnp.concatenate(
        [sum(pc[hs_(g * A_REP + r)] for r in range(A_REP)) for g in range(A_KV)], axis=0)
    imp = ps + pltpu.roll(ps, hs, 1)
    blk = lax.broadcasted_iota(jnp.int32, (1, nslot), 1)
    gid = lax.broadcasted_iota(jnp.int32, (g_rows, 1), 0)
    cur = jnp.right_shift(past + (gid & (t_new - 1)), 6)
    imp = jnp.where(blk < n_c // 2, imp, 0.0)
    imp = jnp.where((blk == 0) | (blk == cur) | (blk == cur - 1), A_REP + 1.0, imp)
    imp = jnp.where(blk <= cur, imp, -1.0)
    imp = jnp.where(blk < ns, imp, -3.0)
    cnt = jnp.zeros((g_rows, nslot), F32)
    for j in range(ns):
        col = imp[:, j:j + 1]
        ahead = (col > imp) | ((col == imp) & (blk > j))
        cnt = cnt + jnp.where(ahead, 1.0, 0.0)
    sel = jnp.where((cnt < min(N_SEL, ns)) & (blk < ns), 1.0, 0.0).astype(BF16)
    selk = _dot(sel, es_ref[...])
    kpos = lax.broadcasted_iota(jnp.int32, (1, past), 1)
    kposf = kpos.astype(F32)
    s_c = _dot(qbd, kall[0:256, :])
    s_n = _dot_nt(qbd, new[:, 512:768])

    def sel_c(h):
        g = h // A_REP
        return jnp.where(selk[g * t_new:(g + 1) * t_new, 0:past] > 0.5,
                         s_c[hs_(h)] - SLOPES_A[h] * (posf[hs_(h)] - kposf), NEG)

    def sel_n(h):
        g = h // A_REP
        ok = (selk[g * t_new:(g + 1) * t_new, past:past + LANES] > 0.5) & (nd[hs_(h)] >= 0)
        return jnp.where(ok, s_n[hs_(h)] - SLOPES_A[h] * ndf[hs_(h)], NEG)

    sc, sn = head_rows(sel_c), head_rows(sel_n)
    m = jnp.maximum(jnp.max(sc, axis=-1, keepdims=True), jnp.max(sn, axis=-1, keepdims=True))
    ec, en = jnp.exp(sc - m), jnp.exp(sn - m)
    den = jnp.sum(ec, axis=-1, keepdims=True) + jnp.sum(en, axis=-1, keepdims=True)
    os_ref[...] = finish((_dot_nt(ec.astype(BF16), kall[256:512, :]) + _dot(en.astype(BF16), new[:, 768:1024])) / den)
    wpos = (past - wbuf) + lax.broadcasted_iota(jnp.int32, (1, wbuf), 1)
    wd = pos - wpos
    wvalid = (wd <= WIN_A) & (wpos >= 0)
    wdf = wd.astype(F32)
    s_c = _dot(qbd, win_ref[0:256, :].astype(BF16))
    s_n = _dot_nt(qbd, new[:, 1024:1280])
    sc = head_rows(lambda h: jnp.where(wvalid[hs_(h)], s_c[hs_(h)] - SLOPES_A[h] * wdf[hs_(h)], NEG))
    sn = head_rows(lambda h: jnp.where(nd[hs_(h)] >= 0, s_n[hs_(h)] - SLOPES_A[h] * ndf[hs_(h)], NEG))
    m = jnp.maximum(jnp.max(sc, axis=-1, keepdims=True), jnp.max(sn, axis=-1, keepdims=True))
    ec, en = jnp.exp(sc - m), jnp.exp(sn - m)
    den = jnp.sum(ec, axis=-1, keepdims=True) + jnp.sum(en, axis=-1, keepdims=True)
    ow_ref[...] = finish((_dot_nt(ec.astype(BF16), win_ref[256:512, :].astype(BF16)) +
                          _dot(en.astype(BF16), new[:, 1280:1536])) / den)


def _nsa_sample(q_exp, ckv_pool, slc_t, kv_new, win_t, page_table, layer, n, t_new):
    n_pages = page_table.shape[1]
    page = slc_t.shape[3]
    past = n_pages * page
    wbuf = win_t.shape[3]
    per_page = page // CMP_LEN
    tk = past + t_new
    ns = -(-tk // SLC_LEN)
    assert past % CMP_LEN == 0 and t_new < CMP_LEN and t_new & (t_new - 1) == 0 and past >= wbuf
    assert wbuf >= WIN_A and t_new <= SLC_LEN
    hs = HEAD_DIM
    while hs < max(n_pages * per_page // 2, ns):
        hs *= 2
    nslot = 2 * hs
    es = (np.arange(nslot)[:, None] == (np.arange(past + LANES)[None, :] // SLC_LEN)).astype(np.float32)
    es[:, past + t_new:] = 0.0
    es = jnp.asarray(es, BF16)
    ck_specs = [pl.BlockSpec((None, per_page, 512), functools.partial(lambda s, pt, p: (pt[s, p], 0, 0), p=p))
                for p in range(n_pages)]
    sp_specs = [pl.BlockSpec((None, None, 512, page), functools.partial(lambda s, pt, p: (pt[s, p], layer, 0, 0), p=p))
                for p in range(n_pages)]
    grid_spec = pltpu.PrefetchScalarGridSpec(
        num_scalar_prefetch=1,
        grid=(n,),
        in_specs=[pl.BlockSpec((t_new, A_HEADS * LANES), lambda s, pt: (s, 0))] + ck_specs + sp_specs +
                 [pl.BlockSpec((t_new, 3 * A_KVW), lambda s, pt: (s, 0)),
                  pl.BlockSpec((None, None, 512, wbuf), lambda s, pt: (s, layer, 0, 0)),
                  pl.BlockSpec(es.shape, lambda s, pt: (0, 0))],
        out_specs=[pl.BlockSpec((t_new, A_Q), lambda s, pt: (s, 0))] * 3,
        scratch_shapes=[pltpu.VMEM((nslot, 512), F32),
                        pltpu.VMEM((512, past), BF16)],
    )
    return pl.pallas_call(
        functools.partial(_nsa_sample_kernel, n_pages=n_pages, t_new=t_new, past=past, wbuf=wbuf,
                          ns=ns, nslot=nslot),
        grid_spec=grid_spec,
        out_shape=[jax.ShapeDtypeStruct((n * t_new, A_Q), F32)] * 3,
        compiler_params=_cparams(("arbitrary",)),
        name="nsa_sample",
    )(page_table, q_exp, *([ckv_pool] * n_pages), *([slc_t] * n_pages), kv_new, win_t, es)


def _take_cols(w, src):
    src = np.asarray(src)
    cols = jnp.take(w, jnp.asarray(np.maximum(src, 0)), axis=1)
    return jnp.where(jnp.asarray(src >= 0)[None, :], cols, 0.0).astype(BF16)


A_Q0, A_KC0, A_KS0, A_KW0 = 0, A_Q, A_Q + A_KVW, A_Q + 2 * A_KVW
A_GATE0 = A_Q + 3 * A_KVW
A_Z0 = A_GATE0 + 3 * A_HEADS
A_QM0 = A_Z0 + 3 * A_Q
A_ZM0 = A_QM0 + MEM_W
A_QW = A_HEADS * LANES
A_RESTW = 3 * A_Q + 2 * MEM_W + LANES


def _a_q_cols():
    src = []
    for h in range(A_HEADS):
        g = h // A_REP
        slot = [-1] * LANES
        for d in range(HEAD_DIM):
            slot[(g % 2) * HEAD_DIM + d] = A_Q0 + h * HEAD_DIM + d
        src += slot
    return src


def _a_rest_cols():
    return (list(range(A_Z0, A_Z0 + 3 * A_Q)) + list(range(A_QM0, A_QM0 + MEM_W)) +
            list(range(A_ZM0, A_ZM0 + MEM_W)) + list(range(A_GATE0, A_GATE0 + 3 * A_HEADS)) +
            [-1] * (LANES - 3 * A_HEADS))


def _b_q_cols():
    src = []
    for gi in range(N_B_GROUPS):
        for h in range(B_KV):
            for j in range(B_REP):
                slot = [-1] * LANES
                for d in range(HEAD_DIM):
                    slot[(h % 2) * HEAD_DIM + d] = gi * B_O + (h * B_REP + j) * HEAD_DIM + d
                src += slot
    return src


def _blockdiag2(w):
    z = jnp.zeros_like(w)
    top = jnp.concatenate([w, z], axis=-1)
    bot = jnp.concatenate([z, w], axis=-1)
    return jnp.concatenate([top, bot], axis=-2)


def _class_perm(dil):
    p = np.zeros((CM_TILE, CM_TILE), np.float32)
    s = np.arange(CM_TILE)
    p[(s % dil) * (CM_TILE // dil) + s // dil, s] = 1.0
    return p


def kernel(x_prompt, x_sample, mem_prompt, cache_cmp_kv, cache_slc_kv, cache_win_kv, cache_dil_kv, cache_mem_kv,
           page_table, g_pre, g_post, g_mem, w_mem_kv, w_in_a, w_out_a, cmp_pos, cmp_w1, cmp_w2,
           g_kv_b, w_kv_b, w_in_b, w_out_b):
    n_b, s_len, d = x_prompt.shape
    n_s, t_s, _ = x_sample.shape
    depth = g_pre.shape[0]
    n_a = w_in_a.shape[0]
    n_pool, page = cache_cmp_kv.shape[:2]
    mem_len = mem_prompt.shape[1]
    wbuf_a = cache_win_kv.shape[1]
    wbuf_b = cache_dil_kv.shape[1]
    ns_p = s_len // SLC_LEN
    m_p = n_b * s_len
    assert s_len % TM == 0 and mem_len % TM == 0 or mem_len == TM
    spb = s_len // TM
    nkt = s_len // TK
    sb_s = max(c for c in (8, 4, 2, 1) if n_s % c == 0)

    xp = x_prompt.reshape(m_p, d)
    xs = x_sample.reshape(n_s * t_s, d)
    mem2d = mem_prompt.reshape(n_b * mem_len, d)

    egate = np.zeros((LANES, 3 * A_Q), np.float32)
    for b in range(3):
        for h in range(A_HEADS):
            egate[b * A_HEADS + h, b * A_Q + h * HEAD_DIM:b * A_Q + (h + 1) * HEAD_DIM] = 1.0
    egate = jnp.asarray(egate, BF16)
    etile = (np.arange(ns_p)[None, :, None] ==
             (np.arange(nkt)[:, None, None] * TK + np.arange(TK)[None, None, :]) // SLC_LEN)
    etile = jnp.asarray(etile.astype(np.float32), BF16)
    anymat = np.zeros((ns_p, LANES), np.float32)
    anymat[np.arange(ns_p), np.arange(ns_p) // (TK // SLC_LEN)] = 1.0
    anymat = jnp.asarray(anymat, BF16)
    perms = [jnp.asarray(_class_perm(dil), BF16) for _, dil in B_GROUPS[1:]]
    unperms = [jnp.asarray(_class_perm(dil).T, BF16) for _, dil in B_GROUPS[1:]]

    cmp_t = jnp.transpose(cache_cmp_kv, (0, 2, 3, 4, 5, 1)).reshape(n_pool, n_a, A_KVW, page)
    slc_t = jnp.transpose(cache_slc_kv, (0, 2, 3, 4, 5, 1)).reshape(n_pool, n_a, A_KVW, page)
    win_t = jnp.transpose(cache_win_kv, (0, 2, 3, 4, 5, 1)).reshape(n_s, n_a, A_KVW, wbuf_a)
    dil_t = jnp.transpose(cache_dil_kv, (0, 2, 3, 4, 1)).reshape(n_s, 2 * B_KV * HEAD_DIM, wbuf_b)
    mem_t = jnp.transpose(cache_mem_kv, (0, 2, 3, 4, 5, 1)).reshape(n_s, depth, 2 * MEM_W, mem_len)

    a_q_cols, a_rest_cols, b_q_cols = _a_q_cols(), _a_rest_cols(), _b_q_cols()
    row_spec = lambda w: pl.BlockSpec((TM, w), lambda i: (i, 0))

    kct_l, kst_l, kwt_l, kv_s, mkvt_l = [], [], [], [], []
    kvbt_p = kvb_s = None
    kvb_cm = None
    for l in range(depth):
        mkvt = _proj(mem2d, g_mem[l], [w_mem_kv[l].T.astype(BF16)], [],
                     [(True, 0, a, b, 0, (_st_t(0, a),)) for a, b in _chunks(0, 2 * MEM_W)],
                     [jax.ShapeDtypeStruct((n_b, 2 * MEM_W, mem_len), F32)],
                     [_t_spec(2 * MEM_W, mem_len // min(TM, mem_len), min(TM, mem_len))])[0]
        mkvt_l.append(mkvt)
        mkvt4 = mkvt.reshape(n_b, 1, 2 * MEM_W, mem_len)
        if l < n_a:
            w = w_in_a[l]
            w_n = jnp.concatenate([w[:, A_KC0:A_KC0 + A_KVW].astype(BF16), _take_cols(w, a_q_cols),
                                   _take_cols(w, a_rest_cols)], axis=1)
            w_t = w[:, A_KC0:A_KC0 + 3 * A_KVW].T.astype(BF16)
            w_out = w_out_a[l].astype(BF16)
            pos2 = jnp.concatenate([cmp_pos[l], cmp_pos[l]], axis=-1).transpose(1, 0, 2)
            w1bd = _blockdiag2(cmp_w1[l]).transpose(1, 0, 2, 3).astype(BF16)
            w2bd = _blockdiag2(cmp_w2[l]).astype(BF16)
            w2bdt = jnp.swapaxes(w2bd, 1, 2)
            plan = []
            for j in range(3):
                stores = [_st_t(j, 0)]
                if j > 0:
                    stores.append(_st_ttile(3, (j - 1) * A_KVW))
                plan.append((True, 1, j * A_KVW, (j + 1) * A_KVW, 0, tuple(stores)))
            for j in range(4):
                plan.append((False, 0, j * LANES, (j + 1) * LANES, 0, (_st_rows(4 + j, 0),)))
            for a, b in _chunks(A_KVW, A_KVW + A_QW):
                plan.append((False, 0, a, b, 0, (_st_rows(8, a - A_KVW, QSCALE),)))
            for a, b in _chunks(A_KVW + A_QW, A_KVW + A_QW + A_RESTW):
                plan.append((False, 0, a, b, 0, (_st_rows(9, a - A_KVW - A_QW),)))
            qm0 = A_KVW + A_QW + 3 * A_Q
            plan.append((False, 0, qm0, qm0 + MEM_W, 0, (_st_rows(10, 0),)))
            shapes = ([jax.ShapeDtypeStruct((n_b, A_KVW, s_len), F32)] * 3 +
                      [jax.ShapeDtypeStruct((n_b, s_len // LANES, 2 * A_KVW, LANES), BF16)] +
                      [jax.ShapeDtypeStruct((m_p, LANES), F32)] * 4 +
                      [jax.ShapeDtypeStruct((m_p, A_QW), BF16), jax.ShapeDtypeStruct((m_p, A_RESTW), F32),
                       jax.ShapeDtypeStruct((m_p, MEM_W), F32)])
            specs = ([_t_spec(A_KVW, spb, TM)] * 3 +
                     [pl.BlockSpec((None, TM // LANES, 2 * A_KVW, LANES), lambda i: (i // spb, i % spb, 0, 0))] +
                     [row_spec(LANES)] * 4 + [row_spec(A_QW), row_spec(A_RESTW), row_spec(MEM_W)])
            outs = _proj(xp, g_pre[l], [w_n, w_t], [], plan, shapes, specs)
            kct, kst, kwt, kvt = outs[0:4]
            slabs, q, rest, qm = outs[4:8], outs[8], outs[9], outs[10]
            kct_l.append(kct)
            kst_l.append(kst)
            kwt_l.append(kwt)
            ckt, cv = _compress_prompt(slabs, pos2, w1bd, w2bd, w2bdt, n_b, s_len)
            q3 = q.reshape(n_b, s_len, A_QW)
            o_cmp, sel, flags = _nsa_cmp(q3, ckt, cv, anymat, n_b, s_len, TQ)
            flags = (flags[:, :, 0:A_KV, 0:nkt] > 0.5).astype(jnp.int32).reshape(-1)
            o_slc, o_win = _nsa_sw(flags, q3, kvt, sel, etile, n_b, s_len, TQ, TK)
            om = _mem_attend(qm, 0, mkvt4, 0, n_b, s_len, min(512, s_len))
            xp = _merge_a(xp, o_cmp.reshape(-1, A_Q), o_slc.reshape(-1, A_Q), o_win.reshape(-1, A_Q),
                          rest, om, w_out, g_post[l], egate)
            w_rows = jnp.concatenate([w[:, A_KC0:A_KC0 + 3 * A_KVW].astype(BF16), w_n[:, A_KVW:]], axis=1)
            kv, q, rest = _proj_rows(xs, g_pre[l], w_rows,
                                     [(0, 3 * A_KVW, F32, 1.0), (3 * A_KVW, A_QW, BF16, QSCALE),
                                      (3 * A_KVW + A_QW, A_RESTW, F32, 1.0)])
            kv_s.append(kv)
            posr = jnp.tile(pos2.transpose(1, 0, 2), (1, page // CMP_LEN, 1))
            ckv_pool = _compress_pool(cmp_t, l, posr, w1bd, w2bd, 32)
            ckv_pool = ckv_pool.reshape(n_pool, page // CMP_LEN, 512)
            o_cmp, o_slc, o_win = _nsa_sample(q, ckv_pool, slc_t, kv, win_t, page_table, l, n_s, t_s)
            om = _mem_attend(rest, 9, mem_t, l, n_s, t_s, t_s, sb_s)
            xs = _merge_a(xs, o_cmp, o_slc, o_win, rest, om, w_out, g_post[l], egate)
            if l == n_a - 1:
                w_kv = w_kv_b.astype(BF16)
                plan = [(True, 1, 0, 512, 0, (_st_t(0, 0),)),
                        (False, 0, 0, 512, 0, (_st_rows(1, 0),)),
                        (False, 0, 0, 512, 1, (_st_cm(2, 0, B_GROUPS[1][1]),)),
                        (False, 0, 0, 512, 2, (_st_cm(3, 0, B_GROUPS[2][1]),))]
                shapes = [jax.ShapeDtypeStruct((n_b, 512, s_len), F32), jax.ShapeDtypeStruct((m_p, 512), BF16)]
                specs = [_t_spec(512, spb, TM), row_spec(512)]
                for _, dil in B_GROUPS[1:]:
                    shapes.append(jax.ShapeDtypeStruct((m_p // CM_TILE, dil, CM_TILE // dil, 512), BF16))
                    specs.append(_cm_spec(dil, 512))
                kvbt_p, kv0, kv1, kv2 = _proj(xp, g_kv_b, [w_kv, w_kv_b.T.astype(BF16)], perms, plan, shapes, specs)
                kvb_cm = [kv0.reshape(n_b, s_len, 512)] + [
                    a.reshape((n_b, s_len // CM_TILE) + a.shape[1:]) for a in (kv1, kv2)]
                kvb_s = _proj_rows(xs, g_kv_b, w_kv, [(0, 512, F32, 1.0)])[0]
        else:
            lb = l - n_a
            w = w_in_b[lb]
            w_n = jnp.concatenate([_take_cols(w, b_q_cols), w[:, B_Q:].astype(BF16)], axis=1)
            w_out = w_out_b[lb].astype(BF16)
            qw = B_HPG * LANES
            restw = B_O + 2 * MEM_W
            plan, shapes, specs = [], [], []
            for gi, (_, dil) in enumerate(B_GROUPS):
                for a, b in _chunks(gi * qw, (gi + 1) * qw):
                    st = _st_rows(gi, a - gi * qw, QSCALE) if dil == 1 else _st_cm(gi, a - gi * qw, dil, QSCALE)
                    plan.append((False, 0, a, b, gi, (st,)))
                if dil == 1:
                    shapes.append(jax.ShapeDtypeStruct((m_p, qw), BF16))
                    specs.append(row_spec(qw))
                else:
                    shapes.append(jax.ShapeDtypeStruct((m_p // CM_TILE, dil, CM_TILE // dil, qw), BF16))
                    specs.append(_cm_spec(dil, qw))
            for a, b in _chunks(3 * qw, 3 * qw + restw):
                plan.append((False, 0, a, b, 0, (_st_rows(3, a - 3 * qw),)))
            shapes.append(jax.ShapeDtypeStruct((m_p, restw), F32))
            specs.append(row_spec(restw))
            qm0 = 3 * qw + B_O
            plan.append((False, 0, qm0, qm0 + MEM_W, 0, (_st_rows(4, 0),)))
            shapes.append(jax.ShapeDtypeStruct((m_p, MEM_W), F32))
            specs.append(row_spec(MEM_W))
            q0, q1, q2, rest, qm = _proj(xp, g_pre[l], [w_n], perms, plan, shapes, specs)
            qs = [q0.reshape(n_b, s_len, qw)] + [a.reshape((n_b, s_len // CM_TILE) + a.shape[1:]) for a in (q1, q2)]
            outs, lses = [], []
            for gi in range(N_B_GROUPS):
                o, lse = _dilated_prompt(qs[gi], kvb_cm[gi], gi, n_b, s_len)
                if gi == 0:
                    o, lse = o.reshape(m_p, B_O), lse.reshape(m_p, B_O)
                else:
                    o, lse = (a.reshape((m_p // CM_TILE,) + a.shape[2:]) for a in (o, lse))
                outs.append(o)
                lses.append(lse)
            om = _mem_attend(qm, 0, mkvt4, 0, n_b, s_len, min(512, s_len))
            xp = _merge_b(xp, outs, lses, rest, om, w_out, g_post[l], unperms)
            q, rest = _proj_rows(xs, g_pre[l], w_n, [(0, 3 * qw, BF16, QSCALE), (3 * qw, restw, F32, 1.0)])
            outs, lses = _dilated_sample(q, kvb_s, dil_t, n_s, t_s)
            om = _mem_attend(rest, 2, mem_t, l, n_s, t_s, t_s, sb_s)
            xs = _merge_b(xs, outs, lses, rest, om, w_out, g_post[l], [])

    def from_t(arrs, n, t):
        a = jnp.stack(arrs, 1).reshape(n, len(arrs), 2, A_KV, HEAD_DIM, t)
        return jnp.transpose(a, (0, 5, 1, 2, 3, 4))

    def kv_stack(kvs, n, t, j):
        return jnp.stack([k[:, j * A_KVW:(j + 1) * A_KVW].reshape(n, t, 2, A_KV, HEAD_DIM) for k in kvs], 2)

    new_cmp_p = from_t(kct_l, n_b, s_len)
    new_slc_p = from_t(kst_l, n_b, s_len)
    wa = min(WIN_A, s_len)
    new_win_p = from_t([k[:, :, s_len - wa:] for k in kwt_l], n_b, wa)
    new_cmp_s = kv_stack(kv_s, n_s, t_s, 0)
    new_slc_s = kv_stack(kv_s, n_s, t_s, 1)
    win_s = kv_stack(kv_s, n_s, t_s, 2)
    new_win_s = jnp.concatenate([cache_win_kv, win_s], 1)[:, -wbuf_a:]
    wb = min(B_GROUPS[-1][0], s_len)
    new_dil_p = jnp.transpose(kvbt_p[:, :, s_len - wb:].reshape(n_b, 2, B_KV, HEAD_DIM, wb), (0, 4, 1, 2, 3))
    new_dil_s = jnp.concatenate([cache_dil_kv, kvb_s.reshape(n_s, t_s, 2, B_KV, HEAD_DIM)], 1)[:, -wbuf_b:]
    new_mem_p = jnp.transpose(jnp.stack(mkvt_l, 1).reshape(n_b, depth, 2, MEM_HEADS, HEAD_DIM, mem_len),
                              (0, 5, 1, 2, 3, 4))
    return (xp.reshape(n_b, s_len, d), xs.reshape(n_s, t_s, d), new_cmp_p, new_cmp_s, new_slc_p, new_slc_s,
            new_win_p, new_win_s, new_dil_p, new_dil_s, new_mem_p)
```
